```python
import math
import jax, jax.numpy as jnp
from jax import lax
import numpy as np

D_MODEL = 2048
BATCH = 2
SEQ = 4096
DEPTH = 1
DEC_BATCH = 32
DEC_SEQ = 1
PAST_LEN = 8192
PAGE_SIZE = 128

D_A = D_MODEL
A_HEADS = 4
A_HEAD_DIM = D_A // A_HEADS
A_CHUNK = 128
FGATE_BIAS_LO = 3.0
FGATE_BIAS_HI = 6.0
D_B = D_MODEL
B_HEAD_DIM = 128
B_HEADS = D_B // B_HEAD_DIM
MOBA_BLOCK = 256
MOBA_TOPK = 3
MOBA_QCHUNK = 16
ROPE_THETA = 10000.0
LN_EPS = 1e-5
DEEPNORM_ALPHA = (2 * DEPTH) ** 0.25
DEEPNORM_BETA = (8 * DEPTH) ** -0.25
POOL_FACTOR = 1.25
_V_SEGMENTS = (2, 9)
_F_SEGMENT = 6

kernel_name = "hybrid_mlstm_moba_gated_step"


def _col_sizes():
    return [D_A] * 5 + [A_HEADS] * 2 + [D_B] * 4 + [D_MODEL] * 2


def _layernorm(x):
    mu = x.mean(-1, keepdims=True)
    xc = x - mu
    return xc * lax.rsqrt((xc * xc).mean(-1, keepdims=True) + LN_EPS)


def _rope(x, pos):
    half = x.shape[-1] // 2
    freqs = ROPE_THETA ** (-jnp.arange(half, dtype=jnp.float32) / half)
    ang = pos.astype(jnp.float32)[:, None] * freqs[None, :]
    cos = jnp.cos(ang)[None, :, None, :]
    sin = jnp.sin(ang)[None, :, None, :]
    x32 = x.astype(jnp.float32)
    x1, x2 = x32[..., :half], x32[..., half:]
    return jnp.concatenate([x1 * cos - x2 * sin, x1 * sin + x2 * cos], -1).astype(x.dtype)


def _mlstm_chunk(carry, inp):
    C, n, m = carry
    q, k, v, ig, lf = inp
    L = q.shape[2]
    b = jnp.cumsum(lf, axis=-1)
    a = ig - b
    m_t = b + jnp.maximum(m[..., None], lax.cummax(a, axis=a.ndim - 1))
    inter = jnp.exp(b + m[..., None] - m_t)
    causal = jnp.tril(jnp.ones((L, L), dtype=bool))
    log_d = a[..., None, :] + b[..., :, None] - m_t[..., :, None]
    dmat = jnp.exp(jnp.where(causal, log_d, -jnp.inf))
    s = jnp.einsum('bhtd,bhsd->bhts', q, k) * dmat
    num = inter[..., None] * jnp.einsum('bhvk,bhtk->bhtv', C, q) + jnp.einsum('bhts,bhsv->bhtv', s, v)
    den = inter * jnp.einsum('bhk,bhtk->bht', n, q) + s.sum(-1)
    h = num / jnp.maximum(jnp.abs(den), jnp.exp(-m_t))[..., None]
    m_last = m_t[..., -1]
    w_last = jnp.exp(a + b[..., -1:] - m_last[..., None])
    decay = jnp.exp(b[..., -1] + m - m_last)
    C_new = decay[..., None, None] * C + jnp.einsum('bhs,bhsv,bhsk->bhvk', w_last, v, k)
    n_new = decay[..., None] * n + jnp.einsum('bhs,bhsk->bhk', w_last, k)
    return (C_new, n_new, m_last), h


def _mlstm(q, k, v, ig, lf, C0, n0, m0):
    B, T, H, D = q.shape
    L = T if T <= A_CHUNK else math.gcd(T, A_CHUNK)
    nc = T // L

    def to_chunks(t):
        t = t.reshape((B, nc, L, H) + t.shape[3:])
        return jnp.moveaxis(t, (1, 3), (0, 2))

    final, h = lax.scan(_mlstm_chunk, (C0, n0, m0),
                        (to_chunks(q), to_chunks(k), to_chunks(v), to_chunks(ig), to_chunks(lf)))
    h = jnp.moveaxis(h, (0, 2), (1, 3)).reshape(B, T, H, D)
    return h, final


def _moba(q, k_all, v_all, q_pos):
    f32 = jnp.float32
    B, Tq, H, D = q.shape
    Tk = k_all.shape[1]
    nb = -(-Tk // MOBA_BLOCK)
    pad = nb * MOBA_BLOCK - Tk
    kb = jnp.pad(k_all, ((0, 0), (0, pad), (0, 0), (0, 0))).reshape(B, nb, MOBA_BLOCK, H, D)
    vb = jnp.pad(v_all, ((0, 0), (0, pad), (0, 0), (0, 0))).reshape(B, nb, MOBA_BLOCK, H, D)
    kmean = kb.astype(f32).mean(axis=2)
    qblk = q_pos // MOBA_BLOCK
    gate = jnp.einsum('bqhd,bnhd->bhqn', q.astype(f32), kmean)
    past = jnp.arange(nb)[None, :] < qblk[:, None]
    gate = jnp.where(past[None, None], gate, -jnp.inf)
    kk = min(MOBA_TOPK, nb)
    _, top_i = lax.top_k(gate, kk)
    idx = jnp.concatenate([top_i.astype(jnp.int32),
                           jnp.broadcast_to(qblk[None, None, :, None], (B, H, Tq, 1)).astype(jnp.int32)], -1)
    slot_valid = jnp.concatenate([jnp.arange(kk)[None, :] < qblk[:, None],
                                  jnp.ones((Tq, 1), dtype=bool)], -1)
    S = kk + 1
    QC = Tq if Tq <= MOBA_QCHUNK else math.gcd(Tq, MOBA_QCHUNK)
    nq = Tq // QC
    scale = B_HEAD_DIM ** -0.5
    bi = jnp.arange(B)[:, None, None, None]
    hi = jnp.arange(H)[None, :, None, None]

    def attend(args):
        qc, ic, vc, pc = args
        ks = kb[bi, ic, :, hi]
        vs = vb[bi, ic, :, hi]
        kpos = ic[..., None] * MOBA_BLOCK + jnp.arange(MOBA_BLOCK)
        mask = vc[None, None, :, :, None] & (kpos <= pc[None, None, :, None, None])
        s = jnp.einsum('bqhd,bhqsjd->bhqsj', qc, ks, preferred_element_type=f32) * scale
        s = jnp.where(mask, s, -jnp.inf)
        p = jax.nn.softmax(s.reshape(B, H, QC, S * MOBA_BLOCK), axis=-1).reshape(s.shape)
        return jnp.einsum('bhqsj,bhqsjd->bqhd', p, vs, preferred_element_type=f32)

    qs = q.reshape(B, nq, QC, H, D).transpose(1, 0, 2, 3, 4)
    ids = idx.reshape(B, H, nq, QC, S).transpose(2, 0, 1, 3, 4)
    vals = slot_valid.reshape(nq, QC, S)
    poss = q_pos.reshape(nq, QC)
    out = lax.map(attend, (qs, ids, vals, poss))
    return out.transpose(1, 0, 2, 3, 4).reshape(B, Tq, H, D)


def _layer(x, pos, C0, n0, m0, kv_past, w_in, b_in, mh_norm_g, w_proj_a, w_proj_b, w_out, ln_g, ln_b):
    f32 = jnp.float32
    dt = x.dtype
    Bsz, T, _ = x.shape
    h = jnp.einsum('btd,de->bte', x, w_in, preferred_element_type=f32) + b_in.astype(f32)
    offs = np.cumsum(_col_sizes())[:-1].tolist()
    qa, ka, va, oa, za, ia, fa, qb, kb, vb, zb, ga, gb = jnp.split(h, offs, axis=-1)
    heads_a = lambda t: t.reshape(Bsz, T, A_HEADS, A_HEAD_DIM)
    ha, (C1, n1, m1) = _mlstm(heads_a(qa), heads_a(ka) * (A_HEAD_DIM ** -0.5), heads_a(va),
                              ia, jax.nn.log_sigmoid(fa),
                              C0.astype(f32), n0.astype(f32), m0.astype(f32))
    ya = (_layernorm(ha).reshape(Bsz, T, D_A) * mh_norm_g.astype(f32)
          * jax.nn.sigmoid(oa) * jax.nn.silu(za))
    heads_b = lambda t: t.reshape(Bsz, T, B_HEADS, B_HEAD_DIM)
    q_b = _rope(heads_b(qb), pos)
    k_new = _rope(heads_b(kb), pos).astype(dt)
    v_new = heads_b(vb).astype(dt)
    if kv_past is None:
        k_all, v_all = k_new, v_new
    else:
        k_all = jnp.concatenate([kv_past[0].astype(dt), k_new], axis=1)
        v_all = jnp.concatenate([kv_past[1].astype(dt), v_new], axis=1)
    yb = _moba(q_b, k_all, v_all, pos).reshape(Bsz, T, D_B) * jax.nn.silu(zb)
    merged = (jax.nn.sigmoid(ga) * jnp.einsum('bte,ed->btd', ya, w_proj_a, preferred_element_type=f32)
              + jax.nn.sigmoid(gb) * jnp.einsum('bte,ed->btd', yb, w_proj_b, preferred_element_type=f32))
    out = jnp.einsum('btd,de->bte', merged, w_out, preferred_element_type=f32)
    y = _layernorm(DEEPNORM_ALPHA * x.astype(f32) + out) * ln_g.astype(f32) + ln_b.astype(f32)
    return y.astype(dt), (k_new, v_new, C1.astype(dt), n1.astype(dt), m1.astype(dt))


def setup_inputs(seed: int = 0) -> dict:
    key = jax.random.key(seed)
    ks = jax.random.split(key, 16)
    f32 = jnp.float32
    n_pages = PAST_LEN // PAGE_SIZE
    n_phys = int(math.ceil(POOL_FACTOR * DEC_BATCH * n_pages))
    sizes = _col_sizes()
    d_in = sum(sizes)
    col_scale = jnp.concatenate([jnp.full((s,), DEEPNORM_BETA if i in _V_SEGMENTS else 1.0, f32)
                                 for i, s in enumerate(sizes)])
    bias_off = jnp.concatenate([jnp.linspace(FGATE_BIAS_LO, FGATE_BIAS_HI, s, dtype=f32) if i == _F_SEGMENT
                                else jnp.zeros((s,), f32) for i, s in enumerate(sizes)])
    nrm = lambda k, shp: jax.random.normal(k, shp, f32)
    x_prompt = nrm(ks[0], (BATCH, SEQ, D_MODEL))
    x_sample = nrm(ks[1], (DEC_BATCH, DEC_SEQ, D_MODEL))
    cache_k = nrm(ks[2], (DEPTH, n_phys, PAGE_SIZE, B_HEADS, B_HEAD_DIM))
    cache_v = nrm(ks[3], (DEPTH, n_phys, PAGE_SIZE, B_HEADS, B_HEAD_DIM)) * DEEPNORM_BETA
    page_table = jax.random.permutation(ks[4], n_phys)[:DEC_BATCH * n_pages].reshape(DEC_BATCH, n_pages).astype(jnp.int32)
    state_mlstm_C = nrm(ks[5], (DEPTH, DEC_BATCH, A_HEADS, A_HEAD_DIM, A_HEAD_DIM)) * 0.5
    state_mlstm_n = nrm(ks[6], (DEPTH, DEC_BATCH, A_HEADS, A_HEAD_DIM)) * 0.5
    state_mlstm_m = nrm(ks[7], (DEPTH, DEC_BATCH, A_HEADS))
    w_in = nrm(ks[8], (DEPTH, D_MODEL, d_in)) * (D_MODEL ** -0.5) * col_scale
    b_in = nrm(ks[9], (DEPTH, d_in)) * 0.01 + bias_off
    mh_norm_g = 1.0 + 0.02 * nrm(ks[10], (DEPTH, D_A))
    w_proj_a = nrm(ks[11], (DEPTH, D_A, D_MODEL)) * (D_A ** -0.5) * DEEPNORM_BETA
    w_proj_b = nrm(ks[12], (DEPTH, D_B, D_MODEL)) * (D_B ** -0.5) * DEEPNORM_BETA
    w_out = nrm(ks[13], (DEPTH, D_MODEL, D_MODEL)) * (D_MODEL ** -0.5) * DEEPNORM_BETA
    ln_g = 1.0 + 0.02 * nrm(ks[14], (DEPTH, D_MODEL))
    ln_b = 0.02 * nrm(ks[15], (DEPTH, D_MODEL))
    return {"x_prompt": x_prompt, "x_sample": x_sample, "cache_k": cache_k, "cache_v": cache_v,
            "page_table": page_table, "state_mlstm_C": state_mlstm_C, "state_mlstm_n": state_mlstm_n,
            "state_mlstm_m": state_mlstm_m, "w_in": w_in, "b_in": b_in, "mh_norm_g": mh_norm_g,
            "w_proj_a": w_proj_a, "w_proj_b": w_proj_b, "w_out": w_out, "ln_g": ln_g, "ln_b": ln_b}


def reference(x_prompt, x_sample, cache_k, cache_v, page_table, state_mlstm_C, state_mlstm_n, state_mlstm_m,
              w_in, b_in, mh_norm_g, w_proj_a, w_proj_b, w_out, ln_g, ln_b):
    f32 = jnp.float32
    Bp, Tp, _ = x_prompt.shape
    Bs, Ts, _ = x_sample.shape
    n_pages = page_table.shape[1]
    past = n_pages * cache_k.shape[2]
    pos_p = jnp.arange(Tp, dtype=jnp.int32)
    pos_s = past + jnp.arange(Ts, dtype=jnp.int32)
    yp, ys = x_prompt, x_sample
    new_p = [[] for _ in range(5)]
    new_s = [[] for _ in range(5)]
    for l in range(DEPTH):
        params = (w_in[l], b_in[l], mh_norm_g[l], w_proj_a[l], w_proj_b[l], w_out[l], ln_g[l], ln_b[l])
        C0 = jnp.zeros((Bp, A_HEADS, A_HEAD_DIM, A_HEAD_DIM), f32)
        n0 = jnp.zeros((Bp, A_HEADS, A_HEAD_DIM), f32)
        m0 = jnp.zeros((Bp, A_HEADS), f32)
        yp, st_p = _layer(yp, pos_p, C0, n0, m0, None, *params)
        k_past = cache_k[l][page_table].reshape(Bs, past, B_HEADS, B_HEAD_DIM)
        v_past = cache_v[l][page_table].reshape(Bs, past, B_HEADS, B_HEAD_DIM)
        ys, st_s = _layer(ys, pos_s, state_mlstm_C[l], state_mlstm_n[l], state_mlstm_m[l],
                          (k_past, v_past), *params)
        for i in range(5):
            new_p[i].append(st_p[i])
            new_s[i].append(st_s[i])
    k_prompt, v_prompt, C_prompt, n_prompt, m_prompt = [jnp.stack(t) for t in new_p]
    k_sample, v_sample, C_sample, n_sample, m_sample = [jnp.stack(t) for t in new_s]
    return (yp, ys, k_prompt, v_prompt, C_prompt, n_prompt, m_prompt,
            k_sample, v_sample, C_sample, n_sample, m_sample)
```

```python
import functools
import math

import jax
import jax.numpy as jnp
from jax import lax
from jax.experimental import pallas as pl
from jax.experimental.pallas import tpu as pltpu

F32 = jnp.float32
BF16 = jnp.bfloat16

A_HEADS = 4
A_CHUNK = 128
B_HEAD_DIM = 128
MOBA_BLOCK = 256
MOBA_TOPK = 3
ROPE_THETA = 10000.0
LN_EPS = 1e-5
LANES = 128
GATE_ROWS = 8
PAGES_PER_STEP = 8
VMEM_LIMIT = 60 * 1024 * 1024

_NT = (((1,), (1,)), ((), ()))


def _cparams(*sem):
    return pltpu.CompilerParams(dimension_semantics=sem, vmem_limit_bytes=VMEM_LIMIT)


def _log_sigmoid(x):
    return jnp.minimum(x, 0.0) - jnp.log1p(jnp.exp(-jnp.abs(x)))


def _silu(x):
    return x * jax.nn.sigmoid(x)


def _layernorm(x):
    mu = jnp.mean(x, axis=-1, keepdims=True)
    xc = x - mu
    return xc * lax.rsqrt(jnp.mean(xc * xc, axis=-1, keepdims=True) + LN_EPS)


def _inproj_kernel(x_ref, w_ref, b_ref, o_ref, xb_ref):
    @pl.when(pl.program_id(1) == 0)
    def _():
        xb_ref[...] = x_ref[...].astype(BF16)

    o_ref[...] = jnp.dot(xb_ref[...], w_ref[...], preferred_element_type=F32) + b_ref[...]


def _inproj(x, w, b, tm, tn):
    m, k = x.shape
    n = w.shape[1]
    return pl.pallas_call(
        _inproj_kernel,
        grid=(m // tm, n // tn),
        in_specs=[pl.BlockSpec((tm, k), lambda i, j: (i, 0)),
                  pl.BlockSpec((k, tn), lambda i, j: (0, j)),
                  pl.BlockSpec((1, tn), lambda i, j: (0, j))],
        out_specs=pl.BlockSpec((tm, tn), lambda i, j: (i, j)),
        out_shape=jax.ShapeDtypeStruct((m, n), F32),
        scratch_shapes=[pltpu.VMEM((tm, k), BF16)],
        compiler_params=_cparams("parallel", "arbitrary"),
        name="inproj",
    )(x, w, b)


def _gates_kernel(x_ref, wc_ref, wr_ref, bc_ref, br_ref, gc_ref, gr_ref):
    xb = x_ref[...].astype(BF16)
    gc_ref[...] = jnp.dot(xb, wc_ref[...], preferred_element_type=F32) + bc_ref[...]
    gr_ref[...] = lax.dot_general(wr_ref[...], xb, _NT, preferred_element_type=F32) + br_ref[...]


def _gates(x, wc, wr, bc, br, tm):
    m, k = x.shape
    return pl.pallas_call(
        _gates_kernel,
        grid=(m // tm,),
        in_specs=[pl.BlockSpec((tm, k), lambda i: (i, 0)),
                  pl.BlockSpec((k, LANES), lambda i: (0, 0)),
                  pl.BlockSpec((GATE_ROWS, k), lambda i: (0, 0)),
                  pl.BlockSpec((1, LANES), lambda i: (0, 0)),
                  pl.BlockSpec((GATE_ROWS, 1), lambda i: (0, 0))],
        out_specs=[pl.BlockSpec((tm, LANES), lambda i: (i, 0)),
                   pl.BlockSpec((GATE_ROWS, tm), lambda i: (0, i))],
        out_shape=[jax.ShapeDtypeStruct((m, LANES), F32),
                   jax.ShapeDtypeStruct((GATE_ROWS, m), F32)],
        compiler_params=_cparams("parallel"),
        name="gates",
    )(x, wc, wr, bc, br)


def _rope_kernel(q_ref, k_ref, v_ref, cos_ref, sin_ref, qo_ref, ko_ref, kb_ref, vo_ref, vb_ref, km_ref):
    cos = cos_ref[...]
    sin = sin_ref[...]
    tm, d = q_ref.shape
    half = B_HEAD_DIM // 2
    for h in range(d // B_HEAD_DIM):
        sl = slice(h * B_HEAD_DIM, (h + 1) * B_HEAD_DIM)
        q = q_ref[:, sl]
        k = k_ref[:, sl]
        qr = q * cos + pltpu.roll(q, half, 1) * sin
        kr = k * cos + pltpu.roll(k, half, 1) * sin
        qo_ref[:, sl] = qr.astype(qo_ref.dtype)
        ko_ref[:, sl] = kr
        kb_ref[:, sl] = kr.astype(BF16)
        km_ref[:, sl] = jnp.sum(kr, axis=0, keepdims=True) * (1.0 / tm)
    v = v_ref[...]
    vo_ref[...] = v
    vb_ref[...] = v.astype(BF16)


def _rope(h, col_q, col_k, col_v, d, cos, sin, tm, q_dtype):
    m = h.shape[0]
    nt = m // tm
    row = lambda c: pl.BlockSpec((tm, d), lambda i: (i, c))
    full = pl.BlockSpec((tm, d), lambda i: (i, 0))
    tab = pl.BlockSpec((tm, B_HEAD_DIM), lambda i: (i, 0))
    return pl.pallas_call(
        _rope_kernel,
        grid=(nt,),
        in_specs=[row(col_q), row(col_k), row(col_v), tab, tab],
        out_specs=[full, full, full, full, full, pl.BlockSpec((None, 1, d), lambda i: (i, 0, 0))],
        out_shape=[jax.ShapeDtypeStruct((m, d), q_dtype),
                   jax.ShapeDtypeStruct((m, d), F32),
                   jax.ShapeDtypeStruct((m, d), BF16),
                   jax.ShapeDtypeStruct((m, d), F32),
                   jax.ShapeDtypeStruct((m, d), BF16),
                   jax.ShapeDtypeStruct((nt, 1, d), F32)],
        compiler_params=_cparams("parallel"),
        name="rope",
    )(h, h, h, cos, sin)


def _mlstm_kernel(q_ref, k_ref, v_ref, oa_ref, za_ref, gc_ref, gr_ref, ng_ref,
                  ya_ref, c_out, n_out, m_out, c_sc, n_sc, m_sc, *, scale):
    h = pl.program_id(1)
    c = pl.program_id(2)
    nc = pl.num_programs(2)
    L, D = q_ref.shape

    @pl.when(c == 0)
    def _():
        c_sc[...] = jnp.zeros(c_sc.shape, F32)
        n_sc[...] = jnp.zeros(n_sc.shape, F32)
        m_sc[...] = jnp.zeros(m_sc.shape, F32)

    gc = gc_ref[...]
    gr = gr_ref[...]
    lane = lax.broadcasted_iota(jnp.int32, gc.shape, 1)
    sub = lax.broadcasted_iota(jnp.int32, gr.shape, 0)
    ig_col = jnp.sum(jnp.where(lane == h, gc, 0.0), axis=1, keepdims=True)
    fa_col = jnp.sum(jnp.where(lane == h + A_HEADS, gc, 0.0), axis=1, keepdims=True)
    ig_row = jnp.sum(jnp.where(sub == h, gr, 0.0), axis=0, keepdims=True)
    fa_row = jnp.sum(jnp.where(sub == h + A_HEADS, gr, 0.0), axis=0, keepdims=True)
    lf_col = _log_sigmoid(fa_col)
    lf_row = _log_sigmoid(fa_row)

    t_i = lax.broadcasted_iota(jnp.int32, (L, L), 0)
    s_i = lax.broadcasted_iota(jnp.int32, (L, L), 1)
    causal = s_i <= t_i
    b_col = jnp.sum(jnp.where(causal, jnp.broadcast_to(lf_row, (L, L)), 0.0), axis=1, keepdims=True)
    b_row = jnp.sum(jnp.where(t_i <= s_i, jnp.broadcast_to(lf_col, (L, L)), 0.0), axis=0, keepdims=True)
    a_row = ig_row - b_row
    a_col = ig_col - b_col
    cm_col = jnp.max(jnp.where(causal, jnp.broadcast_to(a_row, (L, L)), -jnp.inf), axis=1, keepdims=True)
    m_prev = m_sc[...]
    g_col = jnp.maximum(m_prev, cm_col)
    dmat = jnp.exp(jnp.where(causal, a_row - g_col, -jnp.inf))
    inter = jnp.exp(m_prev - g_col)
    mt_col = b_col + g_col
    g_last = jnp.maximum(m_prev, jnp.max(a_row, axis=1, keepdims=True))
    m_last = jnp.sum(lf_row, axis=1, keepdims=True) + g_last
    w_col = jnp.exp(a_col - g_last)
    decay = jnp.exp(m_prev - g_last)

    q = q_ref[...]
    ks = k_ref[...] * scale
    v = v_ref[...]
    qb = q.astype(BF16)
    s = lax.dot_general(qb, ks.astype(BF16), _NT, preferred_element_type=F32) * dmat
    cmat = c_sc[...]
    num = (inter * lax.dot_general(qb, cmat.astype(BF16), _NT, preferred_element_type=F32)
           + jnp.dot(s.astype(BF16), v.astype(BF16), preferred_element_type=F32))
    nvec = n_sc[...]
    den = inter * jnp.sum(q * nvec, axis=1, keepdims=True) + jnp.sum(s, axis=1, keepdims=True)
    hh = num / jnp.maximum(jnp.abs(den), jnp.exp(-mt_col))

    kw = ks * w_col
    c_sc[...] = decay * cmat + jnp.dot(v.T.astype(BF16), kw.astype(BF16), preferred_element_type=F32)
    n_sc[...] = decay * nvec + jnp.sum(kw, axis=0, keepdims=True)
    m_sc[...] = m_last

    ya = _layernorm(hh) * ng_ref[...] * jax.nn.sigmoid(oa_ref[...]) * _silu(za_ref[...])
    ya_ref[...] = ya.astype(ya_ref.dtype)

    @pl.when(c == nc - 1)
    def _():
        c_out[...] = c_sc[...]
        n_out[...] = n_sc[...]
        m_out[...] = jnp.broadcast_to(m_sc[...], m_out.shape)


def _mlstm_prompt(h, gc, gr, ng, bsz, t, d):
    da = d // A_HEADS
    L = t if t <= A_CHUNK else math.gcd(t, A_CHUNK)
    nc = t // L
    H = A_HEADS
    blk = lambda g: pl.BlockSpec((L, da), lambda b, hh, c: (b * nc + c, g * H + hh))
    return pl.pallas_call(
        functools.partial(_mlstm_kernel, scale=da ** -0.5),
        grid=(bsz, H, nc),
        in_specs=[blk(0), blk(1), blk(2), blk(3), blk(4),
                  pl.BlockSpec((L, LANES), lambda b, hh, c: (b * nc + c, 0)),
                  pl.BlockSpec((GATE_ROWS, L), lambda b, hh, c: (0, b * nc + c)),
                  pl.BlockSpec((1, da), lambda b, hh, c: (0, hh))],
        out_specs=[pl.BlockSpec((L, da), lambda b, hh, c: (b * nc + c, hh)),
                   pl.BlockSpec((None, None, da, da), lambda b, hh, c: (b, hh, 0, 0)),
                   pl.BlockSpec((None, None, 1, da), lambda b, hh, c: (b, hh, 0, 0)),
                   pl.BlockSpec((None, None, 1, LANES), lambda b, hh, c: (b, hh, 0, 0))],
        out_shape=[jax.ShapeDtypeStruct((bsz * t, d), BF16),
                   jax.ShapeDtypeStruct((bsz, H, da, da), F32),
                   jax.ShapeDtypeStruct((bsz, H, 1, da), F32),
                   jax.ShapeDtypeStruct((bsz, H, 1, LANES), F32)],
        scratch_shapes=[pltpu.VMEM((da, da), F32), pltpu.VMEM((1, da), F32), pltpu.VMEM((1, 1), F32)],
        compiler_params=_cparams("parallel", "parallel", "arbitrary"),
        name="mlstm_prompt",
    )(h, h, h, h, h, gc, gr, ng)


def _moba_kernel(q_ref, k_ref, v_ref, km_ref, z_ref, o_ref, m_sc, l_sc, acc_sc, *, scale):
    i = pl.program_id(2)
    blk = q_ref.shape[0]
    nb = km_ref.shape[0]
    q = q_ref[...]
    gate = lax.dot_general(q, km_ref[...].astype(BF16), _NT, preferred_element_type=F32)
    col = lax.broadcasted_iota(jnp.int32, gate.shape, 1)
    past = col < i
    g = jnp.where(past, gate, -jnp.inf)
    rank = jnp.zeros(gate.shape, F32)
    for n2 in range(nb):
        g2 = g[:, n2:n2 + 1]
        beats = (g2 > g) | ((g2 == g) & (n2 < col))
        rank = rank + jnp.where(beats, 1.0, 0.0)
    member = jnp.where(past & (rank < MOBA_TOPK), 1.0, 0.0)

    r_i = lax.broadcasted_iota(jnp.int32, (blk, blk), 0)
    c_i = lax.broadcasted_iota(jnp.int32, (blk, blk), 1)
    own = pl.ds(pl.multiple_of(i * blk, blk), blk)
    s = lax.dot_general(q, k_ref[own, :], _NT, preferred_element_type=F32) * scale
    s = jnp.where(c_i <= r_i, s, -jnp.inf)
    m0 = jnp.max(s, axis=1, keepdims=True)
    p = jnp.exp(s - m0)
    m_sc[...] = m0
    l_sc[...] = jnp.sum(p, axis=1, keepdims=True)
    acc_sc[...] = jnp.dot(p.astype(BF16), v_ref[own, :], preferred_element_type=F32)

    def body(j, carry):
        rows = pl.ds(pl.multiple_of(j * blk, blk), blk)
        sj = lax.dot_general(q, k_ref[rows, :], _NT, preferred_element_type=F32) * scale
        mem_j = jnp.sum(jnp.where(col == j, member, 0.0), axis=1, keepdims=True)
        sj = jnp.where(mem_j > 0.0, sj, -jnp.inf)
        m_old = m_sc[...]
        m_new = jnp.maximum(m_old, jnp.max(sj, axis=1, keepdims=True))
        alpha = jnp.exp(m_old - m_new)
        pj = jnp.exp(sj - m_new)
        l_sc[...] = alpha * l_sc[...] + jnp.sum(pj, axis=1, keepdims=True)
        acc_sc[...] = alpha * acc_sc[...] + jnp.dot(pj.astype(BF16), v_ref[rows, :], preferred_element_type=F32)
        m_sc[...] = m_new
        return carry

    lax.fori_loop(0, i, body, 0)
    o_ref[...] = (acc_sc[...] / l_sc[...] * _silu(z_ref[...])).astype(o_ref.dtype)


def _moba_prompt(qb, kb, vb, kmean, h, col_z, bsz, t, d):
    nh = d // B_HEAD_DIM
    nb = t // MOBA_BLOCK
    blk = MOBA_BLOCK
    return pl.pallas_call(
        functools.partial(_moba_kernel, scale=B_HEAD_DIM ** -0.5),
        grid=(bsz, nh, nb),
        in_specs=[pl.BlockSpec((blk, B_HEAD_DIM), lambda b, hh, i: (b * nb + i, hh)),
                  pl.BlockSpec((t, B_HEAD_DIM), lambda b, hh, i: (b, hh)),
                  pl.BlockSpec((t, B_HEAD_DIM), lambda b, hh, i: (b, hh)),
                  pl.BlockSpec((None, nb, B_HEAD_DIM), lambda b, hh, i: (b, 0, hh)),
                  pl.BlockSpec((blk, B_HEAD_DIM), lambda b, hh, i: (b * nb + i, col_z * nh + hh))],
        out_specs=pl.BlockSpec((blk, B_HEAD_DIM), lambda b, hh, i: (b * nb + i, hh)),
        out_shape=jax.ShapeDtypeStruct((bsz * t, d), BF16),
        scratch_shapes=[pltpu.VMEM((blk, 1), F32), pltpu.VMEM((blk, 1), F32), pltpu.VMEM((blk, B_HEAD_DIM), F32)],
        compiler_params=_cparams("parallel", "parallel", "arbitrary"),
        name="moba_prompt",
    )(qb, kb, vb, kmean, h)


def _merge_kernel(ya_ref, yb_ref, ga_ref, gb_ref, wa_ref, wb_ref, o_ref):
    pa = jnp.dot(ya_ref[...].astype(BF16), wa_ref[...], preferred_element_type=F32)
    pb = jnp.dot(yb_ref[...].astype(BF16), wb_ref[...], preferred_element_type=F32)
    o_ref[...] = (jax.nn.sigmoid(ga_ref[...]) * pa + jax.nn.sigmoid(gb_ref[...]) * pb).astype(o_ref.dtype)


def _merge(ya, yb, h, col_ga, col_gb, wa, wb, tm, tn):
    m, d = ya.shape
    nj = d // tn
    return pl.pallas_call(
        _merge_kernel,
        grid=(m // tm, nj),
        in_specs=[pl.BlockSpec((tm, d), lambda i, j: (i, 0)),
                  pl.BlockSpec((tm, d), lambda i, j: (i, 0)),
                  pl.BlockSpec((tm, tn), lambda i, j: (i, col_ga * nj + j)),
                  pl.BlockSpec((tm, tn), lambda i, j: (i, col_gb * nj + j)),
                  pl.BlockSpec((d, tn), lambda i, j: (0, j)),
                  pl.BlockSpec((d, tn), lambda i, j: (0, j))],
        out_specs=pl.BlockSpec((tm, tn), lambda i, j: (i, j)),
        out_shape=jax.ShapeDtypeStruct((m, d), BF16),
        compiler_params=_cparams("parallel", "arbitrary"),
        name="merge",
    )(ya, yb, h, h, wa, wb)


def _out_kernel(mg_ref, wo_ref, x_ref, g_ref, b_ref, y_ref, *, alpha):
    out = jnp.dot(mg_ref[...], wo_ref[...], preferred_element_type=F32)
    y_ref[...] = _layernorm(alpha * x_ref[...] + out) * g_ref[...] + b_ref[...]


def _out(mg, wo, x, g, b, tm, alpha):
    m, d = x.shape
    row = pl.BlockSpec((tm, d), lambda i: (i, 0))
    vec = pl.BlockSpec((1, d), lambda i: (0, 0))
    return pl.pallas_call(
        functools.partial(_out_kernel, alpha=alpha),
        grid=(m // tm,),
        in_specs=[row, pl.BlockSpec((d, d), lambda i: (0, 0)), row, vec, vec],
        out_specs=row,
        out_shape=jax.ShapeDtypeStruct((m, d), F32),
        compiler_params=_cparams("parallel"),
        name="outproj",
    )(mg, wo, x, g, b)


def _mlstm_step_kernel(q_ref, k_ref, v_ref, vc_ref, oa_ref, za_ref, gc_ref, ng_ref, c_ref, n_ref, m_ref,
                       ya_ref, c_out, n_out, m_out, *, scale):
    h = pl.program_id(1)
    gc = gc_ref[...]
    lane = lax.broadcasted_iota(jnp.int32, gc.shape, 1)
    ig = jnp.sum(jnp.where(lane == h, gc, 0.0), axis=1, keepdims=True)
    fa = jnp.sum(jnp.where(lane == h + A_HEADS, gc, 0.0), axis=1, keepdims=True)
    lf = _log_sigmoid(fa)
    m_prev = m_ref[...][:, :1]
    a = ig - lf
    g = jnp.maximum(m_prev, a)
    m_t = lf + g
    inter = jnp.exp(m_prev - g)
    dm = jnp.exp(a - g)

    q = q_ref[...]
    ks = k_ref[...] * scale
    v = v_ref[...]
    cmat = c_ref[...]
    nvec = n_ref[...]
    q8 = jnp.broadcast_to(q, (8, q.shape[1])).astype(BF16)
    cq = lax.dot_general(q8, cmat.astype(BF16), _NT, preferred_element_type=F32)[0:1]
    s = jnp.sum(q * ks, axis=1, keepdims=True) * dm
    num = inter * cq + s * v
    den = inter * jnp.sum(nvec * q, axis=1, keepdims=True) + s
    hh = num / jnp.maximum(jnp.abs(den), jnp.exp(-m_t))

    c_out[...] = inter * cmat + (dm * vc_ref[...]) * ks
    n_out[...] = inter * nvec + dm * ks
    m_out[...] = jnp.broadcast_to(m_t, m_out.shape)
    ya = _layernorm(hh) * ng_ref[...] * jax.nn.sigmoid(oa_ref[...]) * _silu(za_ref[...])
    ya_ref[...] = ya.astype(ya_ref.dtype)


def _mlstm_step(h3, vcol, gc3, ng, c0, n0, m0, d):
    bs = h3.shape[0]
    H = A_HEADS
    da = d // H
    row = lambda g: pl.BlockSpec((None, 1, da), lambda b, hh: (b, 0, g * H + hh))
    st = lambda r, c: pl.BlockSpec((None, None, r, c), lambda b, hh: (b, hh, 0, 0))
    return pl.pallas_call(
        functools.partial(_mlstm_step_kernel, scale=da ** -0.5),
        grid=(bs, H),
        in_specs=[row(0), row(1), row(2), st(da, 1), row(3), row(4),
                  pl.BlockSpec((None, 1, LANES), lambda b, hh: (b, 0, 0)),
                  pl.BlockSpec((1, da), lambda b, hh: (0, hh)),
                  st(da, da), st(1, da), st(1, LANES)],
        out_specs=[pl.BlockSpec((None, 1, da), lambda b, hh: (b, 0, hh)),
                   st(da, da), st(1, da), st(1, LANES)],
        out_shape=[jax.ShapeDtypeStruct((bs, 1, d), F32),
                   jax.ShapeDtypeStruct((bs, H, da, da), F32),
                   jax.ShapeDtypeStruct((bs, H, 1, da), F32),
                   jax.ShapeDtypeStruct((bs, H, 1, LANES), F32)],
        compiler_params=_cparams("parallel", "parallel"),
        name="mlstm_step",
    )(h3, h3, h3, vcol, h3, h3, gc3, ng, c0, n0, m0)


def _kmean_kernel(pt_ref, *refs):
    o_ref = refs[-1]
    pages = refs[:-1]
    per_blk = len(pages) // o_ref.shape[0]
    rows = per_blk * pages[0].shape[0]
    for r in range(o_ref.shape[0]):
        tot = jnp.sum(pages[r * per_blk][...], axis=0, keepdims=True)
        for p in range(1, per_blk):
            tot = tot + jnp.sum(pages[r * per_blk + p][...], axis=0, keepdims=True)
        o_ref[r:r + 1, :] = tot * (1.0 / rows)


def _kmean_paged(cache3, page_table):
    _, page, d = cache3.shape
    bs, n_pages = page_table.shape
    per_blk = MOBA_BLOCK // page
    pps = min(PAGES_PER_STEP, n_pages)
    steps = n_pages // pps
    bps = pps // per_blk
    spec = lambda p: pl.BlockSpec((None, page, d), lambda b, j, pt: (pt[b, j * pps + p], 0, 0))
    out = pl.pallas_call(
        _kmean_kernel,
        grid_spec=pltpu.PrefetchScalarGridSpec(
            num_scalar_prefetch=1,
            grid=(bs, steps),
            in_specs=[spec(p) for p in range(pps)],
            out_specs=pl.BlockSpec((None, None, bps, d), lambda b, j, pt: (b, j, 0, 0))),
        out_shape=jax.ShapeDtypeStruct((bs, steps, bps, d), F32),
        compiler_params=_cparams("parallel", "arbitrary"),
        name="kmean_paged",
    )(page_table, *([cache3] * pps))
    return out.reshape(bs, steps * bps, d)


def _topk_kernel(q_ref, km_ref, o_ref):
    km = km_ref[...]
    prod = km * q_ref[...]
    nb = km.shape[0]
    nh = km.shape[1] // B_HEAD_DIM
    r_i = lax.broadcasted_iota(jnp.int32, (nb, nb), 0)
    c_i = lax.broadcasted_iota(jnp.int32, (nb, nb), 1)
    n_col = lax.broadcasted_iota(jnp.int32, (nb, 1), 0).astype(F32)
    o_r = lax.broadcasted_iota(jnp.int32, o_ref.shape, 0)
    o_c = lax.broadcasted_iota(jnp.int32, o_ref.shape, 1)
    out = jnp.zeros(o_ref.shape, F32)
    for hh in range(nh):
        g_col = jnp.sum(prod[:, hh * B_HEAD_DIM:(hh + 1) * B_HEAD_DIM], axis=1, keepdims=True)
        g_row = jnp.sum(jnp.where(r_i == c_i, jnp.broadcast_to(g_col, (nb, nb)), 0.0), axis=0, keepdims=True)
        beats = (g_row > g_col) | ((g_row == g_col) & (c_i < r_i))
        rank = jnp.sum(jnp.where(beats, 1.0, 0.0), axis=1, keepdims=True)
        for r in range(MOBA_TOPK):
            idx = jnp.sum(jnp.where(rank == r, n_col, 0.0), axis=0, keepdims=True)
            out = jnp.where((o_r == hh) & (o_c == r), idx, out)
    o_ref[...] = out.astype(jnp.int32)


def _topk_blocks(q3, kmean):
    bs, nb, d = kmean.shape
    nh = d // B_HEAD_DIM
    return pl.pallas_call(
        _topk_kernel,
        grid=(bs,),
        in_specs=[pl.BlockSpec((None, 1, d), lambda b: (b, 0, 0)),
                  pl.BlockSpec((None, nb, d), lambda b: (b, 0, 0))],
        out_specs=pl.BlockSpec((None, nh, LANES), lambda b: (b, 0, 0)),
        out_shape=jax.ShapeDtypeStruct((bs, nh, LANES), jnp.int32),
        compiler_params=_cparams("parallel"),
        name="topk_blocks",
    )(q3, kmean)


def _decode_attn_kernel(pt_ref, ix_ref, *refs, n_pages, scale):
    k_pages = refs[:n_pages]
    v_pages = refs[n_pages:2 * n_pages]
    q_ref, kn_ref, vn_ref, z_ref, o_ref = refs[2 * n_pages:]
    q = q_ref[...]
    scores = [jnp.sum(kp[...] * q, axis=1, keepdims=True) * scale for kp in k_pages]
    s_own = jnp.sum(kn_ref[...] * q, axis=1, keepdims=True) * scale
    m = s_own
    for sc in scores:
        m = jnp.maximum(m, jnp.max(sc, axis=0, keepdims=True))
    e_own = jnp.exp(s_own - m)
    l = e_own
    acc = e_own * vn_ref[...]
    for sc, vp in zip(scores, v_pages):
        e = jnp.exp(sc - m)
        l = l + jnp.sum(e, axis=0, keepdims=True)
        acc = acc + jnp.sum(e * vp[...], axis=0, keepdims=True)
    o_ref[...] = acc / l * _silu(z_ref[...])


def _decode_attn(cache_k3, cache_v3, page_table, idx, q3, kn3, vn3, h3, col_z):
    _, page, d = cache_k3.shape
    bs = page_table.shape[0]
    nh = d // B_HEAD_DIM
    per_blk = MOBA_BLOCK // page
    n_pages = MOBA_TOPK * per_blk

    def page_spec(p):
        r, half = divmod(p, per_blk)
        return pl.BlockSpec((None, page, B_HEAD_DIM),
                            lambda b, hh, pt, ix: (pt[b, ix[b, hh * MOBA_TOPK + r] * per_blk + half], 0, hh))

    vec = lambda off: pl.BlockSpec((None, 1, B_HEAD_DIM), lambda b, hh, pt, ix: (b, 0, off + hh))
    return pl.pallas_call(
        functools.partial(_decode_attn_kernel, n_pages=n_pages, scale=B_HEAD_DIM ** -0.5),
        grid_spec=pltpu.PrefetchScalarGridSpec(
            num_scalar_prefetch=2,
            grid=(bs, nh),
            in_specs=([page_spec(p) for p in range(n_pages)] * 2
                      + [vec(0), vec(0), vec(0), vec(col_z * nh)]),
            out_specs=vec(0)),
        out_shape=jax.ShapeDtypeStruct((bs, 1, d), F32),
        compiler_params=_cparams("parallel", "arbitrary"),
        name="decode_attn",
    )(page_table, idx, *([cache_k3] * n_pages), *([cache_v3] * n_pages), q3, kn3, vn3, h3)


def _rope_tables(pos):
    half = B_HEAD_DIM // 2
    freqs = ROPE_THETA ** (-jnp.arange(half, dtype=F32) / half)
    ang = pos.astype(F32)[:, None] * freqs[None, :]
    cos = jnp.cos(ang)
    sin = jnp.sin(ang)
    return jnp.concatenate([cos, cos], -1), jnp.concatenate([-sin, sin], -1)


def _tile(n, pref):
    return pref if n % pref == 0 else n


def _prep_weights(w_in, b_in, w_proj_a, w_proj_b, w_out, d):
    c0 = 5 * d
    c1 = c0 + 2 * A_HEADS
    w_main = jnp.concatenate([w_in[:, :c0], w_in[:, c1:]], axis=1).astype(BF16)
    b_main = jnp.concatenate([b_in[:c0], b_in[c1:]])[None, :]
    w_g = w_in[:, c0:c1]
    wc = jnp.pad(w_g, ((0, 0), (0, LANES - 2 * A_HEADS))).astype(BF16)
    wr = w_g.T.astype(BF16)
    bc = jnp.pad(b_in[c0:c1], (0, LANES - 2 * A_HEADS))[None, :]
    br = b_in[c0:c1][:, None]
    return (w_main, b_main, wc, wr, bc, br,
            w_proj_a.astype(BF16), w_proj_b.astype(BF16), w_out.astype(BF16))


_QA, _KA, _VA, _OA, _ZA, _QB, _KB, _VB, _ZB, _GA, _GB = range(11)


def _layer_prompt(x, wts, ng, ln_g, ln_b, alpha):
    bsz, t, d = x.shape
    m = bsz * t
    w_main, b_main, wc, wr, bc, br, wa, wb, wo = wts
    x2 = x.reshape(m, d)
    h = _inproj(x2, w_main, b_main, _tile(m, 1024), _tile(w_main.shape[1], 1024))
    gc, gr = _gates(x2, wc, wr, bc, br, _tile(m, 512))
    ya, c1, n1, m1 = _mlstm_prompt(h, gc, gr, ng, bsz, t, d)
    cos, sin = _rope_tables(jnp.arange(t, dtype=jnp.int32))
    cos = jnp.tile(cos, (bsz, 1))
    sin = jnp.tile(sin, (bsz, 1))
    q_b, k_new, k_b, v_new, v_b, kmean = _rope(h, _QB, _KB, _VB, d, cos, sin, MOBA_BLOCK, BF16)
    nb = t // MOBA_BLOCK
    yb = _moba_prompt(q_b, k_b, v_b, kmean.reshape(bsz, nb, d), h, _ZB, bsz, t, d)
    mg = _merge(ya, yb, h, _GA, _GB, wa, wb, _tile(m, 512), _tile(d, 512))
    y = _out(mg, wo, x2, ln_g, ln_b, _tile(m, 256), alpha)
    nh = d // B_HEAD_DIM
    return (y.reshape(bsz, t, d), k_new.reshape(bsz, t, nh, B_HEAD_DIM), v_new.reshape(bsz, t, nh, B_HEAD_DIM),
            c1, n1.reshape(bsz, A_HEADS, d // A_HEADS), m1[:, :, 0, 0])


def _layer_sample(x, wts, ng, ln_g, ln_b, alpha, cache_k, cache_v, page_table, c0, n0, m0):
    bs, t, d = x.shape
    w_main, b_main, wc, wr, bc, br, wa, wb, wo = wts
    n_phys, page, nh, hd = cache_k.shape
    past = page_table.shape[1] * page
    da = d // A_HEADS
    x2 = x.reshape(bs, d)
    h = _inproj(x2, w_main, b_main, bs, _tile(w_main.shape[1], 1024))
    gc, _ = _gates(x2, wc, wr, bc, br, bs)
    h3 = h.reshape(bs, 1, h.shape[1])
    vcol = h[:, _VA * d:(_VA + 1) * d].reshape(bs, A_HEADS, da, 1)
    ya, c1, n1, m1 = _mlstm_step(h3, vcol, gc.reshape(bs, 1, LANES), ng, c0,
                                 n0.reshape(bs, A_HEADS, 1, da),
                                 jnp.broadcast_to(m0[:, :, None, None], (bs, A_HEADS, 1, LANES)), d)
    cos, sin = _rope_tables(jnp.full((bs,), past, dtype=jnp.int32))
    q_r, k_new, _, v_new, _, _ = _rope(h, _QB, _KB, _VB, d, cos, sin, bs, F32)
    cache_k3 = cache_k.reshape(n_phys, page, d)
    cache_v3 = cache_v.reshape(n_phys, page, d)
    kmean = _kmean_paged(cache_k3, page_table)
    q3 = q_r.reshape(bs, 1, d)
    idx = _topk_blocks(q3, kmean)[:, :, :MOBA_TOPK].reshape(bs, nh * MOBA_TOPK)
    yb = _decode_attn(cache_k3, cache_v3, page_table, idx, q3, k_new.reshape(bs, 1, d),
                      v_new.reshape(bs, 1, d), h3, _ZB)
    mg = _merge(ya.reshape(bs, d), yb.reshape(bs, d), h, _GA, _GB, wa, wb, bs, _tile(d, 512))
    y = _out(mg, wo, x2, ln_g, ln_b, bs, alpha)
    return (y.reshape(bs, 1, d), k_new.reshape(bs, 1, nh, hd), v_new.reshape(bs, 1, nh, hd),
            c1, n1.reshape(bs, A_HEADS, da), m1[:, :, 0, 0])


def kernel(x_prompt, x_sample, cache_k, cache_v, page_table, state_mlstm_C, state_mlstm_n, state_mlstm_m,
           w_in, b_in, mh_norm_g, w_proj_a, w_proj_b, w_out, ln_g, ln_b):
    depth = w_in.shape[0]
    d = x_prompt.shape[-1]
    n_pages = page_table.shape[1]
    page = cache_k.shape[2]
    assert x_sample.shape[1] == 1 and d % (A_HEADS * LANES) == 0
    assert x_prompt.shape[1] % MOBA_BLOCK == 0 and MOBA_BLOCK % page == 0
    assert (n_pages * page) % MOBA_BLOCK == 0 and n_pages * page // MOBA_BLOCK >= MOBA_TOPK
    alpha = (2 * depth) ** 0.25
    yp, ys = x_prompt, x_sample
    new_p = [[] for _ in range(5)]
    new_s = [[] for _ in range(5)]
    for l in range(depth):
        wts = _prep_weights(w_in[l], b_in[l], w_proj_a[l], w_proj_b[l], w_out[l], d)
        ng, lg, lb = mh_norm_g[l][None, :], ln_g[l][None, :], ln_b[l][None, :]
        yp, *st_p = _layer_prompt(yp, wts, ng, lg, lb, alpha)
        ys, *st_s = _layer_sample(ys, wts, ng, lg, lb, alpha, cache_k[l], cache_v[l], page_table,
                                  state_mlstm_C[l], state_mlstm_n[l], state_mlstm_m[l])
        for i in range(5):
            new_p[i].append(st_p[i])
            new_s[i].append(st_s[i])
    k_p, v_p, c_p, n_p, m_p = [jnp.stack(t) for t in new_p]
    k_s, v_s, c_s, n_s, m_s = [jnp.stack(t) for t in new_s]
    return (yp, ys, k_p, v_p, c_p, n_p, m_p, k_s, v_s, c_s, n_s, m_s)
```

```python
import functools
import math

import jax
import jax.numpy as jnp
from jax import lax
from jax.experimental import pallas as pl
from jax.experimental.pallas import tpu as pltpu

F32 = jnp.float32
BF16 = jnp.bfloat16

A_HEADS = 4
A_CHUNK = 128
B_HEAD_DIM = 128
MOBA_BLOCK = 256
MOBA_TOPK = 3
ROPE_THETA = 10000.0
LN_EPS = 1e-5
LANES = 128
GATE_ROWS = 8
PAGES_PER_STEP = 8
MOBA_HEADS_PER_STEP = 4
VMEM_LIMIT = 60 * 1024 * 1024

_NT = (((1,), (1,)), ((), ()))


def _cparams(*sem):
    return pltpu.CompilerParams(dimension_semantics=sem, vmem_limit_bytes=VMEM_LIMIT)


def _log_sigmoid(x):
    return jnp.minimum(x, 0.0) - jnp.log1p(jnp.exp(-jnp.abs(x)))


def _silu(x):
    return x * jax.nn.sigmoid(x)


def _layernorm(x):
    mu = jnp.mean(x, axis=-1, keepdims=True)
    xc = x - mu
    return xc * lax.rsqrt(jnp.mean(xc * xc, axis=-1, keepdims=True) + LN_EPS)


def _inproj_kernel(x_ref, w_ref, b_ref, o_ref, wb_ref):
    @pl.when(pl.program_id(1) == 0)
    def _():
        wb_ref[...] = w_ref[...].astype(BF16)

    o_ref[...] = jnp.dot(x_ref[...], wb_ref[...], preferred_element_type=F32) + b_ref[...]


def _inproj(xb, w3, layer, n, b, tm, tn):
    m, k = xb.shape
    return pl.pallas_call(
        _inproj_kernel,
        grid=(n // tn, m // tm),
        in_specs=[pl.BlockSpec((tm, k), lambda j, i: (i, 0)),
                  pl.BlockSpec((None, k, tn), lambda j, i: (layer, 0, j)),
                  pl.BlockSpec((1, tn), lambda j, i: (0, j))],
        out_specs=pl.BlockSpec((tm, tn), lambda j, i: (i, j)),
        out_shape=jax.ShapeDtypeStruct((m, n), F32),
        scratch_shapes=[pltpu.VMEM((k, tn), BF16)],
        compiler_params=_cparams("parallel", "arbitrary"),
        name="inproj",
    )(xb, w3, b)


def _gates_kernel(x_ref, wc_ref, wr_ref, bc_ref, br_ref, xb_ref, gc_ref, gr_ref):
    xb = x_ref[...].astype(BF16)
    xb_ref[...] = xb
    gc_ref[...] = jnp.dot(xb, wc_ref[...], preferred_element_type=F32) + bc_ref[...]
    gr_ref[...] = lax.dot_general(wr_ref[...], xb, _NT, preferred_element_type=F32) + br_ref[...]


def _gates(x, wc, wr, bc, br, tm):
    m, k = x.shape
    return pl.pallas_call(
        _gates_kernel,
        grid=(m // tm,),
        in_specs=[pl.BlockSpec((tm, k), lambda i: (i, 0)),
                  pl.BlockSpec((k, LANES), lambda i: (0, 0)),
                  pl.BlockSpec((GATE_ROWS, k), lambda i: (0, 0)),
                  pl.BlockSpec((1, LANES), lambda i: (0, 0)),
                  pl.BlockSpec((GATE_ROWS, 1), lambda i: (0, 0))],
        out_specs=[pl.BlockSpec((tm, k), lambda i: (i, 0)),
                   pl.BlockSpec((tm, LANES), lambda i: (i, 0)),
                   pl.BlockSpec((GATE_ROWS, tm), lambda i: (0, i))],
        out_shape=[jax.ShapeDtypeStruct((m, k), BF16),
                   jax.ShapeDtypeStruct((m, LANES), F32),
                   jax.ShapeDtypeStruct((GATE_ROWS, m), F32)],
        compiler_params=_cparams("parallel"),
        name="gates",
    )(x, wc, wr, bc, br)


def _rotate(x, cos, sin):
    return x * cos + pltpu.roll(x, B_HEAD_DIM // 2, 1) * sin


def _rope_kernel(q_ref, k_ref, v_ref, cos_ref, sin_ref, qt_ref, ko_ref, kb_ref, vo_ref, vt_ref, km_ref, *, scale):
    cos = cos_ref[...]
    sin = sin_ref[...]
    tm, d = q_ref.shape
    for h in range(d // B_HEAD_DIM):
        sl = slice(h * B_HEAD_DIM, (h + 1) * B_HEAD_DIM)
        qr = _rotate(q_ref[:, sl], cos, sin)
        kr = _rotate(k_ref[:, sl], cos, sin)
        v = v_ref[:, sl]
        qt_ref[sl, :] = (qr * scale).T.astype(BF16)
        ko_ref[:, sl] = kr
        kb_ref[:, sl] = kr.astype(BF16)
        km_ref[:, sl] = jnp.sum(kr, axis=0, keepdims=True) * (1.0 / tm)
        vo_ref[:, sl] = v
        vt_ref[sl, :] = v.T.astype(BF16)


def _rope(h, col_q, col_k, col_v, d, cos, sin, scale):
    m = h.shape[0]
    tm = MOBA_BLOCK
    nt = m // tm
    row = lambda c: pl.BlockSpec((tm, d), lambda i: (i, c))
    full = pl.BlockSpec((tm, d), lambda i: (i, 0))
    tab = pl.BlockSpec((tm, B_HEAD_DIM), lambda i: (i, 0))
    tr = pl.BlockSpec((None, d, tm), lambda i: (i, 0, 0))
    return pl.pallas_call(
        functools.partial(_rope_kernel, scale=scale),
        grid=(nt,),
        in_specs=[row(col_q), row(col_k), row(col_v), tab, tab],
        out_specs=[tr, full, full, full, tr, pl.BlockSpec((None, 1, d), lambda i: (i, 0, 0))],
        out_shape=[jax.ShapeDtypeStruct((nt, d, tm), BF16),
                   jax.ShapeDtypeStruct((m, d), F32),
                   jax.ShapeDtypeStruct((m, d), BF16),
                   jax.ShapeDtypeStruct((m, d), F32),
                   jax.ShapeDtypeStruct((nt, d, tm), BF16),
                   jax.ShapeDtypeStruct((nt, 1, d), F32)],
        compiler_params=_cparams("parallel"),
        name="rope",
    )(h, h, h, cos, sin)


def _rope_rows_kernel(q_ref, k_ref, v_ref, cos_ref, sin_ref, qo_ref, ko_ref, vo_ref):
    cos = cos_ref[...]
    sin = sin_ref[...]
    for h in range(q_ref.shape[1] // B_HEAD_DIM):
        sl = slice(h * B_HEAD_DIM, (h + 1) * B_HEAD_DIM)
        qo_ref[:, sl] = _rotate(q_ref[:, sl], cos, sin)
        ko_ref[:, sl] = _rotate(k_ref[:, sl], cos, sin)
    vo_ref[...] = v_ref[...]


def _rope_rows(h, col_q, col_k, col_v, d, cos, sin):
    m = h.shape[0]
    row = lambda c: pl.BlockSpec((m, d), lambda i: (0, c))
    full = pl.BlockSpec((m, d), lambda i: (0, 0))
    tab = pl.BlockSpec((m, B_HEAD_DIM), lambda i: (0, 0))
    return pl.pallas_call(
        _rope_rows_kernel,
        grid=(1,),
        in_specs=[row(col_q), row(col_k), row(col_v), tab, tab],
        out_specs=[full, full, full],
        out_shape=[jax.ShapeDtypeStruct((m, d), F32)] * 3,
        compiler_params=_cparams("arbitrary"),
        name="rope_rows",
    )(h, h, h, cos, sin)


def _mlstm_kernel(q_ref, k_ref, v_ref, oa_ref, za_ref, gc_ref, gr_ref, ng_ref,
                  ya_ref, c_out, n_out, m_out, c_sc, n_sc, m_sc, *, scale):
    h = pl.program_id(1)
    c = pl.program_id(2)
    nc = pl.num_programs(2)
    L, D = q_ref.shape

    @pl.when(c == 0)
    def _():
        c_sc[...] = jnp.zeros(c_sc.shape, F32)
        n_sc[...] = jnp.zeros(n_sc.shape, F32)
        m_sc[...] = jnp.zeros(m_sc.shape, F32)

    gc = gc_ref[...]
    gr = gr_ref[...]
    lane = lax.broadcasted_iota(jnp.int32, gc.shape, 1)
    sub = lax.broadcasted_iota(jnp.int32, gr.shape, 0)
    ig_col = jnp.sum(jnp.where(lane == h, gc, 0.0), axis=1, keepdims=True)
    fa_col = jnp.sum(jnp.where(lane == h + A_HEADS, gc, 0.0), axis=1, keepdims=True)
    ig_row = jnp.sum(jnp.where(sub == h, gr, 0.0), axis=0, keepdims=True)
    fa_row = jnp.sum(jnp.where(sub == h + A_HEADS, gr, 0.0), axis=0, keepdims=True)
    lf_col = _log_sigmoid(fa_col)
    lf_row = _log_sigmoid(fa_row)

    t_i = lax.broadcasted_iota(jnp.int32, (L, L), 0)
    s_i = lax.broadcasted_iota(jnp.int32, (L, L), 1)
    causal = s_i <= t_i
    b_col = jnp.sum(jnp.where(causal, jnp.broadcast_to(lf_row, (L, L)), 0.0), axis=1, keepdims=True)
    b_row = jnp.sum(jnp.where(t_i <= s_i, jnp.broadcast_to(lf_col, (L, L)), 0.0), axis=0, keepdims=True)
    a_row = ig_row - b_row
    a_col = ig_col - b_col
    cm_col = jnp.max(jnp.where(causal, jnp.broadcast_to(a_row, (L, L)), -jnp.inf), axis=1, keepdims=True)
    m_prev = m_sc[...]
    g_col = jnp.maximum(m_prev, cm_col)
    dmat = jnp.exp(jnp.where(causal, a_row - g_col, -jnp.inf))
    inter = jnp.exp(m_prev - g_col)
    mt_col = b_col + g_col
    g_last = jnp.maximum(m_prev, jnp.max(a_row, axis=1, keepdims=True))
    m_last = jnp.sum(lf_row, axis=1, keepdims=True) + g_last
    w_col = jnp.exp(a_col - g_last)
    decay = jnp.exp(m_prev - g_last)

    q = q_ref[...]
    ks = k_ref[...] * scale
    v = v_ref[...]
    qb = q.astype(BF16)
    s = lax.dot_general(qb, ks.astype(BF16), _NT, preferred_element_type=F32) * dmat
    cmat = c_sc[...]
    num = (inter * lax.dot_general(qb, cmat.astype(BF16), _NT, preferred_element_type=F32)
           + jnp.dot(s.astype(BF16), v.astype(BF16), preferred_element_type=F32))
    nvec = n_sc[...]
    den = inter * jnp.sum(q * nvec, axis=1, keepdims=True) + jnp.sum(s, axis=1, keepdims=True)
    hh = num / jnp.maximum(jnp.abs(den), jnp.exp(-mt_col))

    kw = ks * w_col
    c_sc[...] = decay * cmat + jnp.dot(v.T.astype(BF16), kw.astype(BF16), preferred_element_type=F32)
    n_sc[...] = decay * nvec + jnp.sum(kw, axis=0, keepdims=True)
    m_sc[...] = m_last

    ya = _layernorm(hh) * ng_ref[...] * jax.nn.sigmoid(oa_ref[...]) * _silu(za_ref[...])
    ya_ref[...] = ya.astype(ya_ref.dtype)

    @pl.when(c == nc - 1)
    def _():
        c_out[...] = c_sc[...]
        n_out[...] = n_sc[...]
        m_out[...] = jnp.broadcast_to(m_sc[...], m_out.shape)


def _mlstm_prompt(h, gc, gr, ng, bsz, t, d):
    da = d // A_HEADS
    L = t if t <= A_CHUNK else math.gcd(t, A_CHUNK)
    nc = t // L
    H = A_HEADS
    blk = lambda g: pl.BlockSpec((L, da), lambda b, hh, c: (b * nc + c, g * H + hh))
    return pl.pallas_call(
        functools.partial(_mlstm_kernel, scale=da ** -0.5),
        grid=(bsz, H, nc),
        in_specs=[blk(0), blk(1), blk(2), blk(3), blk(4),
                  pl.BlockSpec((L, LANES), lambda b, hh, c: (b * nc + c, 0)),
                  pl.BlockSpec((GATE_ROWS, L), lambda b, hh, c: (0, b * nc + c)),
                  pl.BlockSpec((1, da), lambda b, hh, c: (0, hh))],
        out_specs=[pl.BlockSpec((L, da), lambda b, hh, c: (b * nc + c, hh)),
                   pl.BlockSpec((None, None, da, da), lambda b, hh, c: (b, hh, 0, 0)),
                   pl.BlockSpec((None, None, 1, da), lambda b, hh, c: (b, hh, 0, 0)),
                   pl.BlockSpec((None, None, 1, LANES), lambda b, hh, c: (b, hh, 0, 0))],
        out_shape=[jax.ShapeDtypeStruct((bsz * t, d), BF16),
                   jax.ShapeDtypeStruct((bsz, H, da, da), F32),
                   jax.ShapeDtypeStruct((bsz, H, 1, da), F32),
                   jax.ShapeDtypeStruct((bsz, H, 1, LANES), F32)],
        scratch_shapes=[pltpu.VMEM((da, da), F32), pltpu.VMEM((1, da), F32), pltpu.VMEM((1, 1), F32)],
        compiler_params=_cparams("parallel", "parallel", "arbitrary"),
        name="mlstm_prompt",
    )(h, h, h, h, h, gc, gr, ng)


def _moba_kernel(qt_ref, k_ref, vt_ref, km_ref, z_ref, o_ref, mem_sc, acc_sc):
    i = pl.program_id(2)
    blk = qt_ref.shape[1]
    nb = km_ref.shape[0]
    hd = B_HEAD_DIM
    heads = range(qt_ref.shape[0] // hd)
    hsl = [slice(h * hd, (h + 1) * hd) for h in heads]
    qts = [qt_ref[sl, :] for sl in hsl]

    row = lax.broadcasted_iota(jnp.int32, (nb, blk), 0)
    past = row < i
    for h in heads:
        gate = jnp.dot(km_ref[:, hsl[h]].astype(BF16), qts[h], preferred_element_type=F32)
        g = jnp.where(past, gate, -jnp.inf)
        rank = jnp.zeros(gate.shape, F32)
        for n2 in range(nb):
            g2 = g[n2:n2 + 1, :]
            beats = (g2 > g) | ((g2 == g) & (n2 < row))
            rank = rank + jnp.where(beats, 1.0, 0.0)
        mem_sc[h] = jnp.where(past & (rank < MOBA_TOPK), 1.0, 0.0)

    key_i = lax.broadcasted_iota(jnp.int32, (blk, blk), 0)
    qry_i = lax.broadcasted_iota(jnp.int32, (blk, blk), 1)
    own = pl.ds(pl.multiple_of(i * blk, blk), blk)
    m0, l0 = [], []
    for h in heads:
        s = jnp.dot(k_ref[own, hsl[h]], qts[h], preferred_element_type=F32)
        s = jnp.where(key_i <= qry_i, s, -jnp.inf)
        m = jnp.max(s, axis=0, keepdims=True)
        p = jnp.exp(s - m)
        m0.append(m)
        l0.append(jnp.sum(p, axis=0, keepdims=True))
        acc_sc[h] = jnp.dot(vt_ref[i, hsl[h], :], p.astype(BF16), preferred_element_type=F32)

    def body(jj, carry):
        ms, ls = carry
        j0 = 2 * jj
        rows0 = pl.ds(pl.multiple_of(j0 * blk, blk), blk)
        rows1 = pl.ds(pl.multiple_of(j0 * blk + blk, blk), blk)
        new_m, new_l = [], []
        for h in heads:
            s0 = jnp.dot(k_ref[rows0, hsl[h]], qts[h], preferred_element_type=F32)
            s1 = jnp.dot(k_ref[rows1, hsl[h]], qts[h], preferred_element_type=F32)
            s0 = jnp.where(mem_sc[h, pl.ds(j0, 1), :] > 0.0, s0, -jnp.inf)
            s1 = jnp.where(mem_sc[h, pl.ds(j0 + 1, 1), :] > 0.0, s1, -jnp.inf)
            m_new = jnp.maximum(ms[h], jnp.maximum(jnp.max(s0, axis=0, keepdims=True),
                                                   jnp.max(s1, axis=0, keepdims=True)))
            alpha = jnp.exp(ms[h] - m_new)
            p0 = jnp.exp(s0 - m_new)
            p1 = jnp.exp(s1 - m_new)
            new_l.append(alpha * ls[h] + jnp.sum(p0, axis=0, keepdims=True) + jnp.sum(p1, axis=0, keepdims=True))
            acc_sc[h] = (alpha * acc_sc[h]
                         + jnp.dot(vt_ref[j0, hsl[h], :], p0.astype(BF16), preferred_element_type=F32)
                         + jnp.dot(vt_ref[j0 + 1, hsl[h], :], p1.astype(BF16), preferred_element_type=F32))
            new_m.append(m_new)
        return tuple(new_m), tuple(new_l)

    _, l_fin = lax.fori_loop(0, (i + 1) // 2, body, (tuple(m0), tuple(l0)))
    for h in heads:
        o_ref[:, hsl[h]] = ((acc_sc[h] / l_fin[h]).T * _silu(z_ref[:, hsl[h]])).astype(o_ref.dtype)


def _moba_prompt(qt, kb, vt, kmean, h, col_z, bsz, t, d):
    nh = d // B_HEAD_DIM
    nb = t // MOBA_BLOCK
    blk = MOBA_BLOCK
    hps = MOBA_HEADS_PER_STEP if nh % MOBA_HEADS_PER_STEP == 0 else 1
    w = hps * B_HEAD_DIM
    ng = nh // hps
    return pl.pallas_call(
        _moba_kernel,
        grid=(bsz, ng, nb),
        in_specs=[pl.BlockSpec((None, w, blk), lambda b, g, i: (b * nb + i, g, 0)),
                  pl.BlockSpec((t, w), lambda b, g, i: (b, g)),
                  pl.BlockSpec((nb, w, blk), lambda b, g, i: (b, g, 0)),
                  pl.BlockSpec((None, nb, w), lambda b, g, i: (b, 0, g)),
                  pl.BlockSpec((blk, w), lambda b, g, i: (b * nb + i, col_z * ng + g))],
        out_specs=pl.BlockSpec((blk, w), lambda b, g, i: (b * nb + i, g)),
        out_shape=jax.ShapeDtypeStruct((bsz * t, d), BF16),
        scratch_shapes=[pltpu.VMEM((hps, nb, blk), F32), pltpu.VMEM((hps, B_HEAD_DIM, blk), F32)],
        compiler_params=_cparams("parallel", "parallel", "arbitrary"),
        name="moba_prompt",
    )(qt, kb, vt, kmean, h)


def _merge_kernel(ya_ref, yb_ref, ga_ref, gb_ref, wa_ref, wb_ref, o_ref):
    pa = jnp.dot(ya_ref[...].astype(BF16), wa_ref[...], preferred_element_type=F32)
    pb = jnp.dot(yb_ref[...].astype(BF16), wb_ref[...], preferred_element_type=F32)
    o_ref[...] = (jax.nn.sigmoid(ga_ref[...]) * pa + jax.nn.sigmoid(gb_ref[...]) * pb).astype(o_ref.dtype)


def _merge(ya, yb, h, col_ga, col_gb, wa, wb, tm, tn):
    m, d = ya.shape
    nj = d // tn
    return pl.pallas_call(
        _merge_kernel,
        grid=(m // tm, nj),
        in_specs=[pl.BlockSpec((tm, d), lambda i, j: (i, 0)),
                  pl.BlockSpec((tm, d), lambda i, j: (i, 0)),
                  pl.BlockSpec((tm, tn), lambda i, j: (i, col_ga * nj + j)),
                  pl.BlockSpec((tm, tn), lambda i, j: (i, col_gb * nj + j)),
                  pl.BlockSpec((d, tn), lambda i, j: (0, j)),
                  pl.BlockSpec((d, tn), lambda i, j: (0, j))],
        out_specs=pl.BlockSpec((tm, tn), lambda i, j: (i, j)),
        out_shape=jax.ShapeDtypeStruct((m, d), BF16),
        compiler_params=_cparams("parallel", "arbitrary"),
        name="merge",
    )(ya, yb, h, h, wa, wb)


def _out_kernel(mg_ref, wo_ref, x_ref, g_ref, b_ref, y_ref, *, alpha):
    out = jnp.dot(mg_ref[...], wo_ref[...], preferred_element_type=F32)
    y_ref[...] = _layernorm(alpha * x_ref[...] + out) * g_ref[...] + b_ref[...]


def _out(mg, wo, x, g, b, tm, alpha):
    m, d = x.shape
    row = pl.BlockSpec((tm, d), lambda i: (i, 0))
    vec = pl.BlockSpec((1, d), lambda i: (0, 0))
    return pl.pallas_call(
        functools.partial(_out_kernel, alpha=alpha),
        grid=(m // tm,),
        in_specs=[row, pl.BlockSpec((d, d), lambda i: (0, 0)), row, vec, vec],
        out_specs=row,
        out_shape=jax.ShapeDtypeStruct((m, d), F32),
        compiler_params=_cparams("parallel"),
        name="outproj",
    )(mg, wo, x, g, b)


def _mlstm_step_kernel(q_ref, k_ref, v_ref, vc_ref, oa_ref, za_ref, gc_ref, ng_ref, c_ref, n_ref, m_ref,
                       ya_ref, c_out, n_out, m_out, *, scale):
    h = pl.program_id(1)
    gc = gc_ref[...]
    lane = lax.broadcasted_iota(jnp.int32, gc.shape, 1)
    ig = jnp.sum(jnp.where(lane == h, gc, 0.0), axis=1, keepdims=True)
    fa = jnp.sum(jnp.where(lane == h + A_HEADS, gc, 0.0), axis=1, keepdims=True)
    lf = _log_sigmoid(fa)
    m_prev = m_ref[...][:, :1]
    a = ig - lf
    g = jnp.maximum(m_prev, a)
    m_t = lf + g
    inter = jnp.exp(m_prev - g)
    dm = jnp.exp(a - g)

    q = q_ref[...]
    ks = k_ref[...] * scale
    v = v_ref[...]
    cmat = c_ref[...]
    nvec = n_ref[...]
    q8 = jnp.broadcast_to(q, (8, q.shape[1])).astype(BF16)
    cq = lax.dot_general(q8, cmat.astype(BF16), _NT, preferred_element_type=F32)[0:1]
    s = jnp.sum(q * ks, axis=1, keepdims=True) * dm
    num = inter * cq + s * v
    den = inter * jnp.sum(nvec * q, axis=1, keepdims=True) + s
    hh = num / jnp.maximum(jnp.abs(den), jnp.exp(-m_t))

    c_out[...] = inter * cmat + (dm * vc_ref[...]) * ks
    n_out[...] = inter * nvec + dm * ks
    m_out[...] = jnp.broadcast_to(m_t, m_out.shape)
    ya = _layernorm(hh) * ng_ref[...] * jax.nn.sigmoid(oa_ref[...]) * _silu(za_ref[...])
    ya_ref[...] = ya.astype(ya_ref.dtype)


def _mlstm_step(h3, vcol, gc3, ng, c_all, c_row0, n0, m0, d):
    bs = h3.shape[0]
    H = A_HEADS
    da = d // H
    row = lambda g: pl.BlockSpec((None, 1, da), lambda b, hh: (b, 0, g * H + hh))
    st = lambda r, c: pl.BlockSpec((None, None, r, c), lambda b, hh: (b, hh, 0, 0))
    return pl.pallas_call(
        functools.partial(_mlstm_step_kernel, scale=da ** -0.5),
        grid=(bs, H),
        in_specs=[row(0), row(1), row(2), st(da, 1), row(3), row(4),
                  pl.BlockSpec((None, 1, LANES), lambda b, hh: (b, 0, 0)),
                  pl.BlockSpec((1, da), lambda b, hh: (0, hh)),
                  pl.BlockSpec((None, None, da, da), lambda b, hh: (c_row0 + b, hh, 0, 0)),
                  st(1, da), st(1, LANES)],
        out_specs=[pl.BlockSpec((None, 1, da), lambda b, hh: (b, 0, hh)),
                   st(da, da), st(1, da), st(1, LANES)],
        out_shape=[jax.ShapeDtypeStruct((bs, 1, d), F32),
                   jax.ShapeDtypeStruct((bs, H, da, da), F32),
                   jax.ShapeDtypeStruct((bs, H, 1, da), F32),
                   jax.ShapeDtypeStruct((bs, H, 1, LANES), F32)],
        compiler_params=_cparams("parallel", "parallel"),
        name="mlstm_step",
    )(h3, h3, h3, vcol, h3, h3, gc3, ng, c_all, n0, m0)


def _kmean_kernel(pt_ref, *refs):
    o_ref = refs[-1]
    pages = refs[:-1]
    per_blk = len(pages) // o_ref.shape[0]
    rows = per_blk * pages[0].shape[0]
    for r in range(o_ref.shape[0]):
        tot = jnp.sum(pages[r * per_blk][...], axis=0, keepdims=True)
        for p in range(1, per_blk):
            tot = tot + jnp.sum(pages[r * per_blk + p][...], axis=0, keepdims=True)
        o_ref[r:r + 1, :] = tot * (1.0 / rows)


def _kmean_paged(cache3, page_table):
    _, page, d = cache3.shape
    bs, n_pages = page_table.shape
    per_blk = MOBA_BLOCK // page
    pps = min(PAGES_PER_STEP, n_pages)
    steps = n_pages // pps
    bps = pps // per_blk
    spec = lambda p: pl.BlockSpec((None, page, d), lambda b, j, pt: (pt[b, j * pps + p], 0, 0))
    out = pl.pallas_call(
        _kmean_kernel,
        grid_spec=pltpu.PrefetchScalarGridSpec(
            num_scalar_prefetch=1,
            grid=(bs, steps),
            in_specs=[spec(p) for p in range(pps)],
            out_specs=pl.BlockSpec((None, None, bps, d), lambda b, j, pt: (b, j, 0, 0))),
        out_shape=jax.ShapeDtypeStruct((bs, steps, bps, d), F32),
        compiler_params=_cparams("parallel", "arbitrary"),
        name="kmean_paged",
    )(page_table, *([cache3] * pps))
    return out.reshape(bs, steps * bps, d)


def _topk_kernel(q_ref, km_ref, o_ref):
    km = km_ref[...]
    prod = km * q_ref[...]
    nb = km.shape[0]
    nh = km.shape[1] // B_HEAD_DIM
    r_i = lax.broadcasted_iota(jnp.int32, (nb, nb), 0)
    c_i = lax.broadcasted_iota(jnp.int32, (nb, nb), 1)
    n_col = lax.broadcasted_iota(jnp.int32, (nb, 1), 0).astype(F32)
    o_r = lax.broadcasted_iota(jnp.int32, o_ref.shape, 0)
    o_c = lax.broadcasted_iota(jnp.int32, o_ref.shape, 1)
    out = jnp.zeros(o_ref.shape, F32)
    for hh in range(nh):
        g_col = jnp.sum(prod[:, hh * B_HEAD_DIM:(hh + 1) * B_HEAD_DIM], axis=1, keepdims=True)
        g_row = jnp.sum(jnp.where(r_i == c_i, jnp.broadcast_to(g_col, (nb, nb)), 0.0), axis=0, keepdims=True)
        beats = (g_row > g_col) | ((g_row == g_col) & (c_i < r_i))
        rank = jnp.sum(jnp.where(beats, 1.0, 0.0), axis=1, keepdims=True)
        for r in range(MOBA_TOPK):
            idx = jnp.sum(jnp.where(rank == r, n_col, 0.0), axis=0, keepdims=True)
            out = jnp.where((o_r == hh) & (o_c == r), idx, out)
    o_ref[...] = out.astype(jnp.int32)


def _topk_blocks(q3, kmean):
    bs, nb, d = kmean.shape
    nh = d // B_HEAD_DIM
    return pl.pallas_call(
        _topk_kernel,
        grid=(bs,),
        in_specs=[pl.BlockSpec((None, 1, d), lambda b: (b, 0, 0)),
                  pl.BlockSpec((None, nb, d), lambda b: (b, 0, 0))],
        out_specs=pl.BlockSpec((None, nh, LANES), lambda b: (b, 0, 0)),
        out_shape=jax.ShapeDtypeStruct((bs, nh, LANES), jnp.int32),
        compiler_params=_cparams("parallel"),
        name="topk_blocks",
    )(q3, kmean)


def _decode_attn_kernel(pt_ref, ix_ref, *refs, n_pages, scale):
    k_pages = refs[:n_pages]
    v_pages = refs[n_pages:2 * n_pages]
    q_ref, kn_ref, vn_ref, z_ref, o_ref = refs[2 * n_pages:]
    q = q_ref[...]
    scores = [jnp.sum(kp[...] * q, axis=1, keepdims=True) * scale for kp in k_pages]
    s_own = jnp.sum(kn_ref[...] * q, axis=1, keepdims=True) * scale
    m = s_own
    for sc in scores:
        m = jnp.maximum(m, jnp.max(sc, axis=0, keepdims=True))
    e_own = jnp.exp(s_own - m)
    l = e_own
    acc = e_own * vn_ref[...]
    for sc, vp in zip(scores, v_pages):
        e = jnp.exp(sc - m)
        l = l + jnp.sum(e, axis=0, keepdims=True)
        acc = acc + jnp.sum(e * vp[...], axis=0, keepdims=True)
    o_ref[...] = acc / l * _silu(z_ref[...])


def _decode_attn(cache_k3, cache_v3, page_table, idx, q3, kn3, vn3, h3, col_z):
    _, page, d = cache_k3.shape
    bs = page_table.shape[0]
    nh = d // B_HEAD_DIM
    per_blk = MOBA_BLOCK // page
    n_pages = MOBA_TOPK * per_blk

    def page_spec(p):
        r, half = divmod(p, per_blk)
        return pl.BlockSpec((None, page, B_HEAD_DIM),
                            lambda b, hh, pt, ix: (pt[b, ix[b, hh * MOBA_TOPK + r] * per_blk + half], 0, hh))

    vec = lambda off: pl.BlockSpec((None, 1, B_HEAD_DIM), lambda b, hh, pt, ix: (b, 0, off + hh))
    return pl.pallas_call(
        functools.partial(_decode_attn_kernel, n_pages=n_pages, scale=B_HEAD_DIM ** -0.5),
        grid_spec=pltpu.PrefetchScalarGridSpec(
            num_scalar_prefetch=2,
            grid=(bs, nh),
            in_specs=([page_spec(p) for p in range(n_pages)] * 2
                      + [vec(0), vec(0), vec(0), vec(col_z * nh)]),
            out_specs=vec(0)),
        out_shape=jax.ShapeDtypeStruct((bs, 1, d), F32),
        compiler_params=_cparams("parallel", "arbitrary"),
        name="decode_attn",
    )(page_table, idx, *([cache_k3] * n_pages), *([cache_v3] * n_pages), q3, kn3, vn3, h3)


def _rope_tables(pos):
    half = B_HEAD_DIM // 2
    freqs = ROPE_THETA ** (-jnp.arange(half, dtype=F32) / half)
    ang = pos.astype(F32)[:, None] * freqs[None, :]
    cos = jnp.cos(ang)
    sin = jnp.sin(ang)
    return jnp.concatenate([cos, cos], -1), jnp.concatenate([-sin, sin], -1)


def _tile(n, pref):
    return pref if n % pref == 0 else n


def _prep_weights(w_in_all, layer, b_in, w_proj_a, w_proj_b, w_out, d):
    c0 = 5 * d
    c1 = c0 + 2 * A_HEADS
    w_in = w_in_all[layer]
    w_b = w_in[:, c1:][None]
    w_g = w_in[:, c0:c1]
    wc = jnp.pad(w_g, ((0, 0), (0, LANES - 2 * A_HEADS))).astype(BF16)
    wr = w_g.T.astype(BF16)
    bc = jnp.pad(b_in[c0:c1], (0, LANES - 2 * A_HEADS))[None, :]
    br = b_in[c0:c1][:, None]
    return ((w_in_all, layer, c0, b_in[None, :c0]), (w_b, 0, w_b.shape[2], b_in[None, c1:]),
            wc, wr, bc, br, w_proj_a.astype(BF16), w_proj_b.astype(BF16), w_out.astype(BF16))


_QA, _KA, _VA, _OA, _ZA = range(5)
_QB, _KB, _VB, _ZB, _GA, _GB = range(6)


def _project(xb, part, tm):
    w3, layer, n, b = part
    return _inproj(xb, w3, layer, n, b, tm, _tile(n, 1024))


def _layer_prompt(x, wts, ng, ln_g, ln_b, alpha):
    bsz, t, d = x.shape
    m = bsz * t
    part_a, part_b, wc, wr, bc, br, wa, wb, wo = wts
    x2 = x.reshape(m, d)
    xb, gc, gr = _gates(x2, wc, wr, bc, br, _tile(m, 512))
    h_a = _project(xb, part_a, _tile(m, 1024))
    h_b = _project(xb, part_b, _tile(m, 1024))
    ya, c1, n1, m1 = _mlstm_prompt(h_a, gc, gr, ng, bsz, t, d)
    cos, sin = _rope_tables(jnp.arange(t, dtype=jnp.int32))
    cos = jnp.tile(cos, (bsz, 1))
    sin = jnp.tile(sin, (bsz, 1))
    q_t, k_new, k_b, v_new, v_t, kmean = _rope(h_b, _QB, _KB, _VB, d, cos, sin, B_HEAD_DIM ** -0.5)
    nb = t // MOBA_BLOCK
    yb = _moba_prompt(q_t, k_b, v_t, kmean.reshape(bsz, nb, d), h_b, _ZB, bsz, t, d)
    mg = _merge(ya, yb, h_b, _GA, _GB, wa, wb, _tile(m, 512), _tile(d, 512))
    y = _out(mg, wo, x2, ln_g, ln_b, _tile(m, 256), alpha)
    nh = d // B_HEAD_DIM
    return (y.reshape(bsz, t, d), k_new.reshape(bsz, t, nh, B_HEAD_DIM), v_new.reshape(bsz, t, nh, B_HEAD_DIM),
            c1, n1.reshape(bsz, A_HEADS, d // A_HEADS), m1[:, :, 0, 0])


def _layer_sample(x, wts, ng, ln_g, ln_b, alpha, cache_k3, cache_v3, page_ids, c_all, c_row0, n0, m0):
    bs, t, d = x.shape
    part_a, part_b, wc, wr, bc, br, wa, wb, wo = wts
    page = cache_k3.shape[1]
    nh = d // B_HEAD_DIM
    past = page_ids.shape[1] * page
    da = d // A_HEADS
    x2 = x.reshape(bs, d)
    xb, gc, _ = _gates(x2, wc, wr, bc, br, bs)
    h_a = _project(xb, part_a, bs)
    h_b = _project(xb, part_b, bs)
    ha3 = h_a.reshape(bs, 1, h_a.shape[1])
    hb3 = h_b.reshape(bs, 1, h_b.shape[1])
    vcol = h_a[:, _VA * d:(_VA + 1) * d].reshape(bs, A_HEADS, da, 1)
    ya, c1, n1, m1 = _mlstm_step(ha3, vcol, gc.reshape(bs, 1, LANES), ng, c_all, c_row0,
                                 n0.reshape(bs, A_HEADS, 1, da),
                                 jnp.broadcast_to(m0[:, :, None, None], (bs, A_HEADS, 1, LANES)), d)
    cos, sin = _rope_tables(jnp.full((bs,), past, dtype=jnp.int32))
    q_r, k_new, v_new = _rope_rows(h_b, _QB, _KB, _VB, d, cos, sin)
    kmean = _kmean_paged(cache_k3, page_ids)
    q3 = q_r.reshape(bs, 1, d)
    idx = _topk_blocks(q3, kmean)[:, :, :MOBA_TOPK].reshape(bs, nh * MOBA_TOPK)
    yb = _decode_attn(cache_k3, cache_v3, page_ids, idx, q3, k_new.reshape(bs, 1, d),
                      v_new.reshape(bs, 1, d), hb3, _ZB)
    hd = B_HEAD_DIM
    mg = _merge(ya.reshape(bs, d), yb.reshape(bs, d), h_b, _GA, _GB, wa, wb, bs, _tile(d, 512))
    y = _out(mg, wo, x2, ln_g, ln_b, bs, alpha)
    return (y.reshape(bs, 1, d), k_new.reshape(bs, 1, nh, hd), v_new.reshape(bs, 1, nh, hd),
            c1, n1.reshape(bs, A_HEADS, da), m1[:, :, 0, 0])


def kernel(x_prompt, x_sample, cache_k, cache_v, page_table, state_mlstm_C, state_mlstm_n, state_mlstm_m,
           w_in, b_in, mh_norm_g, w_proj_a, w_proj_b, w_out, ln_g, ln_b):
    depth = w_in.shape[0]
    d = x_prompt.shape[-1]
    n_pages = page_table.shape[1]
    page = cache_k.shape[2]
    assert x_sample.shape[1] == 1 and d % (A_HEADS * LANES) == 0
    assert x_prompt.shape[1] % MOBA_BLOCK == 0 and MOBA_BLOCK % page == 0
    assert (n_pages * page) % MOBA_BLOCK == 0 and n_pages * page // MOBA_BLOCK >= MOBA_TOPK
    alpha = (2 * depth) ** 0.25
    n_phys = cache_k.shape[1]
    bs = x_sample.shape[0]
    cache_k3 = cache_k.reshape(depth * n_phys, page, d)
    cache_v3 = cache_v.reshape(depth * n_phys, page, d)
    c_all = state_mlstm_C.reshape((depth * bs,) + state_mlstm_C.shape[2:])
    yp, ys = x_prompt, x_sample
    new_p = [[] for _ in range(5)]
    new_s = [[] for _ in range(5)]
    for l in range(depth):
        wts = _prep_weights(w_in, l, b_in[l], w_proj_a[l], w_proj_b[l], w_out[l], d)
        ng, lg, lb = mh_norm_g[l][None, :], ln_g[l][None, :], ln_b[l][None, :]
        yp, *st_p = _layer_prompt(yp, wts, ng, lg, lb, alpha)
        ys, *st_s = _layer_sample(ys, wts, ng, lg, lb, alpha, cache_k3, cache_v3, page_table + l * n_phys,
                                  c_all, l * bs, state_mlstm_n[l], state_mlstm_m[l])
        for i in range(5):
            new_p[i].append(st_p[i])
            new_s[i].append(st_s[i])
    stack = lambda ts: ts[0][None] if len(ts) == 1 else jnp.stack(ts)
    k_p, v_p, c_p, n_p, m_p = [stack(t) for t in new_p]
    k_s, v_s, c_s, n_s, m_s = [stack(t) for t in new_s]
    return (yp, ys, k_p, v_p, c_p, n_p, m_p, k_s, v_s, c_s, n_s, m_s)
```

```python
import functools
import math

import jax
import jax.numpy as jnp
from jax import lax
from jax.experimental import pallas as pl
from jax.experimental.pallas import tpu as pltpu

F32 = jnp.float32
BF16 = jnp.bfloat16

A_HEADS = 4
A_CHUNK = 128
B_HEAD_DIM = 128
MOBA_BLOCK = 256
MOBA_TOPK = 3
ROPE_THETA = 10000.0
LN_EPS = 1e-5
LOG2E = math.log2(math.e)
LANES = 128
GATE_ROWS = 8
PAGES_PER_STEP = 8
MOBA_HEADS_PER_STEP = 4
VMEM_LIMIT = 60 * 1024 * 1024

_NT = (((1,), (1,)), ((), ()))


def _cparams(*sem):
    return pltpu.CompilerParams(dimension_semantics=sem, vmem_limit_bytes=VMEM_LIMIT)


def _log_sigmoid(x):
    return jnp.minimum(x, 0.0) - jnp.log1p(jnp.exp(-jnp.abs(x)))


def _silu(x):
    return x * jax.nn.sigmoid(x)


def _layernorm(x):
    mu = jnp.mean(x, axis=-1, keepdims=True)
    xc = x - mu
    return xc * lax.rsqrt(jnp.mean(xc * xc, axis=-1, keepdims=True) + LN_EPS)


def _inproj_kernel(x_ref, w_ref, b_ref, o_ref):
    o_ref[...] = jnp.dot(x_ref[...], w_ref[...], preferred_element_type=F32) + b_ref[...]


def _inproj(xb, w, b, tm, tn):
    m, k = xb.shape
    n = w.shape[1]
    return pl.pallas_call(
        _inproj_kernel,
        grid=(n // tn, m // tm),
        in_specs=[pl.BlockSpec((tm, k), lambda j, i: (i, 0)),
                  pl.BlockSpec((k, tn), lambda j, i: (0, j)),
                  pl.BlockSpec((1, tn), lambda j, i: (0, j))],
        out_specs=pl.BlockSpec((tm, tn), lambda j, i: (i, j)),
        out_shape=jax.ShapeDtypeStruct((m, n), F32),
        compiler_params=_cparams("parallel", "parallel"),
        name="inproj",
    )(xb, w, b)


def _gates_kernel(x_ref, wc_ref, wr_ref, bc_ref, br_ref, xb_ref, gc_ref, gr_ref):
    xb = x_ref[...].astype(BF16)
    xb_ref[...] = xb
    gc_ref[...] = jnp.dot(xb, wc_ref[...], preferred_element_type=F32) + bc_ref[...]
    gr_ref[...] = lax.dot_general(wr_ref[...], xb, _NT, preferred_element_type=F32) + br_ref[...]


def _gates(x, wc, wr, bc, br, tm):
    m, k = x.shape
    return pl.pallas_call(
        _gates_kernel,
        grid=(m // tm,),
        in_specs=[pl.BlockSpec((tm, k), lambda i: (i, 0)),
                  pl.BlockSpec((k, LANES), lambda i: (0, 0)),
                  pl.BlockSpec((GATE_ROWS, k), lambda i: (0, 0)),
                  pl.BlockSpec((1, LANES), lambda i: (0, 0)),
                  pl.BlockSpec((GATE_ROWS, 1), lambda i: (0, 0))],
        out_specs=[pl.BlockSpec((tm, k), lambda i: (i, 0)),
                   pl.BlockSpec((tm, LANES), lambda i: (i, 0)),
                   pl.BlockSpec((GATE_ROWS, tm), lambda i: (0, i))],
        out_shape=[jax.ShapeDtypeStruct((m, k), BF16),
                   jax.ShapeDtypeStruct((m, LANES), F32),
                   jax.ShapeDtypeStruct((GATE_ROWS, m), F32)],
        compiler_params=_cparams("parallel"),
        name="gates",
    )(x, wc, wr, bc, br)


def _rotate(x, cos, sin):
    return x * cos + pltpu.roll(x, B_HEAD_DIM // 2, 1) * sin


def _rope_kernel(q_ref, k_ref, v_ref, cos_ref, sin_ref, qt_ref, ko_ref, kb_ref, vo_ref, vt_ref, km_ref, *, scale):
    cos = cos_ref[...]
    sin = sin_ref[...]
    tm, d = q_ref.shape
    for h in range(d // B_HEAD_DIM):
        sl = slice(h * B_HEAD_DIM, (h + 1) * B_HEAD_DIM)
        qr = _rotate(q_ref[:, sl], cos, sin)
        kr = _rotate(k_ref[:, sl], cos, sin)
        v = v_ref[:, sl]
        qt_ref[sl, :] = (qr * scale).T.astype(BF16)
        ko_ref[:, sl] = kr
        kb_ref[:, sl] = kr.astype(BF16)
        km_ref[:, sl] = jnp.sum(kr, axis=0, keepdims=True) * (1.0 / tm)
        vo_ref[:, sl] = v
        vt_ref[sl, :] = v.T.astype(BF16)


def _rope(h, col_q, col_k, col_v, d, cos, sin, scale):
    m = h.shape[0]
    tm = MOBA_BLOCK
    nt = m // tm
    row = lambda c: pl.BlockSpec((tm, d), lambda i: (i, c))
    full = pl.BlockSpec((tm, d), lambda i: (i, 0))
    tab = pl.BlockSpec((tm, B_HEAD_DIM), lambda i: (i, 0))
    tr = pl.BlockSpec((None, d, tm), lambda i: (i, 0, 0))
    return pl.pallas_call(
        functools.partial(_rope_kernel, scale=scale),
        grid=(nt,),
        in_specs=[row(col_q), row(col_k), row(col_v), tab, tab],
        out_specs=[tr, full, full, full, tr, pl.BlockSpec((None, 1, d), lambda i: (i, 0, 0))],
        out_shape=[jax.ShapeDtypeStruct((nt, d, tm), BF16),
                   jax.ShapeDtypeStruct((m, d), F32),
                   jax.ShapeDtypeStruct((m, d), BF16),
                   jax.ShapeDtypeStruct((m, d), F32),
                   jax.ShapeDtypeStruct((nt, d, tm), BF16),
                   jax.ShapeDtypeStruct((nt, 1, d), F32)],
        compiler_params=_cparams("parallel"),
        name="rope",
    )(h, h, h, cos, sin)


def _rope_rows_kernel(q_ref, k_ref, v_ref, cos_ref, sin_ref, qo_ref, ko_ref, vo_ref):
    cos = cos_ref[...]
    sin = sin_ref[...]
    for h in range(q_ref.shape[1] // B_HEAD_DIM):
        sl = slice(h * B_HEAD_DIM, (h + 1) * B_HEAD_DIM)
        qo_ref[:, sl] = _rotate(q_ref[:, sl], cos, sin)
        ko_ref[:, sl] = _rotate(k_ref[:, sl], cos, sin)
    vo_ref[...] = v_ref[...]


def _rope_rows(h, col_q, col_k, col_v, d, cos, sin):
    m = h.shape[0]
    row = lambda c: pl.BlockSpec((m, d), lambda i: (0, c))
    full = pl.BlockSpec((m, d), lambda i: (0, 0))
    tab = pl.BlockSpec((m, B_HEAD_DIM), lambda i: (0, 0))
    return pl.pallas_call(
        _rope_rows_kernel,
        grid=(1,),
        in_specs=[row(col_q), row(col_k), row(col_v), tab, tab],
        out_specs=[full, full, full],
        out_shape=[jax.ShapeDtypeStruct((m, d), F32)] * 3,
        compiler_params=_cparams("arbitrary"),
        name="rope_rows",
    )(h, h, h, cos, sin)


def _mlstm_kernel(q_ref, k_ref, v_ref, oa_ref, za_ref, gc_ref, gr_ref, ng_ref,
                  ya_ref, c_out, n_out, m_out, c_sc, n_sc, m_sc, *, scale):
    h = pl.program_id(1)
    c = pl.program_id(2)
    nc = pl.num_programs(2)
    L, D = q_ref.shape

    @pl.when(c == 0)
    def _():
        c_sc[...] = jnp.zeros(c_sc.shape, F32)
        n_sc[...] = jnp.zeros(n_sc.shape, F32)
        m_sc[...] = jnp.zeros(m_sc.shape, F32)

    gc = gc_ref[...]
    gr = gr_ref[...]
    lane = lax.broadcasted_iota(jnp.int32, gc.shape, 1)
    sub = lax.broadcasted_iota(jnp.int32, gr.shape, 0)
    ig_col = jnp.sum(jnp.where(lane == h, gc, 0.0), axis=1, keepdims=True)
    fa_col = jnp.sum(jnp.where(lane == h + A_HEADS, gc, 0.0), axis=1, keepdims=True)
    ig_row = jnp.sum(jnp.where(sub == h, gr, 0.0), axis=0, keepdims=True)
    fa_row = jnp.sum(jnp.where(sub == h + A_HEADS, gr, 0.0), axis=0, keepdims=True)
    lf_col = _log_sigmoid(fa_col)
    lf_row = _log_sigmoid(fa_row)

    t_i = lax.broadcasted_iota(jnp.int32, (L, L), 0)
    s_i = lax.broadcasted_iota(jnp.int32, (L, L), 1)
    causal = s_i <= t_i
    b_col = jnp.sum(jnp.where(causal, jnp.broadcast_to(lf_row, (L, L)), 0.0), axis=1, keepdims=True)
    b_row = jnp.sum(jnp.where(t_i <= s_i, jnp.broadcast_to(lf_col, (L, L)), 0.0), axis=0, keepdims=True)
    a_row = ig_row - b_row
    a_col = ig_col - b_col
    cm_col = jnp.max(jnp.where(causal, jnp.broadcast_to(a_row, (L, L)), -jnp.inf), axis=1, keepdims=True)
    m_prev = m_sc[...]
    g_col = jnp.maximum(m_prev, cm_col)
    dmat = jnp.exp(jnp.where(causal, a_row - g_col, -jnp.inf))
    inter = jnp.exp(m_prev - g_col)
    mt_col = b_col + g_col
    g_last = jnp.maximum(m_prev, jnp.max(a_row, axis=1, keepdims=True))
    m_last = jnp.sum(lf_row, axis=1, keepdims=True) + g_last
    w_col = jnp.exp(a_col - g_last)
    decay = jnp.exp(m_prev - g_last)

    q = q_ref[...]
    ks = k_ref[...] * scale
    v = v_ref[...]
    qb = q.astype(BF16)
    s = lax.dot_general(qb, ks.astype(BF16), _NT, preferred_element_type=F32) * dmat
    cmat = c_sc[...]
    num = (inter * lax.dot_general(qb, cmat.astype(BF16), _NT, preferred_element_type=F32)
           + jnp.dot(s.astype(BF16), v.astype(BF16), preferred_element_type=F32))
    nvec = n_sc[...]
    den = inter * jnp.sum(q * nvec, axis=1, keepdims=True) + jnp.sum(s, axis=1, keepdims=True)
    hh = num / jnp.maximum(jnp.abs(den), jnp.exp(-mt_col))

    kw = ks * w_col
    c_sc[...] = decay * cmat + jnp.dot(v.T.astype(BF16), kw.astype(BF16), preferred_element_type=F32)
    n_sc[...] = decay * nvec + jnp.sum(kw, axis=0, keepdims=True)
    m_sc[...] = m_last

    ya = _layernorm(hh) * ng_ref[...] * jax.nn.sigmoid(oa_ref[...]) * _silu(za_ref[...])
    ya_ref[...] = ya.astype(ya_ref.dtype)

    @pl.when(c == nc - 1)
    def _():
        c_out[...] = c_sc[...]
        n_out[...] = n_sc[...]
        m_out[...] = jnp.broadcast_to(m_sc[...], m_out.shape)


def _mlstm_prompt(h, gc, gr, ng, bsz, t, d):
    da = d // A_HEADS
    L = t if t <= A_CHUNK else math.gcd(t, A_CHUNK)
    nc = t // L
    H = A_HEADS
    blk = lambda g: pl.BlockSpec((L, da), lambda b, hh, c: (b * nc + c, g * H + hh))
    return pl.pallas_call(
        functools.partial(_mlstm_kernel, scale=da ** -0.5),
        grid=(bsz, H, nc),
        in_specs=[blk(0), blk(1), blk(2), blk(3), blk(4),
                  pl.BlockSpec((L, LANES), lambda b, hh, c: (b * nc + c, 0)),
                  pl.BlockSpec((GATE_ROWS, L), lambda b, hh, c: (0, b * nc + c)),
                  pl.BlockSpec((1, da), lambda b, hh, c: (0, hh))],
        out_specs=[pl.BlockSpec((L, da), lambda b, hh, c: (b * nc + c, hh)),
                   pl.BlockSpec((None, None, da, da), lambda b, hh, c: (b, hh, 0, 0)),
                   pl.BlockSpec((None, None, 1, da), lambda b, hh, c: (b, hh, 0, 0)),
                   pl.BlockSpec((None, None, 1, LANES), lambda b, hh, c: (b, hh, 0, 0))],
        out_shape=[jax.ShapeDtypeStruct((bsz * t, d), BF16),
                   jax.ShapeDtypeStruct((bsz, H, da, da), F32),
                   jax.ShapeDtypeStruct((bsz, H, 1, da), F32),
                   jax.ShapeDtypeStruct((bsz, H, 1, LANES), F32)],
        scratch_shapes=[pltpu.VMEM((da, da), F32), pltpu.VMEM((1, da), F32), pltpu.VMEM((1, 1), F32)],
        compiler_params=_cparams("parallel", "parallel", "arbitrary"),
        name="mlstm_prompt",
    )(h, h, h, h, h, gc, gr, ng)


def _moba_kernel(qt_ref, k_ref, vt_ref, km_ref, z_ref, o_ref, mem_sc, acc_sc):
    i = pl.program_id(2)
    blk = qt_ref.shape[1]
    nb = km_ref.shape[0]
    hd = B_HEAD_DIM
    heads = range(qt_ref.shape[0] // hd)
    hsl = [slice(h * hd, (h + 1) * hd) for h in heads]
    qts = [qt_ref[sl, :] for sl in hsl]

    row = lax.broadcasted_iota(jnp.int32, (nb, blk), 0)
    past = row < i
    for h in heads:
        gate = jnp.dot(km_ref[:, hsl[h]].astype(BF16), qts[h], preferred_element_type=F32)
        g = jnp.where(past, gate, -jnp.inf)
        rank = jnp.zeros(gate.shape, F32)
        for n2 in range(nb):
            g2 = g[n2:n2 + 1, :]
            beats = (g2 > g) | ((g2 == g) & (n2 < row))
            rank = rank + jnp.where(beats, 1.0, 0.0)
        mem_sc[h] = jnp.where(past & (rank < MOBA_TOPK), 1.0, 0.0)

    key_i = lax.broadcasted_iota(jnp.int32, (blk, blk), 0)
    qry_i = lax.broadcasted_iota(jnp.int32, (blk, blk), 1)
    own = pl.ds(pl.multiple_of(i * blk, blk), blk)
    m0, l0 = [], []
    for h in heads:
        s = jnp.dot(k_ref[own, hsl[h]], qts[h], preferred_element_type=F32)
        s = jnp.where(key_i <= qry_i, s, -jnp.inf)
        m = jnp.max(s, axis=0, keepdims=True)
        p = jnp.exp2(s - m)
        m0.append(m)
        l0.append(jnp.sum(p, axis=0, keepdims=True))
        acc_sc[h] = jnp.dot(vt_ref[i, hsl[h], :], p.astype(BF16), preferred_element_type=F32)

    def body(jj, carry):
        ms, ls = carry
        j0 = 2 * jj
        rows0 = pl.ds(pl.multiple_of(j0 * blk, blk), blk)
        rows1 = pl.ds(pl.multiple_of(j0 * blk + blk, blk), blk)
        new_m, new_l = [], []
        for h in heads:
            s0 = jnp.dot(k_ref[rows0, hsl[h]], qts[h], preferred_element_type=F32)
            s1 = jnp.dot(k_ref[rows1, hsl[h]], qts[h], preferred_element_type=F32)
            s0 = jnp.where(mem_sc[h, pl.ds(j0, 1), :] > 0.0, s0, -jnp.inf)
            s1 = jnp.where(mem_sc[h, pl.ds(j0 + 1, 1), :] > 0.0, s1, -jnp.inf)
            m_new = jnp.maximum(ms[h], jnp.maximum(jnp.max(s0, axis=0, keepdims=True),
                                                   jnp.max(s1, axis=0, keepdims=True)))
            alpha = jnp.exp2(ms[h] - m_new)
            p0 = jnp.exp2(s0 - m_new)
            p1 = jnp.exp2(s1 - m_new)
            new_l.append(alpha * ls[h] + jnp.sum(p0, axis=0, keepdims=True) + jnp.sum(p1, axis=0, keepdims=True))
            acc_sc[h] = (alpha * acc_sc[h]
                         + jnp.dot(vt_ref[j0, hsl[h], :], p0.astype(BF16), preferred_element_type=F32)
                         + jnp.dot(vt_ref[j0 + 1, hsl[h], :], p1.astype(BF16), preferred_element_type=F32))
            new_m.append(m_new)
        return tuple(new_m), tuple(new_l)

    _, l_fin = lax.fori_loop(0, (i + 1) // 2, body, (tuple(m0), tuple(l0)))
    for h in heads:
        o_ref[:, hsl[h]] = ((acc_sc[h] / l_fin[h]).T * _silu(z_ref[:, hsl[h]])).astype(o_ref.dtype)


def _moba_prompt(qt, kb, vt, kmean, h, col_z, bsz, t, d):
    nh = d // B_HEAD_DIM
    nb = t // MOBA_BLOCK
    blk = MOBA_BLOCK
    hps = MOBA_HEADS_PER_STEP if nh % MOBA_HEADS_PER_STEP == 0 else 1
    w = hps * B_HEAD_DIM
    ng = nh // hps
    return pl.pallas_call(
        _moba_kernel,
        grid=(bsz, ng, nb),
        in_specs=[pl.BlockSpec((None, w, blk), lambda b, g, i: (b * nb + i, g, 0)),
                  pl.BlockSpec((t, w), lambda b, g, i: (b, g)),
                  pl.BlockSpec((nb, w, blk), lambda b, g, i: (b, g, 0)),
                  pl.BlockSpec((None, nb, w), lambda b, g, i: (b, 0, g)),
                  pl.BlockSpec((blk, w), lambda b, g, i: (b * nb + i, col_z * ng + g))],
        out_specs=pl.BlockSpec((blk, w), lambda b, g, i: (b * nb + i, g)),
        out_shape=jax.ShapeDtypeStruct((bsz * t, d), BF16),
        scratch_shapes=[pltpu.VMEM((hps, nb, blk), F32), pltpu.VMEM((hps, B_HEAD_DIM, blk), F32)],
        compiler_params=_cparams("parallel", "parallel", "arbitrary"),
        name="moba_prompt",
    )(qt, kb, vt, kmean, h)


def _merge_kernel(ya_ref, yb_ref, ga_ref, gb_ref, wa_ref, wb_ref, o_ref):
    pa = jnp.dot(ya_ref[...].astype(BF16), wa_ref[...], preferred_element_type=F32)
    pb = jnp.dot(yb_ref[...].astype(BF16), wb_ref[...], preferred_element_type=F32)
    o_ref[...] = (jax.nn.sigmoid(ga_ref[...]) * pa + jax.nn.sigmoid(gb_ref[...]) * pb).astype(o_ref.dtype)


def _merge(ya, yb, h, col_ga, col_gb, wa, wb, tm, tn):
    m, d = ya.shape
    nj = d // tn
    return pl.pallas_call(
        _merge_kernel,
        grid=(m // tm, nj),
        in_specs=[pl.BlockSpec((tm, d), lambda i, j: (i, 0)),
                  pl.BlockSpec((tm, d), lambda i, j: (i, 0)),
                  pl.BlockSpec((tm, tn), lambda i, j: (i, col_ga * nj + j)),
                  pl.BlockSpec((tm, tn), lambda i, j: (i, col_gb * nj + j)),
                  pl.BlockSpec((d, tn), lambda i, j: (0, j)),
                  pl.BlockSpec((d, tn), lambda i, j: (0, j))],
        out_specs=pl.BlockSpec((tm, tn), lambda i, j: (i, j)),
        out_shape=jax.ShapeDtypeStruct((m, d), BF16),
        compiler_params=_cparams("parallel", "arbitrary"),
        name="merge",
    )(ya, yb, h, h, wa, wb)


def _out_kernel(mg_ref, wo_ref, x_ref, g_ref, b_ref, y_ref, *, alpha):
    out = jnp.dot(mg_ref[...], wo_ref[...], preferred_element_type=F32)
    y_ref[...] = _layernorm(alpha * x_ref[...] + out) * g_ref[...] + b_ref[...]


def _out(mg, wo, x, g, b, tm, alpha):
    m, d = x.shape
    row = pl.BlockSpec((tm, d), lambda i: (i, 0))
    vec = pl.BlockSpec((1, d), lambda i: (0, 0))
    return pl.pallas_call(
        functools.partial(_out_kernel, alpha=alpha),
        grid=(m // tm,),
        in_specs=[row, pl.BlockSpec((d, d), lambda i: (0, 0)), row, vec, vec],
        out_specs=row,
        out_shape=jax.ShapeDtypeStruct((m, d), F32),
        compiler_params=_cparams("parallel"),
        name="outproj",
    )(mg, wo, x, g, b)


def _mlstm_step_kernel(q_ref, k_ref, v_ref, oa_ref, za_ref, gc_ref, ng_ref, c_ref, n_ref, m_ref,
                       ya_ref, c_out, n_out, m_out, *, scale):
    h = pl.program_id(1)
    gc = gc_ref[...]
    lane = lax.broadcasted_iota(jnp.int32, gc.shape, 1)
    ig = jnp.sum(jnp.where(lane == h, gc, 0.0), axis=1, keepdims=True)
    fa = jnp.sum(jnp.where(lane == h + A_HEADS, gc, 0.0), axis=1, keepdims=True)
    lf = _log_sigmoid(fa)
    m_prev = m_ref[...][:, :1]
    a = ig - lf
    g = jnp.maximum(m_prev, a)
    m_t = lf + g
    inter = jnp.exp(m_prev - g)
    dm = jnp.exp(a - g)

    q = q_ref[...]
    ks = k_ref[...] * scale
    v = v_ref[...]
    cmat = c_ref[...]
    nvec = n_ref[...]
    q8 = jnp.broadcast_to(q, (8, q.shape[1])).astype(BF16)
    cq = lax.dot_general(q8, cmat.astype(BF16), _NT, preferred_element_type=F32)[0:1]
    s = jnp.sum(q * ks, axis=1, keepdims=True) * dm
    num = inter * cq + s * v
    den = inter * jnp.sum(nvec * q, axis=1, keepdims=True) + s
    hh = num / jnp.maximum(jnp.abs(den), jnp.exp(-m_t))

    da = v.shape[1]
    eye = lax.broadcasted_iota(jnp.int32, (da, da), 0) == lax.broadcasted_iota(jnp.int32, (da, da), 1)
    v_col = jnp.sum(jnp.where(eye, jnp.broadcast_to(v, (da, da)), 0.0), axis=1, keepdims=True)
    c_out[...] = inter * cmat + (dm * v_col) * ks
    n_out[...] = inter * nvec + dm * ks
    m_out[...] = jnp.broadcast_to(m_t, m_out.shape)
    ya = _layernorm(hh) * ng_ref[...] * jax.nn.sigmoid(oa_ref[...]) * _silu(za_ref[...])
    ya_ref[...] = ya.astype(ya_ref.dtype)


def _mlstm_step(h3, gc3, ng, c_all, c_row0, n0, m0, d):
    bs = h3.shape[0]
    H = A_HEADS
    da = d // H
    row = lambda g: pl.BlockSpec((None, 1, da), lambda b, hh: (b, 0, g * H + hh))
    st = lambda r, c: pl.BlockSpec((None, None, r, c), lambda b, hh: (b, hh, 0, 0))
    return pl.pallas_call(
        functools.partial(_mlstm_step_kernel, scale=da ** -0.5),
        grid=(bs, H),
        in_specs=[row(0), row(1), row(2), row(3), row(4),
                  pl.BlockSpec((None, 1, LANES), lambda b, hh: (b, 0, 0)),
                  pl.BlockSpec((1, da), lambda b, hh: (0, hh)),
                  pl.BlockSpec((None, None, da, da), lambda b, hh: (c_row0 + b, hh, 0, 0)),
                  st(1, da), st(1, LANES)],
        out_specs=[pl.BlockSpec((None, 1, da), lambda b, hh: (b, 0, hh)),
                   st(da, da), st(1, da), st(1, LANES)],
        out_shape=[jax.ShapeDtypeStruct((bs, 1, d), F32),
                   jax.ShapeDtypeStruct((bs, H, da, da), F32),
                   jax.ShapeDtypeStruct((bs, H, 1, da), F32),
                   jax.ShapeDtypeStruct((bs, H, 1, LANES), F32)],
        compiler_params=_cparams("parallel", "parallel"),
        name="mlstm_step",
    )(h3, h3, h3, h3, h3, gc3, ng, c_all, n0, m0)


def _kmean_kernel(pt_ref, *refs):
    o_ref = refs[-1]
    pages = refs[:-1]
    per_blk = len(pages) // o_ref.shape[0]
    rows = per_blk * pages[0].shape[0]
    for r in range(o_ref.shape[0]):
        tot = jnp.sum(pages[r * per_blk][...], axis=0)
        for p in range(1, per_blk):
            tot = tot + jnp.sum(pages[r * per_blk + p][...], axis=0)
        o_ref[r] = tot * (1.0 / rows)


def _kmean_paged(cache4, page_ids):
    _, page, nh, hd = cache4.shape
    bs, n_pages = page_ids.shape
    per_blk = MOBA_BLOCK // page
    pps = min(PAGES_PER_STEP, n_pages)
    steps = n_pages // pps
    bps = pps // per_blk
    spec = lambda p: pl.BlockSpec((None, page, nh, hd), lambda b, j, pt: (pt[b, j * pps + p], 0, 0, 0))
    return pl.pallas_call(
        _kmean_kernel,
        grid_spec=pltpu.PrefetchScalarGridSpec(
            num_scalar_prefetch=1,
            grid=(bs, steps),
            in_specs=[spec(p) for p in range(pps)],
            out_specs=pl.BlockSpec((None, bps, nh, hd), lambda b, j, pt: (b, j, 0, 0))),
        out_shape=jax.ShapeDtypeStruct((bs, steps * bps, nh, hd), F32),
        compiler_params=_cparams("parallel", "arbitrary"),
        name="kmean_paged",
    )(page_ids, *([cache4] * pps))


def _topk_kernel(q_ref, km_ref, o_ref):
    q = q_ref[...]
    nb = km_ref.shape[0]
    lane = lax.broadcasted_iota(jnp.int32, o_ref.shape, 1)
    gate = jnp.full(o_ref.shape, -jnp.inf, F32)
    for n in range(nb):
        gate = jnp.where(lane == n, jnp.sum(km_ref[n] * q, axis=1, keepdims=True), gate)
    rank = jnp.zeros(o_ref.shape, F32)
    for n2 in range(nb):
        g2 = gate[:, n2:n2 + 1]
        beats = (g2 > gate) | ((g2 == gate) & (n2 < lane))
        rank = rank + jnp.where(beats, 1.0, 0.0)
    lane_f = lane.astype(F32)
    out = jnp.zeros(o_ref.shape, F32)
    for r in range(MOBA_TOPK):
        idx = jnp.sum(jnp.where((rank == r) & (lane < nb), lane_f, 0.0), axis=1, keepdims=True)
        out = jnp.where(lane == r, idx, out)
    o_ref[...] = out.astype(jnp.int32)


def _topk_blocks(q4, kmean):
    bs, nb, nh, hd = kmean.shape
    assert nb <= LANES
    return pl.pallas_call(
        _topk_kernel,
        grid=(bs,),
        in_specs=[pl.BlockSpec((None, nh, hd), lambda b: (b, 0, 0)),
                  pl.BlockSpec((None, nb, nh, hd), lambda b: (b, 0, 0, 0))],
        out_specs=pl.BlockSpec((None, nh, LANES), lambda b: (b, 0, 0)),
        out_shape=jax.ShapeDtypeStruct((bs, nh, LANES), jnp.int32),
        compiler_params=_cparams("parallel"),
        name="topk_blocks",
    )(q4, kmean)


def _decode_attn_kernel(pt_ref, ix_ref, q_ref, kn_ref, vn_ref, z_ref, ck_hbm, cv_hbm, o_ref,
                        kbuf, vbuf, sems, *, per_blk, scale):
    b = pl.program_id(0)
    nbat = pl.num_programs(0)
    nh = q_ref.shape[0]
    n_pages = MOBA_TOPK * per_blk

    def page_copies(bb, slot, h, p):
        r, half = divmod(p, per_blk)
        pid = pt_ref[bb, ix_ref[bb, h * MOBA_TOPK + r] * per_blk + half]
        return (pltpu.make_async_copy(ck_hbm.at[pid, :, h, :], kbuf.at[slot, h, p], sems.at[0, slot]),
                pltpu.make_async_copy(cv_hbm.at[pid, :, h, :], vbuf.at[slot, h, p], sems.at[1, slot]))

    def for_each_copy(bb, slot, fn):
        def per_head(h, carry):
            for p in range(n_pages):
                for cp in page_copies(bb, slot, h, p):
                    fn(cp)
            return carry
        lax.fori_loop(0, nh, per_head, 0)

    @pl.when(b == 0)
    def _():
        for_each_copy(0, 0, lambda cp: cp.start())

    @pl.when(b + 1 < nbat)
    def _():
        for_each_copy(b + 1, (b + 1) % 2, lambda cp: cp.start())

    slot = b % 2
    for_each_copy(b, slot, lambda cp: cp.wait())

    def per_head(h, carry):
        row = pl.ds(h, 1)
        q = q_ref[row, :]
        scores = [jnp.sum(kbuf[slot, h, p] * q, axis=1, keepdims=True) * scale for p in range(n_pages)]
        s_own = jnp.sum(kn_ref[row, :] * q, axis=1, keepdims=True) * scale
        m = s_own
        for sc in scores:
            m = jnp.maximum(m, jnp.max(sc, axis=0, keepdims=True))
        e_own = jnp.exp(s_own - m)
        l = e_own
        acc = e_own * vn_ref[row, :]
        for p, sc in enumerate(scores):
            e = jnp.exp(sc - m)
            l = l + jnp.sum(e, axis=0, keepdims=True)
            acc = acc + jnp.sum(e * vbuf[slot, h, p], axis=0, keepdims=True)
        o_ref[row, :] = acc / l * _silu(z_ref[row, :])
        return carry

    lax.fori_loop(0, nh, per_head, 0)


def _decode_attn(cache_k4, cache_v4, page_ids, idx, q4, kn4, vn4, z4):
    _, page, nh, hd = cache_k4.shape
    bs = page_ids.shape[0]
    per_blk = MOBA_BLOCK // page
    n_pages = MOBA_TOPK * per_blk
    vec = pl.BlockSpec((None, nh, hd), lambda b, pt, ix: (b, 0, 0))
    hbm = pl.BlockSpec(memory_space=pl.ANY)
    return pl.pallas_call(
        functools.partial(_decode_attn_kernel, per_blk=per_blk, scale=hd ** -0.5),
        grid_spec=pltpu.PrefetchScalarGridSpec(
            num_scalar_prefetch=2,
            grid=(bs,),
            in_specs=[vec, vec, vec, vec, hbm, hbm],
            out_specs=vec,
            scratch_shapes=[pltpu.VMEM((2, nh, n_pages, page, hd), F32),
                            pltpu.VMEM((2, nh, n_pages, page, hd), F32),
                            pltpu.SemaphoreType.DMA((2, 2))]),
        out_shape=jax.ShapeDtypeStruct((bs, nh, hd), F32),
        compiler_params=_cparams("arbitrary"),
        name="decode_attn",
    )(page_ids, idx, q4, kn4, vn4, z4, cache_k4, cache_v4)


def _rope_tables(pos):
    half = B_HEAD_DIM // 2
    freqs = ROPE_THETA ** (-jnp.arange(half, dtype=F32) / half)
    ang = pos.astype(F32)[:, None] * freqs[None, :]
    cos = jnp.cos(ang)
    sin = jnp.sin(ang)
    return jnp.concatenate([cos, cos], -1), jnp.concatenate([-sin, sin], -1)


def _tile(n, pref):
    return pref if n % pref == 0 else n


def _prep_weights(w_in_all, layer, b_in, w_proj_a, w_proj_b, w_out, d):
    c0 = 5 * d
    c1 = c0 + 2 * A_HEADS
    w_in = w_in_all[layer]
    w_a = w_in[:, :c0].astype(BF16)
    w_b = w_in[:, c1:].astype(BF16)
    w_g = w_in[:, c0:c1]
    wc = jnp.pad(w_g, ((0, 0), (0, LANES - 2 * A_HEADS))).astype(BF16)
    wr = w_g.T.astype(BF16)
    bc = jnp.pad(b_in[c0:c1], (0, LANES - 2 * A_HEADS))[None, :]
    br = b_in[c0:c1][:, None]
    return ((w_a, b_in[None, :c0]), (w_b, b_in[None, c1:]),
            wc, wr, bc, br, w_proj_a.astype(BF16), w_proj_b.astype(BF16), w_out.astype(BF16))


_QA, _KA, _VA, _OA, _ZA = range(5)
_QB, _KB, _VB, _ZB, _GA, _GB = range(6)


def _project(xb, part, tm):
    w, b = part
    return _inproj(xb, w, b, tm, _tile(w.shape[1], 1024))


def _layer_prompt(x, wts, ng, ln_g, ln_b, alpha):
    bsz, t, d = x.shape
    m = bsz * t
    part_a, part_b, wc, wr, bc, br, wa, wb, wo = wts
    x2 = x.reshape(m, d)
    xb, gc, gr = _gates(x2, wc, wr, bc, br, _tile(m, 512))
    h_a = _project(xb, part_a, _tile(m, 1024))
    h_b = _project(xb, part_b, _tile(m, 1024))
    ya, c1, n1, m1 = _mlstm_prompt(h_a, gc, gr, ng, bsz, t, d)
    cos, sin = _rope_tables(jnp.arange(t, dtype=jnp.int32))
    cos = jnp.tile(cos, (bsz, 1))
    sin = jnp.tile(sin, (bsz, 1))
    q_t, k_new, k_b, v_new, v_t, kmean = _rope(h_b, _QB, _KB, _VB, d, cos, sin, B_HEAD_DIM ** -0.5 * LOG2E)
    nb = t // MOBA_BLOCK
    yb = _moba_prompt(q_t, k_b, v_t, kmean.reshape(bsz, nb, d), h_b, _ZB, bsz, t, d)
    mg = _merge(ya, yb, h_b, _GA, _GB, wa, wb, _tile(m, 512), _tile(d, 512))
    y = _out(mg, wo, x2, ln_g, ln_b, _tile(m, 256), alpha)
    nh = d // B_HEAD_DIM
    return (y.reshape(bsz, t, d), k_new.reshape(bsz, t, nh, B_HEAD_DIM), v_new.reshape(bsz, t, nh, B_HEAD_DIM),
            c1, n1.reshape(bsz, A_HEADS, d // A_HEADS), m1[:, :, 0, 0])


def _layer_sample(x, wts, ng, ln_g, ln_b, alpha, cache_k4, cache_v4, page_ids, c_all, c_row0, n0, m0):
    bs, t, d = x.shape
    part_a, part_b, wc, wr, bc, br, wa, wb, wo = wts
    _, page, nh, hd = cache_k4.shape
    past = page_ids.shape[1] * page
    da = d // A_HEADS
    x2 = x.reshape(bs, d)
    xb, gc, _ = _gates(x2, wc, wr, bc, br, bs)
    h_a = _project(xb, part_a, bs)
    h_b = _project(xb, part_b, bs)
    ya, c1, n1, m1 = _mlstm_step(h_a.reshape(bs, 1, h_a.shape[1]), gc.reshape(bs, 1, LANES), ng, c_all, c_row0,
                                 n0.reshape(bs, A_HEADS, 1, da),
                                 jnp.broadcast_to(m0[:, :, None, None], (bs, A_HEADS, 1, LANES)), d)
    cos, sin = _rope_tables(jnp.full((bs,), past, dtype=jnp.int32))
    q_r, k_new, v_new = _rope_rows(h_b, _QB, _KB, _VB, d, cos, sin)
    heads = lambda a: a.reshape(bs, nh, hd)
    q4, kn4, vn4 = heads(q_r), heads(k_new), heads(v_new)
    kmean = _kmean_paged(cache_k4, page_ids)
    idx = _topk_blocks(q4, kmean)[:, :, :MOBA_TOPK].reshape(bs, nh * MOBA_TOPK)
    yb = _decode_attn(cache_k4, cache_v4, page_ids, idx, q4, kn4, vn4, heads(h_b[:, _ZB * d:(_ZB + 1) * d]))
    mg = _merge(ya.reshape(bs, d), yb.reshape(bs, d), h_b, _GA, _GB, wa, wb, bs, _tile(d, 512))
    y = _out(mg, wo, x2, ln_g, ln_b, bs, alpha)
    return (y.reshape(bs, 1, d), kn4[:, None], vn4[:, None],
            c1, n1.reshape(bs, A_HEADS, da), m1[:, :, 0, 0])


def kernel(x_prompt, x_sample, cache_k, cache_v, page_table, state_mlstm_C, state_mlstm_n, state_mlstm_m,
           w_in, b_in, mh_norm_g, w_proj_a, w_proj_b, w_out, ln_g, ln_b):
    depth = w_in.shape[0]
    d = x_prompt.shape[-1]
    n_pages = page_table.shape[1]
    page = cache_k.shape[2]
    assert x_sample.shape[1] == 1 and d % (A_HEADS * LANES) == 0
    assert x_prompt.shape[1] % MOBA_BLOCK == 0 and MOBA_BLOCK % page == 0
    assert (n_pages * page) % MOBA_BLOCK == 0 and n_pages * page // MOBA_BLOCK >= MOBA_TOPK
    alpha = (2 * depth) ** 0.25
    n_phys = cache_k.shape[1]
    bs = x_sample.shape[0]
    cache_k3 = cache_k.reshape((depth * n_phys,) + cache_k.shape[2:])
    cache_v3 = cache_v.reshape((depth * n_phys,) + cache_v.shape[2:])
    c_all = state_mlstm_C.reshape((depth * bs,) + state_mlstm_C.shape[2:])
    yp, ys = x_prompt, x_sample
    new_p = [[] for _ in range(5)]
    new_s = [[] for _ in range(5)]
    for l in range(depth):
        wts = _prep_weights(w_in, l, b_in[l], w_proj_a[l], w_proj_b[l], w_out[l], d)
        ng, lg, lb = mh_norm_g[l][None, :], ln_g[l][None, :], ln_b[l][None, :]
        yp, *st_p = _layer_prompt(yp, wts, ng, lg, lb, alpha)
        ys, *st_s = _layer_sample(ys, wts, ng, lg, lb, alpha, cache_k3, cache_v3, page_table + l * n_phys,
                                  c_all, l * bs, state_mlstm_n[l], state_mlstm_m[l])
        for i in range(5):
            new_p[i].append(st_p[i])
            new_s[i].append(st_s[i])
    stack = lambda ts: ts[0][None] if len(ts) == 1 else jnp.stack(ts)
    k_p, v_p, c_p, n_p, m_p = [stack(t) for t in new_p]
    k_s, v_s, c_s, n_s, m_s = [stack(t) for t in new_s]
    return (yp, ys, k_p, v_p, c_p, n_p, m_p, k_s, v_s, c_s, n_s, m_s)
```

```python
import functools
import math

import jax
import jax.numpy as jnp
from jax import lax
from jax.experimental import pallas as pl
from jax.experimental.pallas import tpu as pltpu

F32 = jnp.float32
BF16 = jnp.bfloat16

A_HEADS = 4
A_CHUNK = 128
B_HEAD_DIM = 128
MOBA_BLOCK = 256
MOBA_TOPK = 3
ROPE_THETA = 10000.0
LN_EPS = 1e-5
LOG2E = math.log2(math.e)
LANES = 128
GATE_ROWS = 8
PAGES_PER_STEP = 8
MOBA_HEADS_PER_STEP = 8
VMEM_LIMIT = 60 * 1024 * 1024

_NT = (((1,), (1,)), ((), ()))


def _cparams(*sem):
    return pltpu.CompilerParams(dimension_semantics=sem, vmem_limit_bytes=VMEM_LIMIT)


def _log_sigmoid(x):
    return jnp.minimum(x, 0.0) - jnp.log1p(jnp.exp(-jnp.abs(x)))


def _silu(x):
    return x * jax.nn.sigmoid(x)


def _layernorm(x):
    mu = jnp.mean(x, axis=-1, keepdims=True)
    xc = x - mu
    return xc * lax.rsqrt(jnp.mean(xc * xc, axis=-1, keepdims=True) + LN_EPS)


def _inproj_kernel(x_ref, w_ref, b_ref, o_ref):
    o_ref[...] = jnp.dot(x_ref[...], w_ref[...], preferred_element_type=F32) + b_ref[...]


def _inproj(xb, w, b, tm, tn):
    m, k = xb.shape
    n = b.shape[1]
    return pl.pallas_call(
        _inproj_kernel,
        grid=(n // tn, m // tm),
        in_specs=[pl.BlockSpec((tm, k), lambda j, i: (i, 0)),
                  pl.BlockSpec((k, tn), lambda j, i: (0, j)),
                  pl.BlockSpec((1, tn), lambda j, i: (0, j))],
        out_specs=pl.BlockSpec((tm, tn), lambda j, i: (i, j)),
        out_shape=jax.ShapeDtypeStruct((m, n), F32),
        compiler_params=_cparams("parallel", "parallel"),
        name="inproj",
    )(xb, w, b)


def _gates_kernel(x_ref, wc_ref, wr_ref, bc_ref, br_ref, xb_ref, gc_ref, gr_ref):
    xb = x_ref[...].astype(BF16)
    xb_ref[...] = xb
    gc_ref[...] = jnp.dot(xb, wc_ref[...], preferred_element_type=F32) + bc_ref[...]
    gr_ref[...] = lax.dot_general(wr_ref[...], xb, _NT, preferred_element_type=F32) + br_ref[...]


def _gates(x, wc, wr, bc, br, tm):
    m, k = x.shape
    return pl.pallas_call(
        _gates_kernel,
        grid=(m // tm,),
        in_specs=[pl.BlockSpec((tm, k), lambda i: (i, 0)),
                  pl.BlockSpec((k, LANES), lambda i: (0, 0)),
                  pl.BlockSpec((GATE_ROWS, k), lambda i: (0, 0)),
                  pl.BlockSpec((1, LANES), lambda i: (0, 0)),
                  pl.BlockSpec((GATE_ROWS, 1), lambda i: (0, 0))],
        out_specs=[pl.BlockSpec((tm, k), lambda i: (i, 0)),
                   pl.BlockSpec((tm, LANES), lambda i: (i, 0)),
                   pl.BlockSpec((GATE_ROWS, tm), lambda i: (0, i))],
        out_shape=[jax.ShapeDtypeStruct((m, k), BF16),
                   jax.ShapeDtypeStruct((m, LANES), F32),
                   jax.ShapeDtypeStruct((GATE_ROWS, m), F32)],
        compiler_params=_cparams("parallel"),
        name="gates",
    )(x, wc, wr, bc, br)


def _rotate(x, cos, sin):
    return x * cos + pltpu.roll(x, B_HEAD_DIM // 2, 1) * sin


def _rope_kernel(q_ref, k_ref, v_ref, cos_ref, sin_ref, qt_ref, ko_ref, kb_ref, vo_ref, vt_ref, km_ref, *, scale):
    cos = cos_ref[...]
    sin = sin_ref[...]
    tm, d = q_ref.shape
    for h in range(d // B_HEAD_DIM):
        sl = slice(h * B_HEAD_DIM, (h + 1) * B_HEAD_DIM)
        qr = _rotate(q_ref[:, sl], cos, sin)
        kr = _rotate(k_ref[:, sl], cos, sin)
        v = v_ref[:, sl]
        qt_ref[sl, :] = (qr * scale).T.astype(BF16)
        ko_ref[:, sl] = kr
        kb_ref[:, sl] = kr.astype(BF16)
        km_ref[:, sl] = jnp.sum(kr, axis=0, keepdims=True) * (1.0 / tm)
        vo_ref[:, sl] = v
        vt_ref[sl, :] = v.T.astype(BF16)


def _rope(h, col_q, col_k, col_v, d, cos, sin, scale):
    m = h.shape[0]
    tm = MOBA_BLOCK
    nt = m // tm
    row = lambda c: pl.BlockSpec((tm, d), lambda i: (i, c))
    full = pl.BlockSpec((tm, d), lambda i: (i, 0))
    tab = pl.BlockSpec((tm, B_HEAD_DIM), lambda i: (i, 0))
    tr = pl.BlockSpec((None, d, tm), lambda i: (i, 0, 0))
    return pl.pallas_call(
        functools.partial(_rope_kernel, scale=scale),
        grid=(nt,),
        in_specs=[row(col_q), row(col_k), row(col_v), tab, tab],
        out_specs=[tr, full, full, full, tr, pl.BlockSpec((None, 1, d), lambda i: (i, 0, 0))],
        out_shape=[jax.ShapeDtypeStruct((nt, d, tm), BF16),
                   jax.ShapeDtypeStruct((m, d), F32),
                   jax.ShapeDtypeStruct((m, d), BF16),
                   jax.ShapeDtypeStruct((m, d), F32),
                   jax.ShapeDtypeStruct((nt, d, tm), BF16),
                   jax.ShapeDtypeStruct((nt, 1, d), F32)],
        compiler_params=_cparams("parallel"),
        name="rope",
    )(h, h, h, cos, sin)


def _rope_rows_kernel(q_ref, k_ref, v_ref, cos_ref, sin_ref, qo_ref, ko_ref, vo_ref):
    cos = cos_ref[...]
    sin = sin_ref[...]
    for h in range(q_ref.shape[1] // B_HEAD_DIM):
        sl = slice(h * B_HEAD_DIM, (h + 1) * B_HEAD_DIM)
        qo_ref[:, sl] = _rotate(q_ref[:, sl], cos, sin)
        ko_ref[:, sl] = _rotate(k_ref[:, sl], cos, sin)
    vo_ref[...] = v_ref[...]


def _rope_rows(h, col_q, col_k, col_v, d, cos, sin):
    m = h.shape[0]
    row = lambda c: pl.BlockSpec((m, d), lambda i: (0, c))
    full = pl.BlockSpec((m, d), lambda i: (0, 0))
    tab = pl.BlockSpec((m, B_HEAD_DIM), lambda i: (0, 0))
    return pl.pallas_call(
        _rope_rows_kernel,
        grid=(1,),
        in_specs=[row(col_q), row(col_k), row(col_v), tab, tab],
        out_specs=[full, full, full],
        out_shape=[jax.ShapeDtypeStruct((m, d), F32)] * 3,
        compiler_params=_cparams("arbitrary"),
        name="rope_rows",
    )(h, h, h, cos, sin)


def _mlstm_kernel(q_ref, k_ref, v_ref, oa_ref, za_ref, gc_ref, gr_ref, ng_ref,
                  ya_ref, c_out, n_out, m_out, c_sc, n_sc, m_sc, *, scale):
    c = pl.program_id(1)
    nc = pl.num_programs(1)
    L = q_ref.shape[0]
    H = A_HEADS
    D = q_ref.shape[1] // H
    heads = range(H)
    hsl = [slice(h * D, (h + 1) * D) for h in heads]

    @pl.when(c == 0)
    def _():
        c_sc[...] = jnp.zeros(c_sc.shape, F32)
        n_sc[...] = jnp.zeros(n_sc.shape, F32)
        m_sc[...] = jnp.zeros(m_sc.shape, F32)

    gc = gc_ref[...]
    gr = gr_ref[...]
    t_i = lax.broadcasted_iota(jnp.int32, (L, L), 0)
    s_i = lax.broadcasted_iota(jnp.int32, (L, L), 1)
    causal = s_i <= t_i
    def head_gates(h):
        ig_col, fa_col = gc[:, h:h + 1], gc[:, h + H:h + H + 1]
        ig_row, fa_row = gr[h:h + 1, :], gr[h + H:h + H + 1, :]
        lf_col = _log_sigmoid(fa_col)
        lf_row = _log_sigmoid(fa_row)
        b_col = jnp.sum(jnp.where(causal, jnp.broadcast_to(lf_row, (L, L)), 0.0), axis=1, keepdims=True)
        b_row = jnp.sum(jnp.where(t_i <= s_i, jnp.broadcast_to(lf_col, (L, L)), 0.0), axis=0, keepdims=True)
        a_row = ig_row - b_row
        a_col = ig_col - b_col
        cm_col = jnp.max(jnp.where(causal, jnp.broadcast_to(a_row, (L, L)), -jnp.inf), axis=1, keepdims=True)
        m_prev = m_sc[h]
        g_col = jnp.maximum(m_prev, cm_col)
        g_last = jnp.maximum(m_prev, jnp.max(a_row, axis=1, keepdims=True))
        return dict(
            dmat=jnp.exp(jnp.where(causal, a_row - g_col, -jnp.inf)),
            inter=jnp.exp(m_prev - g_col),
            inv_floor=jnp.exp(-(b_col + g_col)),
            m_last=jnp.sum(lf_row, axis=1, keepdims=True) + g_last,
            w_col=jnp.exp(a_col - g_last),
            decay=jnp.exp(m_prev - g_last))

    def first_matmuls(h):
        q = q_ref[:, hsl[h]]
        ks = k_ref[:, hsl[h]] * scale
        qb = q.astype(BF16)
        s_raw = lax.dot_general(qb, ks.astype(BF16), _NT, preferred_element_type=F32)
        cq = lax.dot_general(qb, c_sc[h].astype(BF16), _NT, preferred_element_type=F32)
        return head_gates(h), q, ks, s_raw, cq

    def rest(h, g, q, ks, s_raw, cq):
        v = v_ref[:, hsl[h]]
        s = s_raw * g["dmat"]
        num = g["inter"] * cq + jnp.dot(s.astype(BF16), v.astype(BF16), preferred_element_type=F32)
        nvec = n_sc[h]
        den = g["inter"] * jnp.sum(q * nvec, axis=1, keepdims=True) + jnp.sum(s, axis=1, keepdims=True)
        hh = num / jnp.maximum(jnp.abs(den), g["inv_floor"])
        ya = (_layernorm(hh) * ng_ref[:, hsl[h]] * jax.nn.sigmoid(oa_ref[:, hsl[h]])
              * _silu(za_ref[:, hsl[h]]))
        ya_ref[:, hsl[h]] = ya.astype(ya_ref.dtype)
        kw = ks * g["w_col"]
        c_sc[h] = g["decay"] * c_sc[h] + jnp.dot(v.T.astype(BF16), kw.astype(BF16), preferred_element_type=F32)
        n_sc[h] = g["decay"] * nvec + jnp.sum(kw, axis=0, keepdims=True)
        m_sc[h] = g["m_last"]

    ahead = first_matmuls(0)
    for h in heads:
        cur = ahead
        if h + 1 < H:
            ahead = first_matmuls(h + 1)
        rest(h, *cur)

    @pl.when(c == nc - 1)
    def _():
        c_out[...] = c_sc[...]
        n_out[...] = n_sc[...]
        for h in heads:
            m_out[h] = jnp.broadcast_to(m_sc[h], m_out.shape[1:])


def _mlstm_prompt(h, gc, gr, ng, bsz, t, d):
    da = d // A_HEADS
    L = t if t <= A_CHUNK else math.gcd(t, A_CHUNK)
    nc = t // L
    H = A_HEADS
    blk = lambda g: pl.BlockSpec((L, d), lambda b, c: (b * nc + c, g))
    return pl.pallas_call(
        functools.partial(_mlstm_kernel, scale=da ** -0.5),
        grid=(bsz, nc),
        in_specs=[blk(0), blk(1), blk(2), blk(3), blk(4),
                  pl.BlockSpec((L, LANES), lambda b, c: (b * nc + c, 0)),
                  pl.BlockSpec((GATE_ROWS, L), lambda b, c: (0, b * nc + c)),
                  pl.BlockSpec((1, d), lambda b, c: (0, 0))],
        out_specs=[pl.BlockSpec((L, d), lambda b, c: (b * nc + c, 0)),
                   pl.BlockSpec((None, H, da, da), lambda b, c: (b, 0, 0, 0)),
                   pl.BlockSpec((None, H, 1, da), lambda b, c: (b, 0, 0, 0)),
                   pl.BlockSpec((None, H, 1, LANES), lambda b, c: (b, 0, 0, 0))],
        out_shape=[jax.ShapeDtypeStruct((bsz * t, d), BF16),
                   jax.ShapeDtypeStruct((bsz, H, da, da), F32),
                   jax.ShapeDtypeStruct((bsz, H, 1, da), F32),
                   jax.ShapeDtypeStruct((bsz, H, 1, LANES), F32)],
        scratch_shapes=[pltpu.VMEM((H, da, da), F32), pltpu.VMEM((H, 1, da), F32), pltpu.VMEM((H, 1, 1), F32)],
        compiler_params=_cparams("parallel", "arbitrary"),
        name="mlstm_prompt",
    )(h, h, h, h, h, gc, gr, ng)


def _moba_kernel(qt_ref, k_ref, vt_ref, km_ref, z_ref, o_ref, mem_sc, acc_sc):
    i = pl.program_id(2)
    blk = qt_ref.shape[1]
    nb = km_ref.shape[0]
    hd = B_HEAD_DIM
    heads = range(qt_ref.shape[0] // hd)
    hsl = [slice(h * hd, (h + 1) * hd) for h in heads]
    qts = [qt_ref[sl, :] for sl in hsl]

    row = lax.broadcasted_iota(jnp.int32, (nb, blk), 0)
    past = row < i
    key_i = lax.broadcasted_iota(jnp.int32, (blk, blk), 0)
    qry_i = lax.broadcasted_iota(jnp.int32, (blk, blk), 1)
    own = pl.ds(pl.multiple_of(i * blk, blk), blk)
    gates = [jnp.dot(km_ref[:, hsl[h]].astype(BF16), qts[h], preferred_element_type=F32) for h in heads]
    s_own = [jnp.dot(k_ref[own, hsl[h]], qts[h], preferred_element_type=F32) for h in heads]
    for h in heads:
        g = jnp.where(past, gates[h], -jnp.inf)
        rank = jnp.zeros(g.shape, F32)
        for n2 in range(nb):
            g2 = g[n2:n2 + 1, :]
            beats = (g2 > g) | ((g2 == g) & (n2 < row))
            rank = rank + jnp.where(beats, 1.0, 0.0)
        mem_sc[h] = jnp.where(past & (rank < MOBA_TOPK), 1.0, 0.0)
    m0, l0, p_own = [], [], []
    for h in heads:
        s = jnp.where(key_i <= qry_i, s_own[h], -jnp.inf)
        m = jnp.max(s, axis=0, keepdims=True)
        p = jnp.exp2(s - m)
        m0.append(m)
        l0.append(jnp.sum(p, axis=0, keepdims=True))
        p_own.append(p.astype(BF16))
    for h in heads:
        acc_sc[h] = jnp.dot(vt_ref[i, hsl[h], :], p_own[h], preferred_element_type=F32)

    def body(jj, carry):
        ms, ls = carry
        j0 = 2 * jj
        rows0 = pl.ds(pl.multiple_of(j0 * blk, blk), blk)
        rows1 = pl.ds(pl.multiple_of(j0 * blk + blk, blk), blk)
        new_m, new_l = [], []
        ss = []
        for h in heads:
            s0 = jnp.dot(k_ref[rows0, hsl[h]], qts[h], preferred_element_type=F32)
            s1 = jnp.dot(k_ref[rows1, hsl[h]], qts[h], preferred_element_type=F32)
            s0 = jnp.where(mem_sc[h, pl.ds(j0, 1), :] > 0.0, s0, -jnp.inf)
            s1 = jnp.where(mem_sc[h, pl.ds(j0 + 1, 1), :] > 0.0, s1, -jnp.inf)
            ss.append((s0, s1))
        ps = []
        for h in heads:
            s0, s1 = ss[h]
            m_new = jnp.maximum(ms[h], jnp.maximum(jnp.max(s0, axis=0, keepdims=True),
                                                   jnp.max(s1, axis=0, keepdims=True)))
            alpha = jnp.exp2(ms[h] - m_new)
            p0 = jnp.exp2(s0 - m_new)
            p1 = jnp.exp2(s1 - m_new)
            new_l.append(alpha * ls[h] + jnp.sum(p0, axis=0, keepdims=True) + jnp.sum(p1, axis=0, keepdims=True))
            new_m.append(m_new)
            ps.append((alpha, p0.astype(BF16), p1.astype(BF16)))
        for h in heads:
            alpha, p0, p1 = ps[h]
            acc_sc[h] = (alpha * acc_sc[h]
                         + jnp.dot(vt_ref[j0, hsl[h], :], p0, preferred_element_type=F32)
                         + jnp.dot(vt_ref[j0 + 1, hsl[h], :], p1, preferred_element_type=F32))
        return tuple(new_m), tuple(new_l)

    _, l_fin = lax.fori_loop(0, (i + 1) // 2, body, (tuple(m0), tuple(l0)))
    outs = [(acc_sc[h] / l_fin[h]).T for h in heads]
    for h in heads:
        o_ref[:, hsl[h]] = (outs[h] * _silu(z_ref[:, hsl[h]])).astype(o_ref.dtype)


def _moba_prompt(qt, kb, vt, kmean, h, col_z, bsz, t, d):
    nh = d // B_HEAD_DIM
    nb = t // MOBA_BLOCK
    blk = MOBA_BLOCK
    hps = MOBA_HEADS_PER_STEP if nh % MOBA_HEADS_PER_STEP == 0 else 1
    w = hps * B_HEAD_DIM
    ng = nh // hps
    return pl.pallas_call(
        _moba_kernel,
        grid=(bsz, ng, nb),
        in_specs=[pl.BlockSpec((None, w, blk), lambda b, g, i: (b * nb + i, g, 0)),
                  pl.BlockSpec((t, w), lambda b, g, i: (b, g)),
                  pl.BlockSpec((nb, w, blk), lambda b, g, i: (b, g, 0)),
                  pl.BlockSpec((None, nb, w), lambda b, g, i: (b, 0, g)),
                  pl.BlockSpec((blk, w), lambda b, g, i: (b * nb + i, col_z * ng + g))],
        out_specs=pl.BlockSpec((blk, w), lambda b, g, i: (b * nb + i, g)),
        out_shape=jax.ShapeDtypeStruct((bsz * t, d), BF16),
        scratch_shapes=[pltpu.VMEM((hps, nb, blk), F32), pltpu.VMEM((hps, B_HEAD_DIM, blk), F32)],
        compiler_params=_cparams("parallel", "parallel", "arbitrary"),
        name="moba_prompt",
    )(qt, kb, vt, kmean, h)


def _merge_kernel(ya_ref, yb_ref, ga_ref, gb_ref, wa_ref, wb_ref, o_ref):
    pa = jnp.dot(ya_ref[...].astype(BF16), wa_ref[...], preferred_element_type=F32)
    pb = jnp.dot(yb_ref[...].astype(BF16), wb_ref[...], preferred_element_type=F32)
    o_ref[...] = (jax.nn.sigmoid(ga_ref[...]) * pa + jax.nn.sigmoid(gb_ref[...]) * pb).astype(o_ref.dtype)


def _merge(ya, yb, h, col_ga, col_gb, wa, wb, tm, tn):
    m, d = ya.shape
    nj = d // tn
    return pl.pallas_call(
        _merge_kernel,
        grid=(m // tm, nj),
        in_specs=[pl.BlockSpec((tm, d), lambda i, j: (i, 0)),
                  pl.BlockSpec((tm, d), lambda i, j: (i, 0)),
                  pl.BlockSpec((tm, tn), lambda i, j: (i, col_ga * nj + j)),
                  pl.BlockSpec((tm, tn), lambda i, j: (i, col_gb * nj + j)),
                  pl.BlockSpec((d, tn), lambda i, j: (0, j)),
                  pl.BlockSpec((d, tn), lambda i, j: (0, j))],
        out_specs=pl.BlockSpec((tm, tn), lambda i, j: (i, j)),
        out_shape=jax.ShapeDtypeStruct((m, d), BF16),
        compiler_params=_cparams("parallel", "arbitrary"),
        name="merge",
    )(ya, yb, h, h, wa, wb)


def _out_kernel(mg_ref, wo_ref, x_ref, g_ref, b_ref, y_ref, *, alpha):
    out = jnp.dot(mg_ref[...], wo_ref[...], preferred_element_type=F32)
    y_ref[...] = _layernorm(alpha * x_ref[...] + out) * g_ref[...] + b_ref[...]


def _out(mg, wo, x, g, b, tm, alpha):
    m, d = x.shape
    row = pl.BlockSpec((tm, d), lambda i: (i, 0))
    vec = pl.BlockSpec((1, d), lambda i: (0, 0))
    return pl.pallas_call(
        functools.partial(_out_kernel, alpha=alpha),
        grid=(m // tm,),
        in_specs=[row, pl.BlockSpec((d, d), lambda i: (0, 0)), row, vec, vec],
        out_specs=row,
        out_shape=jax.ShapeDtypeStruct((m, d), F32),
        compiler_params=_cparams("parallel"),
        name="outproj",
    )(mg, wo, x, g, b)


def _mlstm_step_kernel(q_ref, k_ref, v_ref, oa_ref, za_ref, gc_ref, ng_ref, c_ref, n_ref, m_ref,
                       ya_ref, c_out, n_out, m_out, *, scale):
    H = A_HEADS
    da = q_ref.shape[1] // H
    gc = gc_ref[...]
    eye = lax.broadcasted_iota(jnp.int32, (da, da), 0) == lax.broadcasted_iota(jnp.int32, (da, da), 1)
    for h in range(H):
        sl = slice(h * da, (h + 1) * da)
        ig = gc[:, h:h + 1]
        lf = _log_sigmoid(gc[:, h + H:h + H + 1])
        m_prev = m_ref[h][:, :1]
        a = ig - lf
        g = jnp.maximum(m_prev, a)
        m_t = lf + g
        inter = jnp.exp(m_prev - g)
        dm = jnp.exp(a - g)

        q = q_ref[:, sl]
        ks = k_ref[:, sl] * scale
        v = v_ref[:, sl]
        cmat = c_ref[h]
        nvec = n_ref[h]
        q8 = jnp.broadcast_to(q, (8, da)).astype(BF16)
        cq = lax.dot_general(q8, cmat.astype(BF16), _NT, preferred_element_type=F32)[0:1]
        s = jnp.sum(q * ks, axis=1, keepdims=True) * dm
        num = inter * cq + s * v
        den = inter * jnp.sum(nvec * q, axis=1, keepdims=True) + s
        hh = num / jnp.maximum(jnp.abs(den), jnp.exp(-m_t))

        v_col = jnp.sum(jnp.where(eye, jnp.broadcast_to(v, (da, da)), 0.0), axis=1, keepdims=True)
        c_out[h] = inter * cmat + (dm * v_col) * ks
        n_out[h] = inter * nvec + dm * ks
        m_out[h] = jnp.broadcast_to(m_t, m_out.shape[1:])
        ya = _layernorm(hh) * ng_ref[:, sl] * jax.nn.sigmoid(oa_ref[:, sl]) * _silu(za_ref[:, sl])
        ya_ref[:, sl] = ya.astype(ya_ref.dtype)


def _mlstm_step(h3, gc3, ng, c_all, c_row0, n0, m0, d):
    bs = h3.shape[0]
    H = A_HEADS
    da = d // H
    row = lambda g: pl.BlockSpec((None, 1, d), lambda b: (b, 0, g))
    st = lambda r, c: pl.BlockSpec((None, H, r, c), lambda b: (b, 0, 0, 0))
    return pl.pallas_call(
        functools.partial(_mlstm_step_kernel, scale=da ** -0.5),
        grid=(bs,),
        in_specs=[row(0), row(1), row(2), row(3), row(4),
                  pl.BlockSpec((None, 1, LANES), lambda b: (b, 0, 0)),
                  pl.BlockSpec((1, d), lambda b: (0, 0)),
                  pl.BlockSpec((None, H, da, da), lambda b: (c_row0 + b, 0, 0, 0)),
                  st(1, da), st(1, LANES)],
        out_specs=[pl.BlockSpec((None, 1, d), lambda b: (b, 0, 0)),
                   st(da, da), st(1, da), st(1, LANES)],
        out_shape=[jax.ShapeDtypeStruct((bs, 1, d), F32),
                   jax.ShapeDtypeStruct((bs, H, da, da), F32),
                   jax.ShapeDtypeStruct((bs, H, 1, da), F32),
                   jax.ShapeDtypeStruct((bs, H, 1, LANES), F32)],
        compiler_params=_cparams("parallel"),
        name="mlstm_step",
    )(h3, h3, h3, h3, h3, gc3, ng, c_all, n0, m0)


def _kmean_kernel(pt_ref, *refs):
    o_ref = refs[-1]
    pages = refs[:-1]
    per_blk = len(pages) // o_ref.shape[0]
    rows = per_blk * pages[0].shape[0]
    for r in range(o_ref.shape[0]):
        tot = jnp.sum(pages[r * per_blk][...], axis=0)
        for p in range(1, per_blk):
            tot = tot + jnp.sum(pages[r * per_blk + p][...], axis=0)
        o_ref[r] = tot * (1.0 / rows)


def _kmean_paged(cache4, page_ids):
    _, page, nh, hd = cache4.shape
    bs, n_pages = page_ids.shape
    per_blk = MOBA_BLOCK // page
    pps = min(PAGES_PER_STEP, n_pages)
    steps = n_pages // pps
    bps = pps // per_blk
    spec = lambda p: pl.BlockSpec((None, page, nh, hd), lambda b, j, pt: (pt[b, j * pps + p], 0, 0, 0))
    return pl.pallas_call(
        _kmean_kernel,
        grid_spec=pltpu.PrefetchScalarGridSpec(
            num_scalar_prefetch=1,
            grid=(bs, steps),
            in_specs=[spec(p) for p in range(pps)],
            out_specs=pl.BlockSpec((None, bps, nh, hd), lambda b, j, pt: (b, j, 0, 0))),
        out_shape=jax.ShapeDtypeStruct((bs, steps * bps, nh, hd), F32),
        compiler_params=_cparams("parallel", "arbitrary"),
        name="kmean_paged",
    )(page_ids, *([cache4] * pps))


def _topk_kernel(q_ref, km_ref, o_ref):
    q = q_ref[...]
    nb = km_ref.shape[0]
    lane = lax.broadcasted_iota(jnp.int32, o_ref.shape, 1)
    gate = jnp.full(o_ref.shape, -jnp.inf, F32)
    for n in range(nb):
        gate = jnp.where(lane == n, jnp.sum(km_ref[n] * q, axis=1, keepdims=True), gate)
    rank = jnp.zeros(o_ref.shape, F32)
    for n2 in range(nb):
        g2 = gate[:, n2:n2 + 1]
        beats = (g2 > gate) | ((g2 == gate) & (n2 < lane))
        rank = rank + jnp.where(beats, 1.0, 0.0)
    lane_f = lane.astype(F32)
    out = jnp.zeros(o_ref.shape, F32)
    for r in range(MOBA_TOPK):
        idx = jnp.sum(jnp.where((rank == r) & (lane < nb), lane_f, 0.0), axis=1, keepdims=True)
        out = jnp.where(lane == r, idx, out)
    o_ref[...] = out.astype(jnp.int32)


def _topk_blocks(q4, kmean):
    bs, nb, nh, hd = kmean.shape
    assert nb <= LANES
    return pl.pallas_call(
        _topk_kernel,
        grid=(bs,),
        in_specs=[pl.BlockSpec((None, nh, hd), lambda b: (b, 0, 0)),
                  pl.BlockSpec((None, nb, nh, hd), lambda b: (b, 0, 0, 0))],
        out_specs=pl.BlockSpec((None, nh, LANES), lambda b: (b, 0, 0)),
        out_shape=jax.ShapeDtypeStruct((bs, nh, LANES), jnp.int32),
        compiler_params=_cparams("parallel"),
        name="topk_blocks",
    )(q4, kmean)


def _decode_attn_kernel(pt_ref, ix_ref, q_ref, kn_ref, vn_ref, z_ref, ck_hbm, cv_hbm, o_ref,
                        kbuf, vbuf, sems, *, per_blk, scale):
    b = pl.program_id(0)
    nbat = pl.num_programs(0)
    nh = q_ref.shape[0]
    n_pages = MOBA_TOPK * per_blk

    def page_copies(bb, slot, h, p):
        r, half = divmod(p, per_blk)
        pid = pt_ref[bb, ix_ref[bb, h * MOBA_TOPK + r] * per_blk + half]
        return (pltpu.make_async_copy(ck_hbm.at[pid, :, h, :], kbuf.at[slot, h, p], sems.at[0, slot]),
                pltpu.make_async_copy(cv_hbm.at[pid, :, h, :], vbuf.at[slot, h, p], sems.at[1, slot]))

    def for_each_copy(bb, slot, fn):
        def per_head(h, carry):
            for p in range(n_pages):
                for cp in page_copies(bb, slot, h, p):
                    fn(cp)
            return carry
        lax.fori_loop(0, nh, per_head, 0)

    @pl.when(b == 0)
    def _():
        for_each_copy(0, 0, lambda cp: cp.start())

    @pl.when(b + 1 < nbat)
    def _():
        for_each_copy(b + 1, (b + 1) % 2, lambda cp: cp.start())

    slot = b % 2
    for_each_copy(b, slot, lambda cp: cp.wait())

    def per_head(h, carry):
        row = pl.ds(h, 1)
        q = q_ref[row, :]
        scores = [jnp.sum(kbuf[slot, h, p] * q, axis=1, keepdims=True) * scale for p in range(n_pages)]
        s_own = jnp.sum(kn_ref[row, :] * q, axis=1, keepdims=True) * scale
        m = s_own
        for sc in scores:
            m = jnp.maximum(m, jnp.max(sc, axis=0, keepdims=True))
        e_own = jnp.exp(s_own - m)
        l = e_own
        acc = e_own * vn_ref[row, :]
        for p, sc in enumerate(scores):
            e = jnp.exp(sc - m)
            l = l + jnp.sum(e, axis=0, keepdims=True)
            acc = acc + jnp.sum(e * vbuf[slot, h, p], axis=0, keepdims=True)
        o_ref[row, :] = acc / l * _silu(z_ref[row, :])
        return carry

    lax.fori_loop(0, nh, per_head, 0)


def _decode_attn(cache_k4, cache_v4, page_ids, idx, q4, kn4, vn4, z4):
    _, page, nh, hd = cache_k4.shape
    bs = page_ids.shape[0]
    per_blk = MOBA_BLOCK // page
    n_pages = MOBA_TOPK * per_blk
    vec = pl.BlockSpec((None, nh, hd), lambda b, pt, ix: (b, 0, 0))
    hbm = pl.BlockSpec(memory_space=pl.ANY)
    return pl.pallas_call(
        functools.partial(_decode_attn_kernel, per_blk=per_blk, scale=hd ** -0.5),
        grid_spec=pltpu.PrefetchScalarGridSpec(
            num_scalar_prefetch=2,
            grid=(bs,),
            in_specs=[vec, vec, vec, vec, hbm, hbm],
            out_specs=vec,
            scratch_shapes=[pltpu.VMEM((2, nh, n_pages, page, hd), F32),
                            pltpu.VMEM((2, nh, n_pages, page, hd), F32),
                            pltpu.SemaphoreType.DMA((2, 2))]),
        out_shape=jax.ShapeDtypeStruct((bs, nh, hd), F32),
        compiler_params=_cparams("arbitrary"),
        name="decode_attn",
    )(page_ids, idx, q4, kn4, vn4, z4, cache_k4, cache_v4)


def _rope_tables(pos):
    half = B_HEAD_DIM // 2
    freqs = ROPE_THETA ** (-jnp.arange(half, dtype=F32) / half)
    ang = pos.astype(F32)[:, None] * freqs[None, :]
    cos = jnp.cos(ang)
    sin = jnp.sin(ang)
    return jnp.concatenate([cos, cos], -1), jnp.concatenate([-sin, sin], -1)


def _tile(n, pref):
    return pref if n % pref == 0 else n


def _prep_weights(w_in_all, layer, b_in, w_proj_a, w_proj_b, w_out, d):
    c0 = 5 * d
    c1 = c0 + 2 * A_HEADS
    w_in = w_in_all[layer]
    w_a = w_in.astype(BF16)
    w_b = w_a[:, c1:]
    w_g = w_in[:, c0:c1]
    wc = jnp.pad(w_g, ((0, 0), (0, LANES - 2 * A_HEADS))).astype(BF16)
    wr = w_g.T.astype(BF16)
    bc = jnp.pad(b_in[c0:c1], (0, LANES - 2 * A_HEADS))[None, :]
    br = b_in[c0:c1][:, None]
    return ((w_a, b_in[None, :c0]), (w_b, b_in[None, c1:]),
            wc, wr, bc, br, w_proj_a.astype(BF16), w_proj_b.astype(BF16), w_out.astype(BF16))


_QA, _KA, _VA, _OA, _ZA = range(5)
_QB, _KB, _VB, _ZB, _GA, _GB = range(6)


def _project(xb, part, tm):
    w, b = part
    return _inproj(xb, w, b, tm, _tile(b.shape[1], 1024))


def _layer_prompt(x, wts, ng, ln_g, ln_b, alpha):
    bsz, t, d = x.shape
    m = bsz * t
    part_a, part_b, wc, wr, bc, br, wa, wb, wo = wts
    x2 = x.reshape(m, d)
    xb, gc, gr = _gates(x2, wc, wr, bc, br, _tile(m, 512))
    h_a = _project(xb, part_a, _tile(m, 1024))
    h_b = _project(xb, part_b, _tile(m, 1024))
    ya, c1, n1, m1 = _mlstm_prompt(h_a, gc, gr, ng, bsz, t, d)
    cos, sin = _rope_tables(jnp.arange(t, dtype=jnp.int32))
    cos = jnp.tile(cos, (bsz, 1))
    sin = jnp.tile(sin, (bsz, 1))
    q_t, k_new, k_b, v_new, v_t, kmean = _rope(h_b, _QB, _KB, _VB, d, cos, sin, B_HEAD_DIM ** -0.5 * LOG2E)
    nb = t // MOBA_BLOCK
    yb = _moba_prompt(q_t, k_b, v_t, kmean.reshape(bsz, nb, d), h_b, _ZB, bsz, t, d)
    mg = _merge(ya, yb, h_b, _GA, _GB, wa, wb, _tile(m, 1024), _tile(d, 512))
    y = _out(mg, wo, x2, ln_g, ln_b, _tile(m, 512), alpha)
    nh = d // B_HEAD_DIM
    return (y.reshape(bsz, t, d), k_new.reshape(bsz, t, nh, B_HEAD_DIM), v_new.reshape(bsz, t, nh, B_HEAD_DIM),
            c1, n1.reshape(bsz, A_HEADS, d // A_HEADS), m1[:, :, 0, 0])


def _layer_sample(x, wts, ng, ln_g, ln_b, alpha, cache_k4, cache_v4, page_ids, c_all, c_row0, n0, m0):
    bs, t, d = x.shape
    part_a, part_b, wc, wr, bc, br, wa, wb, wo = wts
    _, page, nh, hd = cache_k4.shape
    past = page_ids.shape[1] * page
    da = d // A_HEADS
    x2 = x.reshape(bs, d)
    xb, gc, _ = _gates(x2, wc, wr, bc, br, bs)
    h_a = _project(xb, part_a, bs)
    h_b = _project(xb, part_b, bs)
    ya, c1, n1, m1 = _mlstm_step(h_a.reshape(bs, 1, h_a.shape[1]), gc.reshape(bs, 1, LANES), ng, c_all, c_row0,
                                 n0.reshape(bs, A_HEADS, 1, da),
                                 jnp.broadcast_to(m0[:, :, None, None], (bs, A_HEADS, 1, LANES)), d)
    cos, sin = _rope_tables(jnp.full((bs,), past, dtype=jnp.int32))
    q_r, k_new, v_new = _rope_rows(h_b, _QB, _KB, _VB, d, cos, sin)
    heads = lambda a: a.reshape(bs, nh, hd)
    q4, kn4, vn4 = heads(q_r), heads(k_new), heads(v_new)
    kmean = _kmean_paged(cache_k4, page_ids)
    idx = _topk_blocks(q4, kmean)[:, :, :MOBA_TOPK].reshape(bs, nh * MOBA_TOPK)
    yb = _decode_attn(cache_k4, cache_v4, page_ids, idx, q4, kn4, vn4, heads(h_b[:, _ZB * d:(_ZB + 1) * d]))
    mg = _merge(ya.reshape(bs, d), yb.reshape(bs, d), h_b, _GA, _GB, wa, wb, bs, _tile(d, 512))
    y = _out(mg, wo, x2, ln_g, ln_b, bs, alpha)
    return (y.reshape(bs, 1, d), kn4[:, None], vn4[:, None],
            c1, n1.reshape(bs, A_HEADS, da), m1[:, :, 0, 0])


def kernel(x_prompt, x_sample, cache_k, cache_v, page_table, state_mlstm_C, state_mlstm_n, state_mlstm_m,
           w_in, b_in, mh_norm_g, w_proj_a, w_proj_b, w_out, ln_g, ln_b):
    depth = w_in.shape[0]
    d = x_prompt.shape[-1]
    n_pages = page_table.shape[1]
    page = cache_k.shape[2]
    assert x_sample.shape[1] == 1 and d % (A_HEADS * LANES) == 0
    assert x_prompt.shape[1] % MOBA_BLOCK == 0 and MOBA_BLOCK % page == 0
    assert (n_pages * page) % MOBA_BLOCK == 0 and n_pages * page // MOBA_BLOCK >= MOBA_TOPK
    alpha = (2 * depth) ** 0.25
    n_phys = cache_k.shape[1]
    bs = x_sample.shape[0]
    cache_k3 = cache_k.reshape((depth * n_phys,) + cache_k.shape[2:])
    cache_v3 = cache_v.reshape((depth * n_phys,) + cache_v.shape[2:])
    c_all = state_mlstm_C.reshape((depth * bs,) + state_mlstm_C.shape[2:])
    yp, ys = x_prompt, x_sample
    new_p = [[] for _ in range(5)]
    new_s = [[] for _ in range(5)]
    for l in range(depth):
        wts = _prep_weights(w_in, l, b_in[l], w_proj_a[l], w_proj_b[l], w_out[l], d)
        ng, lg, lb = mh_norm_g[l][None, :], ln_g[l][None, :], ln_b[l][None, :]
        yp, *st_p = _layer_prompt(yp, wts, ng, lg, lb, alpha)
        ys, *st_s = _layer_sample(ys, wts, ng, lg, lb, alpha, cache_k3, cache_v3, page_table + l * n_phys,
                                  c_all, l * bs, state_mlstm_n[l], state_mlstm_m[l])
        for i in range(5):
            new_p[i].append(st_p[i])
            new_s[i].append(st_s[i])
    stack = lambda ts: ts[0][None] if len(ts) == 1 else jnp.stack(ts)
    k_p, v_p, c_p, n_p, m_p = [stack(t) for t in new_p]
    k_s, v_s, c_s, n_s, m_s = [stack(t) for t in new_s]
    return (yp, ys, k_p, v_p, c_p, n_p, m_p, k_s, v_s, c_s, n_s, m_s)
```

```python
import functools
import math

import jax
import jax.numpy as jnp
from jax import lax
from jax.experimental import pallas as pl
from jax.experimental.pallas import tpu as pltpu

F32 = jnp.float32
BF16 = jnp.bfloat16

A_HEADS = 4
A_CHUNK = 128
B_HEAD_DIM = 128
MOBA_BLOCK = 256
MOBA_TOPK = 3
ROPE_THETA = 10000.0
LN_EPS = 1e-5
LOG2E = math.log2(math.e)
LANES = 128
GATE_ROWS = 8
MOBA_HEADS_PER_STEP = 8
VMEM_LIMIT = 60 * 1024 * 1024

_NT = (((1,), (1,)), ((), ()))


def _cparams(*sem):
    return pltpu.CompilerParams(dimension_semantics=sem, vmem_limit_bytes=VMEM_LIMIT)


def _log_sigmoid(x):
    return jnp.minimum(x, 0.0) - jnp.log1p(jnp.exp(-jnp.abs(x)))


def _silu(x):
    return x * jax.nn.sigmoid(x)


def _layernorm(x):
    mu = jnp.mean(x, axis=-1, keepdims=True)
    xc = x - mu
    return xc * lax.rsqrt(jnp.mean(xc * xc, axis=-1, keepdims=True) + LN_EPS)


def _inproj_kernel(x_ref, w_ref, b_ref, o_ref):
    o_ref[...] = jnp.dot(x_ref[...], w_ref[...], preferred_element_type=F32) + b_ref[...]


def _inproj(xb, w, b, tm, tn):
    m, k = xb.shape
    n = b.shape[1]
    return pl.pallas_call(
        _inproj_kernel,
        grid=(n // tn, m // tm),
        in_specs=[pl.BlockSpec((tm, k), lambda j, i: (i, 0)),
                  pl.BlockSpec((k, tn), lambda j, i: (0, j)),
                  pl.BlockSpec((1, tn), lambda j, i: (0, j))],
        out_specs=pl.BlockSpec((tm, tn), lambda j, i: (i, j)),
        out_shape=jax.ShapeDtypeStruct((m, n), F32),
        compiler_params=_cparams("parallel", "parallel"),
        name="inproj",
    )(xb, w, b)


def _inproj_means_kernel(pt_ref, x_ref, w_ref, b_ref, ck_hbm, o_ref, km_ref, pbuf, sems,
                         *, first_block, n_blocks, per_blk, n_pages):
    ni = pl.num_programs(1)
    step = pl.program_id(0) * ni + pl.program_id(1)
    n_steps = pl.num_programs(0) * ni
    bps = km_ref.shape[0]

    def page_copy(s, slot, r, p):
        blk = jnp.minimum(first_block + s * bps + r, n_blocks - 1)
        g = blk * per_blk + p
        pid = pt_ref[g // n_pages, g % n_pages]
        return pltpu.make_async_copy(ck_hbm.at[pid], pbuf.at[slot, r * per_blk + p], sems.at[slot])

    def for_each_copy(s, slot, fn):
        for r in range(bps):
            for p in range(per_blk):
                fn(page_copy(s, slot, r, p))

    @pl.when(step == 0)
    def _():
        for_each_copy(0, 0, lambda cp: cp.start())

    @pl.when(step + 1 < n_steps)
    def _():
        for_each_copy(step + 1, (step + 1) % 2, lambda cp: cp.start())

    slot = step % 2
    for_each_copy(step, slot, lambda cp: cp.wait())
    o_ref[...] = jnp.dot(x_ref[...], w_ref[...], preferred_element_type=F32) + b_ref[...]
    rows = per_blk * pbuf.shape[2]
    for r in range(bps):
        tot = jnp.sum(pbuf[slot, r * per_blk], axis=0)
        for p in range(1, per_blk):
            tot = tot + jnp.sum(pbuf[slot, r * per_blk + p], axis=0)
        km_ref[r] = tot * (1.0 / rows)


def _inproj_means(xb, w, b, tm, tn, page_ids, cache4, first_block, blocks_per_step):
    m, k = xb.shape
    n = b.shape[1]
    _, page, nh, hd = cache4.shape
    bs, n_pages = page_ids.shape
    per_blk = MOBA_BLOCK // page
    nj, ni = n // tn, m // tm
    bps = blocks_per_step
    return pl.pallas_call(
        functools.partial(_inproj_means_kernel, first_block=first_block, n_blocks=bs * n_pages // per_blk,
                          per_blk=per_blk, n_pages=n_pages),
        grid_spec=pltpu.PrefetchScalarGridSpec(
            num_scalar_prefetch=1,
            grid=(nj, ni),
            in_specs=[pl.BlockSpec((tm, k), lambda j, i, pt: (i, 0)),
                      pl.BlockSpec((k, tn), lambda j, i, pt: (0, j)),
                      pl.BlockSpec((1, tn), lambda j, i, pt: (0, j)),
                      pl.BlockSpec(memory_space=pl.ANY)],
            out_specs=[pl.BlockSpec((tm, tn), lambda j, i, pt: (i, j)),
                       pl.BlockSpec((bps, nh, hd), lambda j, i, pt: (j * ni + i, 0, 0))],
            scratch_shapes=[pltpu.VMEM((2, bps * per_blk, page, nh, hd), F32),
                            pltpu.SemaphoreType.DMA((2,))]),
        out_shape=[jax.ShapeDtypeStruct((m, n), F32),
                   jax.ShapeDtypeStruct((nj * ni * bps, nh, hd), F32)],
        compiler_params=_cparams("arbitrary", "arbitrary"),
        name="inproj_means",
    )(page_ids, xb, w, b, cache4)


def _gates_kernel(x_ref, wc_ref, wr_ref, bc_ref, br_ref, xb_ref, gc_ref, gr_ref):
    xb = x_ref[...].astype(BF16)
    xb_ref[...] = xb
    gc_ref[...] = jnp.dot(xb, wc_ref[...], preferred_element_type=F32) + bc_ref[...]
    gr_ref[...] = lax.dot_general(wr_ref[...], xb, _NT, preferred_element_type=F32) + br_ref[...]


def _gates(x, wc, wr, bc, br, tm):
    m, k = x.shape
    return pl.pallas_call(
        _gates_kernel,
        grid=(m // tm,),
        in_specs=[pl.BlockSpec((tm, k), lambda i: (i, 0)),
                  pl.BlockSpec((k, LANES), lambda i: (0, 0)),
                  pl.BlockSpec((GATE_ROWS, k), lambda i: (0, 0)),
                  pl.BlockSpec((1, LANES), lambda i: (0, 0)),
                  pl.BlockSpec((GATE_ROWS, 1), lambda i: (0, 0))],
        out_specs=[pl.BlockSpec((tm, k), lambda i: (i, 0)),
                   pl.BlockSpec((tm, LANES), lambda i: (i, 0)),
                   pl.BlockSpec((GATE_ROWS, tm), lambda i: (0, i))],
        out_shape=[jax.ShapeDtypeStruct((m, k), BF16),
                   jax.ShapeDtypeStruct((m, LANES), F32),
                   jax.ShapeDtypeStruct((GATE_ROWS, m), F32)],
        compiler_params=_cparams("parallel"),
        name="gates",
    )(x, wc, wr, bc, br)


def _rotate(x, cos, sin):
    return x * cos + pltpu.roll(x, B_HEAD_DIM // 2, 1) * sin


def _rope_kernel(q_ref, k_ref, v_ref, cos_ref, sin_ref, qt_ref, ko_ref, kb_ref, vo_ref, vt_ref, km_ref, *, scale):
    cos = cos_ref[...]
    sin = sin_ref[...]
    tm, d = q_ref.shape
    for h in range(d // B_HEAD_DIM):
        sl = slice(h * B_HEAD_DIM, (h + 1) * B_HEAD_DIM)
        qr = _rotate(q_ref[:, sl], cos, sin)
        kr = _rotate(k_ref[:, sl], cos, sin)
        v = v_ref[:, sl]
        qt_ref[sl, :] = (qr * scale).T.astype(BF16)
        ko_ref[:, sl] = kr
        kb_ref[:, sl] = kr.astype(BF16)
        km_ref[:, sl] = jnp.sum(kr, axis=0, keepdims=True) * (1.0 / tm)
        vo_ref[:, sl] = v
        vt_ref[sl, :] = v.T.astype(BF16)


def _rope(h, col_q, col_k, col_v, d, cos, sin, scale):
    m = h.shape[0]
    tm = MOBA_BLOCK
    nt = m // tm
    row = lambda c: pl.BlockSpec((tm, d), lambda i: (i, c))
    full = pl.BlockSpec((tm, d), lambda i: (i, 0))
    tab = pl.BlockSpec((tm, B_HEAD_DIM), lambda i: (i, 0))
    tr = pl.BlockSpec((None, d, tm), lambda i: (i, 0, 0))
    return pl.pallas_call(
        functools.partial(_rope_kernel, scale=scale),
        grid=(nt,),
        in_specs=[row(col_q), row(col_k), row(col_v), tab, tab],
        out_specs=[tr, full, full, full, tr, pl.BlockSpec((None, 1, d), lambda i: (i, 0, 0))],
        out_shape=[jax.ShapeDtypeStruct((nt, d, tm), BF16),
                   jax.ShapeDtypeStruct((m, d), F32),
                   jax.ShapeDtypeStruct((m, d), BF16),
                   jax.ShapeDtypeStruct((m, d), F32),
                   jax.ShapeDtypeStruct((nt, d, tm), BF16),
                   jax.ShapeDtypeStruct((nt, 1, d), F32)],
        compiler_params=_cparams("parallel"),
        name="rope",
    )(h, h, h, cos, sin)


def _rope_rows_kernel(q_ref, k_ref, v_ref, cos_ref, sin_ref, qo_ref, ko_ref, vo_ref):
    cos = cos_ref[...]
    sin = sin_ref[...]
    for h in range(q_ref.shape[1] // B_HEAD_DIM):
        sl = slice(h * B_HEAD_DIM, (h + 1) * B_HEAD_DIM)
        qo_ref[:, sl] = _rotate(q_ref[:, sl], cos, sin)
        ko_ref[:, sl] = _rotate(k_ref[:, sl], cos, sin)
    vo_ref[...] = v_ref[...]


def _rope_rows(h, col_q, col_k, col_v, d, cos, sin):
    m = h.shape[0]
    row = lambda c: pl.BlockSpec((m, d), lambda i: (0, c))
    full = pl.BlockSpec((m, d), lambda i: (0, 0))
    tab = pl.BlockSpec((m, B_HEAD_DIM), lambda i: (0, 0))
    return pl.pallas_call(
        _rope_rows_kernel,
        grid=(1,),
        in_specs=[row(col_q), row(col_k), row(col_v), tab, tab],
        out_specs=[full, full, full],
        out_shape=[jax.ShapeDtypeStruct((m, d), F32)] * 3,
        compiler_params=_cparams("arbitrary"),
        name="rope_rows",
    )(h, h, h, cos, sin)


def _mlstm_kernel(q_ref, k_ref, v_ref, oa_ref, za_ref, gc_ref, gr_ref, ng_ref,
                  ya_ref, c_out, n_out, m_out, c_sc, n_sc, m_sc, *, scale):
    c = pl.program_id(1)
    nc = pl.num_programs(1)
    L = q_ref.shape[0]
    H = A_HEADS
    D = q_ref.shape[1] // H
    heads = range(H)
    hsl = [slice(h * D, (h + 1) * D) for h in heads]

    @pl.when(c == 0)
    def _():
        c_sc[...] = jnp.zeros(c_sc.shape, F32)
        n_sc[...] = jnp.zeros(n_sc.shape, F32)
        m_sc[...] = jnp.zeros(m_sc.shape, F32)

    gc = gc_ref[...]
    gr = gr_ref[...]
    t_i = lax.broadcasted_iota(jnp.int32, (L, L), 0)
    s_i = lax.broadcasted_iota(jnp.int32, (L, L), 1)
    causal = s_i <= t_i
    def head_gates(h):
        ig_col, fa_col = gc[:, h:h + 1], gc[:, h + H:h + H + 1]
        ig_row, fa_row = gr[h:h + 1, :], gr[h + H:h + H + 1, :]
        lf_col = _log_sigmoid(fa_col)
        lf_row = _log_sigmoid(fa_row)
        b_col = jnp.sum(jnp.where(causal, jnp.broadcast_to(lf_row, (L, L)), 0.0), axis=1, keepdims=True)
        b_row = jnp.sum(jnp.where(t_i <= s_i, jnp.broadcast_to(lf_col, (L, L)), 0.0), axis=0, keepdims=True)
        a_row = ig_row - b_row
        a_col = ig_col - b_col
        cm_col = jnp.max(jnp.where(causal, jnp.broadcast_to(a_row, (L, L)), -jnp.inf), axis=1, keepdims=True)
        m_prev = m_sc[h]
        g_col = jnp.maximum(m_prev, cm_col)
        g_last = jnp.maximum(m_prev, jnp.max(a_row, axis=1, keepdims=True))
        return dict(
            dmat=jnp.exp(jnp.where(causal, a_row - g_col, -jnp.inf)),
            inter=jnp.exp(m_prev - g_col),
            inv_floor=jnp.exp(-(b_col + g_col)),
            m_last=jnp.sum(lf_row, axis=1, keepdims=True) + g_last,
            w_col=jnp.exp(a_col - g_last),
            decay=jnp.exp(m_prev - g_last))

    def first_matmuls(h):
        q = q_ref[:, hsl[h]]
        ks = k_ref[:, hsl[h]] * scale
        qb = q.astype(BF16)
        s_raw = lax.dot_general(qb, ks.astype(BF16), _NT, preferred_element_type=F32)
        cq = lax.dot_general(qb, c_sc[h].astype(BF16), _NT, preferred_element_type=F32)
        return head_gates(h), q, ks, s_raw, cq

    def rest(h, g, q, ks, s_raw, cq):
        v = v_ref[:, hsl[h]]
        s = s_raw * g["dmat"]
        num = g["inter"] * cq + jnp.dot(s.astype(BF16), v.astype(BF16), preferred_element_type=F32)
        nvec = n_sc[h]
        den = g["inter"] * jnp.sum(q * nvec, axis=1, keepdims=True) + jnp.sum(s, axis=1, keepdims=True)
        hh = num / jnp.maximum(jnp.abs(den), g["inv_floor"])
        ya = (_layernorm(hh) * ng_ref[:, hsl[h]] * jax.nn.sigmoid(oa_ref[:, hsl[h]])
              * _silu(za_ref[:, hsl[h]]))
        ya_ref[:, hsl[h]] = ya.astype(ya_ref.dtype)
        kw = ks * g["w_col"]
        c_sc[h] = g["decay"] * c_sc[h] + jnp.dot(v.T.astype(BF16), kw.astype(BF16), preferred_element_type=F32)
        n_sc[h] = g["decay"] * nvec + jnp.sum(kw, axis=0, keepdims=True)
        m_sc[h] = g["m_last"]

    ahead = first_matmuls(0)
    for h in heads:
        cur = ahead
        if h + 1 < H:
            ahead = first_matmuls(h + 1)
        rest(h, *cur)

    @pl.when(c == nc - 1)
    def _():
        c_out[...] = c_sc[...]
        n_out[...] = n_sc[...]
        for h in heads:
            m_out[h] = jnp.broadcast_to(m_sc[h], m_out.shape[1:])


def _mlstm_prompt(h, gc, gr, ng, bsz, t, d):
    da = d // A_HEADS
    L = t if t <= A_CHUNK else math.gcd(t, A_CHUNK)
    nc = t // L
    H = A_HEADS
    blk = lambda g: pl.BlockSpec((L, d), lambda b, c: (b * nc + c, g))
    return pl.pallas_call(
        functools.partial(_mlstm_kernel, scale=da ** -0.5),
        grid=(bsz, nc),
        in_specs=[blk(0), blk(1), blk(2), blk(3), blk(4),
                  pl.BlockSpec((L, LANES), lambda b, c: (b * nc + c, 0)),
                  pl.BlockSpec((GATE_ROWS, L), lambda b, c: (0, b * nc + c)),
                  pl.BlockSpec((1, d), lambda b, c: (0, 0))],
        out_specs=[pl.BlockSpec((L, d), lambda b, c: (b * nc + c, 0)),
                   pl.BlockSpec((None, H, da, da), lambda b, c: (b, 0, 0, 0)),
                   pl.BlockSpec((None, H, 1, da), lambda b, c: (b, 0, 0, 0)),
                   pl.BlockSpec((None, H, 1, LANES), lambda b, c: (b, 0, 0, 0))],
        out_shape=[jax.ShapeDtypeStruct((bsz * t, d), BF16),
                   jax.ShapeDtypeStruct((bsz, H, da, da), F32),
                   jax.ShapeDtypeStruct((bsz, H, 1, da), F32),
                   jax.ShapeDtypeStruct((bsz, H, 1, LANES), F32)],
        scratch_shapes=[pltpu.VMEM((H, da, da), F32), pltpu.VMEM((H, 1, da), F32), pltpu.VMEM((H, 1, 1), F32)],
        compiler_params=_cparams("parallel", "arbitrary"),
        name="mlstm_prompt",
    )(h, h, h, h, h, gc, gr, ng)


def _moba_kernel(qt_ref, k_ref, vt_ref, km_ref, z_ref, o_ref, mem_sc, acc_sc):
    i = pl.program_id(2)
    blk = qt_ref.shape[1]
    nb = km_ref.shape[0]
    hd = B_HEAD_DIM
    heads = range(qt_ref.shape[0] // hd)
    hsl = [slice(h * hd, (h + 1) * hd) for h in heads]
    qts = [qt_ref[sl, :] for sl in hsl]

    row = lax.broadcasted_iota(jnp.int32, (nb, blk), 0)
    past = row < i
    key_i = lax.broadcasted_iota(jnp.int32, (blk, blk), 0)
    qry_i = lax.broadcasted_iota(jnp.int32, (blk, blk), 1)
    own = pl.ds(pl.multiple_of(i * blk, blk), blk)
    gates = [jnp.dot(km_ref[:, hsl[h]].astype(BF16), qts[h], preferred_element_type=F32) for h in heads]
    s_own = [jnp.dot(k_ref[own, hsl[h]], qts[h], preferred_element_type=F32) for h in heads]
    for h in heads:
        g = jnp.where(past, gates[h], -jnp.inf)
        rank = jnp.zeros(g.shape, F32)
        for n2 in range(nb):
            g2 = g[n2:n2 + 1, :]
            beats = (g2 > g) | ((g2 == g) & (n2 < row))
            rank = rank + jnp.where(beats, 1.0, 0.0)
        mem_sc[h] = jnp.where(past & (rank < MOBA_TOPK), 1.0, 0.0)
    m0, l0, p_own = [], [], []
    for h in heads:
        s = jnp.where(key_i <= qry_i, s_own[h], -jnp.inf)
        m = jnp.max(s, axis=0, keepdims=True)
        p = jnp.exp2(s - m)
        m0.append(m)
        l0.append(jnp.sum(p, axis=0, keepdims=True))
        p_own.append(p.astype(BF16))
    for h in heads:
        acc_sc[h] = jnp.dot(vt_ref[i, hsl[h], :], p_own[h], preferred_element_type=F32)

    def body(jj, carry):
        ms, ls = carry
        j0 = 2 * jj
        rows0 = pl.ds(pl.multiple_of(j0 * blk, blk), blk)
        rows1 = pl.ds(pl.multiple_of(j0 * blk + blk, blk), blk)
        new_m, new_l = [], []
        ss = []
        for h in heads:
            s0 = jnp.dot(k_ref[rows0, hsl[h]], qts[h], preferred_element_type=F32)
            s1 = jnp.dot(k_ref[rows1, hsl[h]], qts[h], preferred_element_type=F32)
            s0 = jnp.where(mem_sc[h, pl.ds(j0, 1), :] > 0.0, s0, -jnp.inf)
            s1 = jnp.where(mem_sc[h, pl.ds(j0 + 1, 1), :] > 0.0, s1, -jnp.inf)
            ss.append((s0, s1))
        ps = []
        for h in heads:
            s0, s1 = ss[h]
            m_new = jnp.maximum(ms[h], jnp.maximum(jnp.max(s0, axis=0, keepdims=True),
                                                   jnp.max(s1, axis=0, keepdims=True)))
            alpha = jnp.exp2(ms[h] - m_new)
            p0 = jnp.exp2(s0 - m_new)
            p1 = jnp.exp2(s1 - m_new)
            new_l.append(alpha * ls[h] + jnp.sum(p0, axis=0, keepdims=True) + jnp.sum(p1, axis=0, keepdims=True))
            new_m.append(m_new)
            ps.append((alpha, p0.astype(BF16), p1.astype(BF16)))
        for h in heads:
            alpha, p0, p1 = ps[h]
            acc_sc[h] = (alpha * acc_sc[h]
                         + jnp.dot(vt_ref[j0, hsl[h], :], p0, preferred_element_type=F32)
                         + jnp.dot(vt_ref[j0 + 1, hsl[h], :], p1, preferred_element_type=F32))
        return tuple(new_m), tuple(new_l)

    _, l_fin = lax.fori_loop(0, (i + 1) // 2, body, (tuple(m0), tuple(l0)))
    outs = [(acc_sc[h] / l_fin[h]).T for h in heads]
    for h in heads:
        o_ref[:, hsl[h]] = (outs[h] * _silu(z_ref[:, hsl[h]])).astype(o_ref.dtype)


def _moba_prompt(qt, kb, vt, kmean, h, col_z, bsz, t, d):
    nh = d // B_HEAD_DIM
    nb = t // MOBA_BLOCK
    blk = MOBA_BLOCK
    hps = MOBA_HEADS_PER_STEP if nh % MOBA_HEADS_PER_STEP == 0 else 1
    w = hps * B_HEAD_DIM
    ng = nh // hps
    return pl.pallas_call(
        _moba_kernel,
        grid=(bsz, ng, nb),
        in_specs=[pl.BlockSpec((None, w, blk), lambda b, g, i: (b * nb + i, g, 0)),
                  pl.BlockSpec((t, w), lambda b, g, i: (b, g)),
                  pl.BlockSpec((nb, w, blk), lambda b, g, i: (b, g, 0)),
                  pl.BlockSpec((None, nb, w), lambda b, g, i: (b, 0, g)),
                  pl.BlockSpec((blk, w), lambda b, g, i: (b * nb + i, col_z * ng + g))],
        out_specs=pl.BlockSpec((blk, w), lambda b, g, i: (b * nb + i, g)),
        out_shape=jax.ShapeDtypeStruct((bsz * t, d), BF16),
        scratch_shapes=[pltpu.VMEM((hps, nb, blk), F32), pltpu.VMEM((hps, B_HEAD_DIM, blk), F32)],
        compiler_params=_cparams("parallel", "parallel", "arbitrary"),
        name="moba_prompt",
    )(qt, kb, vt, kmean, h)


def _merge_kernel(ya_ref, yb_ref, ga_ref, gb_ref, wa_ref, wb_ref, o_ref):
    pa = jnp.dot(ya_ref[...].astype(BF16), wa_ref[...], preferred_element_type=F32)
    pb = jnp.dot(yb_ref[...].astype(BF16), wb_ref[...], preferred_element_type=F32)
    o_ref[...] = (jax.nn.sigmoid(ga_ref[...]) * pa + jax.nn.sigmoid(gb_ref[...]) * pb).astype(o_ref.dtype)


def _merge(ya, yb, h, col_ga, col_gb, wa, wb, tm, tn):
    m, d = ya.shape
    nj = d // tn
    return pl.pallas_call(
        _merge_kernel,
        grid=(m // tm, nj),
        in_specs=[pl.BlockSpec((tm, d), lambda i, j: (i, 0)),
                  pl.BlockSpec((tm, d), lambda i, j: (i, 0)),
                  pl.BlockSpec((tm, tn), lambda i, j: (i, col_ga * nj + j)),
                  pl.BlockSpec((tm, tn), lambda i, j: (i, col_gb * nj + j)),
                  pl.BlockSpec((d, tn), lambda i, j: (0, j)),
                  pl.BlockSpec((d, tn), lambda i, j: (0, j))],
        out_specs=pl.BlockSpec((tm, tn), lambda i, j: (i, j)),
        out_shape=jax.ShapeDtypeStruct((m, d), BF16),
        compiler_params=_cparams("parallel", "arbitrary"),
        name="merge",
    )(ya, yb, h, h, wa, wb)


def _out_kernel(mg_ref, wo_ref, x_ref, g_ref, b_ref, y_ref, *, alpha):
    out = jnp.dot(mg_ref[...], wo_ref[...], preferred_element_type=F32)
    y_ref[...] = _layernorm(alpha * x_ref[...] + out) * g_ref[...] + b_ref[...]


def _out(mg, wo, x, g, b, tm, alpha):
    m, d = x.shape
    row = pl.BlockSpec((tm, d), lambda i: (i, 0))
    vec = pl.BlockSpec((1, d), lambda i: (0, 0))
    return pl.pallas_call(
        functools.partial(_out_kernel, alpha=alpha),
        grid=(m // tm,),
        in_specs=[row, pl.BlockSpec((d, d), lambda i: (0, 0)), row, vec, vec],
        out_specs=row,
        out_shape=jax.ShapeDtypeStruct((m, d), F32),
        compiler_params=_cparams("parallel"),
        name="outproj",
    )(mg, wo, x, g, b)


def _mlstm_step_kernel(q_ref, k_ref, v_ref, oa_ref, za_ref, gc_ref, ng_ref, c_ref, n_ref, m_ref,
                       ya_ref, c_out, n_out, m_out, *, scale):
    H = A_HEADS
    da = q_ref.shape[1] // H
    gc = gc_ref[...]
    eye = lax.broadcasted_iota(jnp.int32, (da, da), 0) == lax.broadcasted_iota(jnp.int32, (da, da), 1)
    for h in range(H):
        sl = slice(h * da, (h + 1) * da)
        ig = gc[:, h:h + 1]
        lf = _log_sigmoid(gc[:, h + H:h + H + 1])
        m_prev = m_ref[h][:, :1]
        a = ig - lf
        g = jnp.maximum(m_prev, a)
        m_t = lf + g
        inter = jnp.exp(m_prev - g)
        dm = jnp.exp(a - g)

        q = q_ref[:, sl]
        ks = k_ref[:, sl] * scale
        v = v_ref[:, sl]
        cmat = c_ref[h]
        nvec = n_ref[h]
        q8 = jnp.broadcast_to(q, (8, da)).astype(BF16)
        cq = lax.dot_general(q8, cmat.astype(BF16), _NT, preferred_element_type=F32)[0:1]
        s = jnp.sum(q * ks, axis=1, keepdims=True) * dm
        num = inter * cq + s * v
        den = inter * jnp.sum(nvec * q, axis=1, keepdims=True) + s
        hh = num / jnp.maximum(jnp.abs(den), jnp.exp(-m_t))

        v_col = jnp.sum(jnp.where(eye, jnp.broadcast_to(v, (da, da)), 0.0), axis=1, keepdims=True)
        c_out[h] = inter * cmat + (dm * v_col) * ks
        n_out[h] = inter * nvec + dm * ks
        m_out[h] = jnp.broadcast_to(m_t, m_out.shape[1:])
        ya = _layernorm(hh) * ng_ref[:, sl] * jax.nn.sigmoid(oa_ref[:, sl]) * _silu(za_ref[:, sl])
        ya_ref[:, sl] = ya.astype(ya_ref.dtype)


def _mlstm_step(h3, gc3, ng, c_all, c_row0, n0, m0, d):
    bs = h3.shape[0]
    H = A_HEADS
    da = d // H
    row = lambda g: pl.BlockSpec((None, 1, d), lambda b: (b, 0, g))
    st = lambda r, c: pl.BlockSpec((None, H, r, c), lambda b: (b, 0, 0, 0))
    return pl.pallas_call(
        functools.partial(_mlstm_step_kernel, scale=da ** -0.5),
        grid=(bs,),
        in_specs=[row(0), row(1), row(2), row(3), row(4),
                  pl.BlockSpec((None, 1, LANES), lambda b: (b, 0, 0)),
                  pl.BlockSpec((1, d), lambda b: (0, 0)),
                  pl.BlockSpec((None, H, da, da), lambda b: (c_row0 + b, 0, 0, 0)),
                  st(1, da), st(1, LANES)],
        out_specs=[pl.BlockSpec((None, 1, d), lambda b: (b, 0, 0)),
                   st(da, da), st(1, da), st(1, LANES)],
        out_shape=[jax.ShapeDtypeStruct((bs, 1, d), F32),
                   jax.ShapeDtypeStruct((bs, H, da, da), F32),
                   jax.ShapeDtypeStruct((bs, H, 1, da), F32),
                   jax.ShapeDtypeStruct((bs, H, 1, LANES), F32)],
        compiler_params=_cparams("parallel"),
        name="mlstm_step",
    )(h3, h3, h3, h3, h3, gc3, ng, c_all, n0, m0)


def _topk_kernel(q_ref, km_ref, o_ref):
    q = q_ref[...]
    nb = km_ref.shape[0]
    lane = lax.broadcasted_iota(jnp.int32, o_ref.shape, 1)
    gate = jnp.full(o_ref.shape, -jnp.inf, F32)
    for n in range(nb):
        gate = jnp.where(lane == n, jnp.sum(km_ref[n] * q, axis=1, keepdims=True), gate)
    rank = jnp.zeros(o_ref.shape, F32)
    for n2 in range(nb):
        g2 = gate[:, n2:n2 + 1]
        beats = (g2 > gate) | ((g2 == gate) & (n2 < lane))
        rank = rank + jnp.where(beats, 1.0, 0.0)
    lane_f = lane.astype(F32)
    out = jnp.zeros(o_ref.shape, F32)
    for r in range(MOBA_TOPK):
        idx = jnp.sum(jnp.where((rank == r) & (lane < nb), lane_f, 0.0), axis=1, keepdims=True)
        out = jnp.where(lane == r, idx, out)
    o_ref[...] = out.astype(jnp.int32)


def _topk_blocks(q4, kmean):
    bs, nb, nh, hd = kmean.shape
    assert nb <= LANES
    return pl.pallas_call(
        _topk_kernel,
        grid=(bs,),
        in_specs=[pl.BlockSpec((None, nh, hd), lambda b: (b, 0, 0)),
                  pl.BlockSpec((None, nb, nh, hd), lambda b: (b, 0, 0, 0))],
        out_specs=pl.BlockSpec((None, nh, LANES), lambda b: (b, 0, 0)),
        out_shape=jax.ShapeDtypeStruct((bs, nh, LANES), jnp.int32),
        compiler_params=_cparams("parallel"),
        name="topk_blocks",
    )(q4, kmean)


def _decode_attn_kernel(pt_ref, ix_ref, q_ref, kn_ref, vn_ref, z_ref, ck_hbm, cv_hbm, o_ref,
                        kbuf, vbuf, sems, *, per_blk, scale):
    b = pl.program_id(0)
    nbat = pl.num_programs(0)
    nh = q_ref.shape[0]
    n_pages = MOBA_TOPK * per_blk

    def page_copies(bb, slot, h, p):
        r, half = divmod(p, per_blk)
        pid = pt_ref[bb, ix_ref[bb, h * MOBA_TOPK + r] * per_blk + half]
        return (pltpu.make_async_copy(ck_hbm.at[pid, :, h, :], kbuf.at[slot, h, p], sems.at[0, slot]),
                pltpu.make_async_copy(cv_hbm.at[pid, :, h, :], vbuf.at[slot, h, p], sems.at[1, slot]))

    def for_each_copy(bb, slot, fn):
        def per_head(h, carry):
            for p in range(n_pages):
                for cp in page_copies(bb, slot, h, p):
                    fn(cp)
            return carry
        lax.fori_loop(0, nh, per_head, 0)

    @pl.when(b == 0)
    def _():
        for_each_copy(0, 0, lambda cp: cp.start())

    @pl.when(b + 1 < nbat)
    def _():
        for_each_copy(b + 1, (b + 1) % 2, lambda cp: cp.start())

    slot = b % 2
    for_each_copy(b, slot, lambda cp: cp.wait())

    def per_head(h, carry):
        row = pl.ds(h, 1)
        q = q_ref[row, :]
        scores = [jnp.sum(kbuf[slot, h, p] * q, axis=1, keepdims=True) * scale for p in range(n_pages)]
        s_own = jnp.sum(kn_ref[row, :] * q, axis=1, keepdims=True) * scale
        m = s_own
        for sc in scores:
            m = jnp.maximum(m, jnp.max(sc, axis=0, keepdims=True))
        e_own = jnp.exp(s_own - m)
        l = e_own
        acc = e_own * vn_ref[row, :]
        for p, sc in enumerate(scores):
            e = jnp.exp(sc - m)
            l = l + jnp.sum(e, axis=0, keepdims=True)
            acc = acc + jnp.sum(e * vbuf[slot, h, p], axis=0, keepdims=True)
        o_ref[row, :] = acc / l * _silu(z_ref[row, :])
        return carry

    lax.fori_loop(0, nh, per_head, 0)


def _decode_attn(cache_k4, cache_v4, page_ids, idx, q4, kn4, vn4, z4):
    _, page, nh, hd = cache_k4.shape
    bs = page_ids.shape[0]
    per_blk = MOBA_BLOCK // page
    n_pages = MOBA_TOPK * per_blk
    vec = pl.BlockSpec((None, nh, hd), lambda b, pt, ix: (b, 0, 0))
    hbm = pl.BlockSpec(memory_space=pl.ANY)
    return pl.pallas_call(
        functools.partial(_decode_attn_kernel, per_blk=per_blk, scale=hd ** -0.5),
        grid_spec=pltpu.PrefetchScalarGridSpec(
            num_scalar_prefetch=2,
            grid=(bs,),
            in_specs=[vec, vec, vec, vec, hbm, hbm],
            out_specs=vec,
            scratch_shapes=[pltpu.VMEM((2, nh, n_pages, page, hd), F32),
                            pltpu.VMEM((2, nh, n_pages, page, hd), F32),
                            pltpu.SemaphoreType.DMA((2, 2))]),
        out_shape=jax.ShapeDtypeStruct((bs, nh, hd), F32),
        compiler_params=_cparams("arbitrary"),
        name="decode_attn",
    )(page_ids, idx, q4, kn4, vn4, z4, cache_k4, cache_v4)


def _rope_tables(pos):
    half = B_HEAD_DIM // 2
    freqs = ROPE_THETA ** (-jnp.arange(half, dtype=F32) / half)
    ang = pos.astype(F32)[:, None] * freqs[None, :]
    cos = jnp.cos(ang)
    sin = jnp.sin(ang)
    return jnp.concatenate([cos, cos], -1), jnp.concatenate([-sin, sin], -1)


def _tile(n, pref):
    return pref if n % pref == 0 else n


def _prep_weights(w_in_all, layer, b_in, w_proj_a, w_proj_b, w_out, d):
    c0 = 5 * d
    c1 = c0 + 2 * A_HEADS
    w_in = w_in_all[layer]
    w_a = w_in.astype(BF16)
    w_b = w_a[:, c1:]
    w_g = w_in[:, c0:c1]
    wc = jnp.pad(w_g, ((0, 0), (0, LANES - 2 * A_HEADS))).astype(BF16)
    wr = w_g.T.astype(BF16)
    bc = jnp.pad(b_in[c0:c1], (0, LANES - 2 * A_HEADS))[None, :]
    br = b_in[c0:c1][:, None]
    return ((w_a, b_in[None, :c0]), (w_b, b_in[None, c1:]),
            wc, wr, bc, br, w_proj_a.astype(BF16), w_proj_b.astype(BF16), w_out.astype(BF16))


_QA, _KA, _VA, _OA, _ZA = range(5)
_QB, _KB, _VB, _ZB, _GA, _GB = range(6)


def _project(xb, part, tm):
    w, b = part
    return _inproj(xb, w, b, tm, _tile(b.shape[1], 1024))


def _project_with_means(xb, parts, tm, cache_k4, page_ids):
    _, page, nh, hd = cache_k4.shape
    bs, n_pages = page_ids.shape
    n_blocks = bs * n_pages * page // MOBA_BLOCK
    tiles = [(tm, _tile(b.shape[1], 1024)) for _, b in parts]
    steps = [(xb.shape[0] // tm_) * (b.shape[1] // tn_) for (tm_, tn_), (_, b) in zip(tiles, parts)]
    bps = -(-n_blocks // sum(steps))
    hs, means, first = [], [], 0
    for (w, b), (tm_, tn_), n_steps in zip(parts, tiles, steps):
        h, km = _inproj_means(xb, w, b, tm_, tn_, page_ids, cache_k4, first, bps)
        hs.append(h)
        means.append(km)
        first += n_steps * bps
    kmean = jnp.concatenate(means)[:n_blocks].reshape(bs, n_blocks // bs, nh, hd)
    return hs[0], hs[1], kmean


def _layer_prompt(x, wts, ng, ln_g, ln_b, alpha, cache_k4, page_ids):
    bsz, t, d = x.shape
    m = bsz * t
    part_a, part_b, wc, wr, bc, br, wa, wb, wo = wts
    x2 = x.reshape(m, d)
    xb, gc, gr = _gates(x2, wc, wr, bc, br, _tile(m, 512))
    h_a, h_b, kmean_cache = _project_with_means(xb, (part_a, part_b), _tile(m, 1024), cache_k4, page_ids)
    ya, c1, n1, m1 = _mlstm_prompt(h_a, gc, gr, ng, bsz, t, d)
    cos, sin = _rope_tables(jnp.arange(t, dtype=jnp.int32))
    cos = jnp.tile(cos, (bsz, 1))
    sin = jnp.tile(sin, (bsz, 1))
    q_t, k_new, k_b, v_new, v_t, kmean = _rope(h_b, _QB, _KB, _VB, d, cos, sin, B_HEAD_DIM ** -0.5 * LOG2E)
    nb = t // MOBA_BLOCK
    yb = _moba_prompt(q_t, k_b, v_t, kmean.reshape(bsz, nb, d), h_b, _ZB, bsz, t, d)
    mg = _merge(ya, yb, h_b, _GA, _GB, wa, wb, _tile(m, 1024), _tile(d, 512))
    y = _out(mg, wo, x2, ln_g, ln_b, _tile(m, 512), alpha)
    nh = d // B_HEAD_DIM
    return (kmean_cache, y.reshape(bsz, t, d), k_new.reshape(bsz, t, nh, B_HEAD_DIM),
            v_new.reshape(bsz, t, nh, B_HEAD_DIM), c1, n1.reshape(bsz, A_HEADS, d // A_HEADS), m1[:, :, 0, 0])


def _layer_sample(x, wts, ng, ln_g, ln_b, alpha, cache_k4, cache_v4, page_ids, kmean, c_all, c_row0, n0, m0):
    bs, t, d = x.shape
    part_a, part_b, wc, wr, bc, br, wa, wb, wo = wts
    _, page, nh, hd = cache_k4.shape
    past = page_ids.shape[1] * page
    da = d // A_HEADS
    x2 = x.reshape(bs, d)
    xb, gc, _ = _gates(x2, wc, wr, bc, br, bs)
    h_a = _project(xb, part_a, bs)
    h_b = _project(xb, part_b, bs)
    ya, c1, n1, m1 = _mlstm_step(h_a.reshape(bs, 1, h_a.shape[1]), gc.reshape(bs, 1, LANES), ng, c_all, c_row0,
                                 n0.reshape(bs, A_HEADS, 1, da),
                                 jnp.broadcast_to(m0[:, :, None, None], (bs, A_HEADS, 1, LANES)), d)
    cos, sin = _rope_tables(jnp.full((bs,), past, dtype=jnp.int32))
    q_r, k_new, v_new = _rope_rows(h_b, _QB, _KB, _VB, d, cos, sin)
    heads = lambda a: a.reshape(bs, nh, hd)
    q4, kn4, vn4 = heads(q_r), heads(k_new), heads(v_new)
    idx = _topk_blocks(q4, kmean)[:, :, :MOBA_TOPK].reshape(bs, nh * MOBA_TOPK)
    yb = _decode_attn(cache_k4, cache_v4, page_ids, idx, q4, kn4, vn4, heads(h_b[:, _ZB * d:(_ZB + 1) * d]))
    mg = _merge(ya.reshape(bs, d), yb.reshape(bs, d), h_b, _GA, _GB, wa, wb, bs, _tile(d, 512))
    y = _out(mg, wo, x2, ln_g, ln_b, bs, alpha)
    return (y.reshape(bs, 1, d), kn4[:, None], vn4[:, None],
            c1, n1.reshape(bs, A_HEADS, da), m1[:, :, 0, 0])


def kernel(x_prompt, x_sample, cache_k, cache_v, page_table, state_mlstm_C, state_mlstm_n, state_mlstm_m,
           w_in, b_in, mh_norm_g, w_proj_a, w_proj_b, w_out, ln_g, ln_b):
    depth = w_in.shape[0]
    d = x_prompt.shape[-1]
    n_pages = page_table.shape[1]
    page = cache_k.shape[2]
    assert x_sample.shape[1] == 1 and d % (A_HEADS * LANES) == 0
    assert x_prompt.shape[1] % MOBA_BLOCK == 0 and MOBA_BLOCK % page == 0
    assert (n_pages * page) % MOBA_BLOCK == 0 and n_pages * page // MOBA_BLOCK >= MOBA_TOPK
    alpha = (2 * depth) ** 0.25
    n_phys = cache_k.shape[1]
    bs = x_sample.shape[0]
    cache_k3 = cache_k.reshape((depth * n_phys,) + cache_k.shape[2:])
    cache_v3 = cache_v.reshape((depth * n_phys,) + cache_v.shape[2:])
    c_all = state_mlstm_C.reshape((depth * bs,) + state_mlstm_C.shape[2:])
    yp, ys = x_prompt, x_sample
    new_p = [[] for _ in range(5)]
    new_s = [[] for _ in range(5)]
    for l in range(depth):
        wts = _prep_weights(w_in, l, b_in[l], w_proj_a[l], w_proj_b[l], w_out[l], d)
        ng, lg, lb = mh_norm_g[l][None, :], ln_g[l][None, :], ln_b[l][None, :]
        page_ids = page_table + l * n_phys
        kmean, yp, *st_p = _layer_prompt(yp, wts, ng, lg, lb, alpha, cache_k3, page_ids)
        ys, *st_s = _layer_sample(ys, wts, ng, lg, lb, alpha, cache_k3, cache_v3, page_ids, kmean,
                                  c_all, l * bs, state_mlstm_n[l], state_mlstm_m[l])
        for i in range(5):
            new_p[i].append(st_p[i])
            new_s[i].append(st_s[i])
    stack = lambda ts: ts[0][None] if len(ts) == 1 else jnp.stack(ts)
    k_p, v_p, c_p, n_p, m_p = [stack(t) for t in new_p]
    k_s, v_s, c_s, n_s, m_s = [stack(t) for t in new_s]
    return (yp, ys, k_p, v_p, c_p, n_p, m_p, k_s, v_s, c_s, n_s, m_s)
```

```python
import functools
import math

import jax
import jax.numpy as jnp
from jax import lax
from jax.experimental import pallas as pl
from jax.experimental.pallas import tpu as pltpu

F32 = jnp.float32
BF16 = jnp.bfloat16

A_HEADS = 4
A_CHUNK = 128
B_HEAD_DIM = 128
MOBA_BLOCK = 256
MOBA_TOPK = 3
ROPE_THETA = 10000.0
LN_EPS = 1e-5
LOG2E = math.log2(math.e)
LANES = 128
GATE_ROWS = 8
MOBA_HEADS_PER_STEP = 8
VMEM_LIMIT = 60 * 1024 * 1024

_NT = (((1,), (1,)), ((), ()))


def _cparams(*sem):
    return pltpu.CompilerParams(dimension_semantics=sem, vmem_limit_bytes=VMEM_LIMIT)


def _log_sigmoid(x):
    return jnp.minimum(x, 0.0) - jnp.log1p(jnp.exp(-jnp.abs(x)))


def _silu(x):
    return x * jax.nn.sigmoid(x)


def _layernorm(x):
    mu = jnp.mean(x, axis=-1, keepdims=True)
    xc = x - mu
    return xc * lax.rsqrt(jnp.mean(xc * xc, axis=-1, keepdims=True) + LN_EPS)


def _inproj_kernel(x_ref, w_ref, b_ref, o_ref):
    o_ref[...] = jnp.dot(x_ref[...], w_ref[...], preferred_element_type=F32) + b_ref[...]


def _inproj(xb, w, b, tm, tn):
    m, k = xb.shape
    n = b.shape[1]
    return pl.pallas_call(
        _inproj_kernel,
        grid=(n // tn, m // tm),
        in_specs=[pl.BlockSpec((tm, k), lambda j, i: (i, 0)),
                  pl.BlockSpec((k, tn), lambda j, i: (0, j)),
                  pl.BlockSpec((1, tn), lambda j, i: (0, j))],
        out_specs=pl.BlockSpec((tm, tn), lambda j, i: (i, j)),
        out_shape=jax.ShapeDtypeStruct((m, n), F32),
        compiler_params=_cparams("parallel", "parallel"),
        name="inproj",
    )(xb, w, b)


def _inproj_means_kernel(pt_ref, x_ref, w_ref, b_ref, ck_hbm, o_ref, km_ref, pbuf, sems,
                         *, first_block, n_blocks, per_blk, n_pages):
    ni = pl.num_programs(1)
    step = pl.program_id(0) * ni + pl.program_id(1)
    n_steps = pl.num_programs(0) * ni
    bps = km_ref.shape[0]

    def page_copy(s, slot, r, p):
        blk = jnp.minimum(first_block + s * bps + r, n_blocks - 1)
        g = blk * per_blk + p
        pid = pt_ref[g // n_pages, g % n_pages]
        return pltpu.make_async_copy(ck_hbm.at[pid], pbuf.at[slot, r * per_blk + p], sems.at[slot])

    def for_each_copy(s, slot, fn):
        for r in range(bps):
            for p in range(per_blk):
                fn(page_copy(s, slot, r, p))

    @pl.when(step == 0)
    def _():
        for_each_copy(0, 0, lambda cp: cp.start())

    @pl.when(step + 1 < n_steps)
    def _():
        for_each_copy(step + 1, (step + 1) % 2, lambda cp: cp.start())

    slot = step % 2
    for_each_copy(step, slot, lambda cp: cp.wait())
    o_ref[...] = jnp.dot(x_ref[...], w_ref[...], preferred_element_type=F32) + b_ref[...]
    rows = per_blk * pbuf.shape[2]
    for r in range(bps):
        tot = jnp.sum(pbuf[slot, r * per_blk], axis=0)
        for p in range(1, per_blk):
            tot = tot + jnp.sum(pbuf[slot, r * per_blk + p], axis=0)
        km_ref[r] = tot * (1.0 / rows)


def _inproj_means(xb, w, b, tm, tn, page_ids, cache4, first_block, blocks_per_step):
    m, k = xb.shape
    n = b.shape[1]
    _, page, nh, hd = cache4.shape
    bs, n_pages = page_ids.shape
    per_blk = MOBA_BLOCK // page
    nj, ni = n // tn, m // tm
    bps = blocks_per_step
    return pl.pallas_call(
        functools.partial(_inproj_means_kernel, first_block=first_block, n_blocks=bs * n_pages // per_blk,
                          per_blk=per_blk, n_pages=n_pages),
        grid_spec=pltpu.PrefetchScalarGridSpec(
            num_scalar_prefetch=1,
            grid=(nj, ni),
            in_specs=[pl.BlockSpec((tm, k), lambda j, i, pt: (i, 0)),
                      pl.BlockSpec((k, tn), lambda j, i, pt: (0, j)),
                      pl.BlockSpec((1, tn), lambda j, i, pt: (0, j)),
                      pl.BlockSpec(memory_space=pl.ANY)],
            out_specs=[pl.BlockSpec((tm, tn), lambda j, i, pt: (i, j)),
                       pl.BlockSpec((bps, nh, hd), lambda j, i, pt: (j * ni + i, 0, 0))],
            scratch_shapes=[pltpu.VMEM((2, bps * per_blk, page, nh, hd), F32),
                            pltpu.SemaphoreType.DMA((2,))]),
        out_shape=[jax.ShapeDtypeStruct((m, n), F32),
                   jax.ShapeDtypeStruct((nj * ni * bps, nh, hd), F32)],
        compiler_params=_cparams("arbitrary", "arbitrary"),
        name="inproj_means",
    )(page_ids, xb, w, b, cache4)


def _gates_kernel(x_ref, wc_ref, wr_ref, bc_ref, br_ref, xb_ref, gc_ref, gr_ref):
    xb = x_ref[...].astype(BF16)
    xb_ref[...] = xb
    gc_ref[...] = jnp.dot(xb, wc_ref[...], preferred_element_type=F32) + bc_ref[...]
    gr_ref[...] = lax.dot_general(wr_ref[...], xb, _NT, preferred_element_type=F32) + br_ref[...]


def _gates(x, wc, wr, bc, br, tm):
    m, k = x.shape
    return pl.pallas_call(
        _gates_kernel,
        grid=(m // tm,),
        in_specs=[pl.BlockSpec((tm, k), lambda i: (i, 0)),
                  pl.BlockSpec((k, LANES), lambda i: (0, 0)),
                  pl.BlockSpec((GATE_ROWS, k), lambda i: (0, 0)),
                  pl.BlockSpec((1, LANES), lambda i: (0, 0)),
                  pl.BlockSpec((GATE_ROWS, 1), lambda i: (0, 0))],
        out_specs=[pl.BlockSpec((tm, k), lambda i: (i, 0)),
                   pl.BlockSpec((tm, LANES), lambda i: (i, 0)),
                   pl.BlockSpec((GATE_ROWS, tm), lambda i: (0, i))],
        out_shape=[jax.ShapeDtypeStruct((m, k), BF16),
                   jax.ShapeDtypeStruct((m, LANES), F32),
                   jax.ShapeDtypeStruct((GATE_ROWS, m), F32)],
        compiler_params=_cparams("parallel"),
        name="gates",
    )(x, wc, wr, bc, br)


def _rotate(x, cos, sin):
    return x * cos + pltpu.roll(x, B_HEAD_DIM // 2, 1) * sin


def _rope_kernel(q_ref, k_ref, v_ref, cos_ref, sin_ref, qt_ref, ko_ref, kb_ref, vo_ref, vt_ref, km_ref, *, scale):
    cos = cos_ref[...]
    sin = sin_ref[...]
    tm, d = q_ref.shape
    for h in range(d // B_HEAD_DIM):
        sl = slice(h * B_HEAD_DIM, (h + 1) * B_HEAD_DIM)
        qr = _rotate(q_ref[:, sl], cos, sin)
        kr = _rotate(k_ref[:, sl], cos, sin)
        v = v_ref[:, sl]
        qt_ref[sl, :] = (qr * scale).T.astype(BF16)
        ko_ref[:, sl] = kr
        kb_ref[:, sl] = kr.astype(BF16)
        km_ref[:, sl] = jnp.sum(kr, axis=0, keepdims=True) * (1.0 / tm)
        vo_ref[:, sl] = v
        vt_ref[sl, :] = v.T.astype(BF16)


def _rope(h, col_q, col_k, col_v, d, cos, sin, scale):
    m = h.shape[0]
    tm = MOBA_BLOCK
    nt = m // tm
    row = lambda c: pl.BlockSpec((tm, d), lambda i: (i, c))
    full = pl.BlockSpec((tm, d), lambda i: (i, 0))
    tab = pl.BlockSpec((tm, B_HEAD_DIM), lambda i: (i, 0))
    tr = pl.BlockSpec((None, d, tm), lambda i: (i, 0, 0))
    return pl.pallas_call(
        functools.partial(_rope_kernel, scale=scale),
        grid=(nt,),
        in_specs=[row(col_q), row(col_k), row(col_v), tab, tab],
        out_specs=[tr, full, full, full, tr, pl.BlockSpec((None, 1, d), lambda i: (i, 0, 0))],
        out_shape=[jax.ShapeDtypeStruct((nt, d, tm), BF16),
                   jax.ShapeDtypeStruct((m, d), F32),
                   jax.ShapeDtypeStruct((m, d), BF16),
                   jax.ShapeDtypeStruct((m, d), F32),
                   jax.ShapeDtypeStruct((nt, d, tm), BF16),
                   jax.ShapeDtypeStruct((nt, 1, d), F32)],
        compiler_params=_cparams("parallel"),
        name="rope",
    )(h, h, h, cos, sin)


def _rope_rows_kernel(q_ref, k_ref, v_ref, cos_ref, sin_ref, qo_ref, ko_ref, vo_ref):
    cos = cos_ref[...]
    sin = sin_ref[...]
    for h in range(q_ref.shape[1] // B_HEAD_DIM):
        sl = slice(h * B_HEAD_DIM, (h + 1) * B_HEAD_DIM)
        qo_ref[:, sl] = _rotate(q_ref[:, sl], cos, sin)
        ko_ref[:, sl] = _rotate(k_ref[:, sl], cos, sin)
    vo_ref[...] = v_ref[...]


def _rope_rows(h, col_q, col_k, col_v, d, cos, sin):
    m = h.shape[0]
    row = lambda c: pl.BlockSpec((m, d), lambda i: (0, c))
    full = pl.BlockSpec((m, d), lambda i: (0, 0))
    tab = pl.BlockSpec((m, B_HEAD_DIM), lambda i: (0, 0))
    return pl.pallas_call(
        _rope_rows_kernel,
        grid=(1,),
        in_specs=[row(col_q), row(col_k), row(col_v), tab, tab],
        out_specs=[full, full, full],
        out_shape=[jax.ShapeDtypeStruct((m, d), F32)] * 3,
        compiler_params=_cparams("arbitrary"),
        name="rope_rows",
    )(h, h, h, cos, sin)


def _mlstm_kernel(q_ref, k_ref, v_ref, oa_ref, za_ref, gc_ref, gr_ref, ng_ref,
                  ya_ref, c_out, n_out, m_out, c_sc, n_sc, m_sc, *, scale):
    c = pl.program_id(1)
    nc = pl.num_programs(1)
    L = q_ref.shape[0]
    H = A_HEADS
    D = q_ref.shape[1] // H
    heads = range(H)
    hsl = [slice(h * D, (h + 1) * D) for h in heads]

    @pl.when(c == 0)
    def _():
        c_sc[...] = jnp.zeros(c_sc.shape, F32)
        n_sc[...] = jnp.zeros(n_sc.shape, F32)
        m_sc[...] = jnp.zeros(m_sc.shape, F32)

    gc = gc_ref[...]
    gr = gr_ref[...]
    t_i = lax.broadcasted_iota(jnp.int32, (L, L), 0)
    s_i = lax.broadcasted_iota(jnp.int32, (L, L), 1)
    causal = s_i <= t_i
    def head_gates(h):
        ig_col, fa_col = gc[:, h:h + 1], gc[:, h + H:h + H + 1]
        ig_row, fa_row = gr[h:h + 1, :], gr[h + H:h + H + 1, :]
        lf_col = _log_sigmoid(fa_col)
        lf_row = _log_sigmoid(fa_row)
        b_col = jnp.sum(jnp.where(causal, jnp.broadcast_to(lf_row, (L, L)), 0.0), axis=1, keepdims=True)
        b_row = jnp.sum(jnp.where(t_i <= s_i, jnp.broadcast_to(lf_col, (L, L)), 0.0), axis=0, keepdims=True)
        a_row = ig_row - b_row
        a_col = ig_col - b_col
        cm_col = jnp.max(jnp.where(causal, jnp.broadcast_to(a_row, (L, L)), -jnp.inf), axis=1, keepdims=True)
        m_prev = m_sc[h]
        g_col = jnp.maximum(m_prev, cm_col)
        g_last = jnp.maximum(m_prev, jnp.max(a_row, axis=1, keepdims=True))
        return dict(
            dmat=jnp.exp(jnp.where(causal, a_row - g_col, -jnp.inf)),
            inter=jnp.exp(m_prev - g_col),
            inv_floor=jnp.exp(-(b_col + g_col)),
            m_last=jnp.sum(lf_row, axis=1, keepdims=True) + g_last,
            w_col=jnp.exp(a_col - g_last),
            decay=jnp.exp(m_prev - g_last))

    def first_matmuls(h):
        q = q_ref[:, hsl[h]]
        ks = k_ref[:, hsl[h]] * scale
        qb = q.astype(BF16)
        s_raw = lax.dot_general(qb, ks.astype(BF16), _NT, preferred_element_type=F32)
        cq = lax.dot_general(qb, c_sc[h].astype(BF16), _NT, preferred_element_type=F32)
        return head_gates(h), q, ks, s_raw, cq

    def rest(h, g, q, ks, s_raw, cq):
        v = v_ref[:, hsl[h]]
        s = s_raw * g["dmat"]
        num = g["inter"] * cq + jnp.dot(s.astype(BF16), v.astype(BF16), preferred_element_type=F32)
        nvec = n_sc[h]
        den = g["inter"] * jnp.sum(q * nvec, axis=1, keepdims=True) + jnp.sum(s, axis=1, keepdims=True)
        hh = num / jnp.maximum(jnp.abs(den), g["inv_floor"])
        ya = (_layernorm(hh) * ng_ref[:, hsl[h]] * jax.nn.sigmoid(oa_ref[:, hsl[h]])
              * _silu(za_ref[:, hsl[h]]))
        ya_ref[:, hsl[h]] = ya.astype(ya_ref.dtype)
        kw = ks * g["w_col"]
        c_sc[h] = g["decay"] * c_sc[h] + jnp.dot(v.T.astype(BF16), kw.astype(BF16), preferred_element_type=F32)
        n_sc[h] = g["decay"] * nvec + jnp.sum(kw, axis=0, keepdims=True)
        m_sc[h] = g["m_last"]

    ahead = first_matmuls(0)
    for h in heads:
        cur = ahead
        if h + 1 < H:
            ahead = first_matmuls(h + 1)
        rest(h, *cur)

    @pl.when(c == nc - 1)
    def _():
        c_out[...] = c_sc[...]
        n_out[...] = n_sc[...]
        for h in heads:
            m_out[h] = jnp.broadcast_to(m_sc[h], m_out.shape[1:])


def _mlstm_prompt(h, gc, gr, ng, bsz, t, d):
    da = d // A_HEADS
    L = t if t <= A_CHUNK else math.gcd(t, A_CHUNK)
    nc = t // L
    H = A_HEADS
    blk = lambda g: pl.BlockSpec((L, d), lambda b, c: (b * nc + c, g))
    return pl.pallas_call(
        functools.partial(_mlstm_kernel, scale=da ** -0.5),
        grid=(bsz, nc),
        in_specs=[blk(0), blk(1), blk(2), blk(3), blk(4),
                  pl.BlockSpec((L, LANES), lambda b, c: (b * nc + c, 0)),
                  pl.BlockSpec((GATE_ROWS, L), lambda b, c: (0, b * nc + c)),
                  pl.BlockSpec((1, d), lambda b, c: (0, 0))],
        out_specs=[pl.BlockSpec((L, d), lambda b, c: (b * nc + c, 0)),
                   pl.BlockSpec((None, H, da, da), lambda b, c: (b, 0, 0, 0)),
                   pl.BlockSpec((None, H, 1, da), lambda b, c: (b, 0, 0, 0)),
                   pl.BlockSpec((None, H, 1, LANES), lambda b, c: (b, 0, 0, 0))],
        out_shape=[jax.ShapeDtypeStruct((bsz * t, d), BF16),
                   jax.ShapeDtypeStruct((bsz, H, da, da), F32),
                   jax.ShapeDtypeStruct((bsz, H, 1, da), F32),
                   jax.ShapeDtypeStruct((bsz, H, 1, LANES), F32)],
        scratch_shapes=[pltpu.VMEM((H, da, da), F32), pltpu.VMEM((H, 1, da), F32), pltpu.VMEM((H, 1, 1), F32)],
        compiler_params=_cparams("parallel", "arbitrary"),
        name="mlstm_prompt",
    )(h, h, h, h, h, gc, gr, ng)


def _moba_kernel(qt_ref, k_ref, vt_ref, km_ref, z_ref, o_ref, mem_sc, acc_sc):
    i = pl.program_id(2)
    blk = qt_ref.shape[1]
    nb = km_ref.shape[0]
    hd = B_HEAD_DIM
    heads = range(qt_ref.shape[0] // hd)
    hsl = [slice(h * hd, (h + 1) * hd) for h in heads]
    qts = [qt_ref[sl, :] for sl in hsl]

    row = lax.broadcasted_iota(jnp.int32, (nb, blk), 0)
    past = row < i
    key_i = lax.broadcasted_iota(jnp.int32, (blk, blk), 0)
    qry_i = lax.broadcasted_iota(jnp.int32, (blk, blk), 1)
    own = pl.ds(pl.multiple_of(i * blk, blk), blk)
    gates = [jnp.dot(km_ref[:, hsl[h]].astype(BF16), qts[h], preferred_element_type=F32) for h in heads]
    s_own = [jnp.dot(k_ref[own, hsl[h]], qts[h], preferred_element_type=F32) for h in heads]
    for h in heads:
        g = jnp.where(past, gates[h], -jnp.inf)
        rank = jnp.zeros(g.shape, F32)
        for n2 in range(nb):
            g2 = g[n2:n2 + 1, :]
            beats = (g2 > g) | ((g2 == g) & (n2 < row))
            rank = rank + jnp.where(beats, 1.0, 0.0)
        mem_sc[h] = jnp.where(past & (rank < MOBA_TOPK), 1.0, 0.0)
    m0, l0, p_own = [], [], []
    for h in heads:
        s = jnp.where(key_i <= qry_i, s_own[h], -jnp.inf)
        m = jnp.max(s, axis=0, keepdims=True)
        p = jnp.exp2(s - m)
        m0.append(m)
        l0.append(jnp.sum(p, axis=0, keepdims=True))
        p_own.append(p.astype(BF16))
    for h in heads:
        acc_sc[h] = jnp.dot(vt_ref[i, hsl[h], :], p_own[h], preferred_element_type=F32)

    def body(jj, carry):
        ms, ls = carry
        j0 = 2 * jj
        rows0 = pl.ds(pl.multiple_of(j0 * blk, blk), blk)
        rows1 = pl.ds(pl.multiple_of(j0 * blk + blk, blk), blk)
        new_m, new_l = [], []
        ss = []
        for h in heads:
            s0 = jnp.dot(k_ref[rows0, hsl[h]], qts[h], preferred_element_type=F32)
            s1 = jnp.dot(k_ref[rows1, hsl[h]], qts[h], preferred_element_type=F32)
            s0 = jnp.where(mem_sc[h, pl.ds(j0, 1), :] > 0.0, s0, -jnp.inf)
            s1 = jnp.where(mem_sc[h, pl.ds(j0 + 1, 1), :] > 0.0, s1, -jnp.inf)
            ss.append((s0, s1))
        ps = []
        for h in heads:
            s0, s1 = ss[h]
            m_new = jnp.maximum(ms[h], jnp.maximum(jnp.max(s0, axis=0, keepdims=True),
                                                   jnp.max(s1, axis=0, keepdims=True)))
            alpha = jnp.exp2(ms[h] - m_new)
            p0 = jnp.exp2(s0 - m_new)
            p1 = jnp.exp2(s1 - m_new)
            new_l.append(alpha * ls[h] + jnp.sum(p0, axis=0, keepdims=True) + jnp.sum(p1, axis=0, keepdims=True))
            new_m.append(m_new)
            ps.append((alpha, p0.astype(BF16), p1.astype(BF16)))
        for h in heads:
            alpha, p0, p1 = ps[h]
            acc_sc[h] = (alpha * acc_sc[h]
                         + jnp.dot(vt_ref[j0, hsl[h], :], p0, preferred_element_type=F32)
                         + jnp.dot(vt_ref[j0 + 1, hsl[h], :], p1, preferred_element_type=F32))
        return tuple(new_m), tuple(new_l)

    _, l_fin = lax.fori_loop(0, (i + 1) // 2, body, (tuple(m0), tuple(l0)))
    outs = [(acc_sc[h] / l_fin[h]).T for h in heads]
    for h in heads:
        o_ref[:, hsl[h]] = (outs[h] * _silu(z_ref[:, hsl[h]])).astype(o_ref.dtype)


def _moba_prompt(qt, kb, vt, kmean, h, col_z, bsz, t, d):
    nh = d // B_HEAD_DIM
    nb = t // MOBA_BLOCK
    blk = MOBA_BLOCK
    hps = MOBA_HEADS_PER_STEP if nh % MOBA_HEADS_PER_STEP == 0 else 1
    w = hps * B_HEAD_DIM
    ng = nh // hps
    return pl.pallas_call(
        _moba_kernel,
        grid=(bsz, ng, nb),
        in_specs=[pl.BlockSpec((None, w, blk), lambda b, g, i: (b * nb + i, g, 0)),
                  pl.BlockSpec((t, w), lambda b, g, i: (b, g)),
                  pl.BlockSpec((nb, w, blk), lambda b, g, i: (b, g, 0)),
                  pl.BlockSpec((None, nb, w), lambda b, g, i: (b, 0, g)),
                  pl.BlockSpec((blk, w), lambda b, g, i: (b * nb + i, col_z * ng + g))],
        out_specs=pl.BlockSpec((blk, w), lambda b, g, i: (b * nb + i, g)),
        out_shape=jax.ShapeDtypeStruct((bsz * t, d), BF16),
        scratch_shapes=[pltpu.VMEM((hps, nb, blk), F32), pltpu.VMEM((hps, B_HEAD_DIM, blk), F32)],
        compiler_params=_cparams("parallel", "parallel", "arbitrary"),
        name="moba_prompt",
    )(qt, kb, vt, kmean, h)


def _merge_kernel(ya_ref, yb_ref, ga_ref, gb_ref, wa_ref, wb_ref, o_ref):
    pa = jnp.dot(ya_ref[...].astype(BF16), wa_ref[...], preferred_element_type=F32)
    pb = jnp.dot(yb_ref[...].astype(BF16), wb_ref[...], preferred_element_type=F32)
    o_ref[...] = (jax.nn.sigmoid(ga_ref[...]) * pa + jax.nn.sigmoid(gb_ref[...]) * pb).astype(o_ref.dtype)


def _merge(ya, yb, h, col_ga, col_gb, wa, wb, tm, tn):
    m, d = ya.shape
    nj = d // tn
    return pl.pallas_call(
        _merge_kernel,
        grid=(m // tm, nj),
        in_specs=[pl.BlockSpec((tm, d), lambda i, j: (i, 0)),
                  pl.BlockSpec((tm, d), lambda i, j: (i, 0)),
                  pl.BlockSpec((tm, tn), lambda i, j: (i, col_ga * nj + j)),
                  pl.BlockSpec((tm, tn), lambda i, j: (i, col_gb * nj + j)),
                  pl.BlockSpec((d, tn), lambda i, j: (0, j)),
                  pl.BlockSpec((d, tn), lambda i, j: (0, j))],
        out_specs=pl.BlockSpec((tm, tn), lambda i, j: (i, j)),
        out_shape=jax.ShapeDtypeStruct((m, d), BF16),
        compiler_params=_cparams("parallel", "arbitrary"),
        name="merge",
    )(ya, yb, h, h, wa, wb)


def _out_kernel(mg_ref, wo_ref, x_ref, g_ref, b_ref, y_ref, *, alpha):
    tm = mg_ref.shape[0]
    strip = tm // 4 if tm % 32 == 0 else tm
    rows = [slice(r, r + strip) for r in range(0, tm, strip)]
    outs = [jnp.dot(mg_ref[rs, :], wo_ref[...], preferred_element_type=F32) for rs in rows]
    for rs, out in zip(rows, outs):
        y_ref[rs, :] = _layernorm(alpha * x_ref[rs, :] + out) * g_ref[...] + b_ref[...]


def _out(mg, wo, x, g, b, tm, alpha):
    m, d = x.shape
    row = pl.BlockSpec((tm, d), lambda i: (i, 0))
    vec = pl.BlockSpec((1, d), lambda i: (0, 0))
    return pl.pallas_call(
        functools.partial(_out_kernel, alpha=alpha),
        grid=(m // tm,),
        in_specs=[row, pl.BlockSpec((d, d), lambda i: (0, 0)), row, vec, vec],
        out_specs=row,
        out_shape=jax.ShapeDtypeStruct((m, d), F32),
        compiler_params=_cparams("parallel"),
        name="outproj",
    )(mg, wo, x, g, b)


def _mlstm_step_kernel(q_ref, k_ref, v_ref, oa_ref, za_ref, gc_ref, ng_ref, c_ref, n_ref, m_ref,
                       ya_ref, c_out, n_out, m_out, *, scale):
    H = A_HEADS
    da = q_ref.shape[1] // H
    gc = gc_ref[...]
    eye = lax.broadcasted_iota(jnp.int32, (da, da), 0) == lax.broadcasted_iota(jnp.int32, (da, da), 1)
    for h in range(H):
        sl = slice(h * da, (h + 1) * da)
        ig = gc[:, h:h + 1]
        lf = _log_sigmoid(gc[:, h + H:h + H + 1])
        m_prev = m_ref[h][:, :1]
        a = ig - lf
        g = jnp.maximum(m_prev, a)
        m_t = lf + g
        inter = jnp.exp(m_prev - g)
        dm = jnp.exp(a - g)

        q = q_ref[:, sl]
        ks = k_ref[:, sl] * scale
        v = v_ref[:, sl]
        cmat = c_ref[h]
        nvec = n_ref[h]
        q8 = jnp.broadcast_to(q, (8, da)).astype(BF16)
        cq = lax.dot_general(q8, cmat.astype(BF16), _NT, preferred_element_type=F32)[0:1]
        s = jnp.sum(q * ks, axis=1, keepdims=True) * dm
        num = inter * cq + s * v
        den = inter * jnp.sum(nvec * q, axis=1, keepdims=True) + s
        hh = num / jnp.maximum(jnp.abs(den), jnp.exp(-m_t))

        v_col = jnp.sum(jnp.where(eye, jnp.broadcast_to(v, (da, da)), 0.0), axis=1, keepdims=True)
        c_out[h] = inter * cmat + (dm * v_col) * ks
        n_out[h] = inter * nvec + dm * ks
        m_out[h] = jnp.broadcast_to(m_t, m_out.shape[1:])
        ya = _layernorm(hh) * ng_ref[:, sl] * jax.nn.sigmoid(oa_ref[:, sl]) * _silu(za_ref[:, sl])
        ya_ref[:, sl] = ya.astype(ya_ref.dtype)


def _mlstm_step(h3, gc3, ng, c_all, c_row0, n0, m0, d):
    bs = h3.shape[0]
    H = A_HEADS
    da = d // H
    row = lambda g: pl.BlockSpec((None, 1, d), lambda b: (b, 0, g))
    st = lambda r, c: pl.BlockSpec((None, H, r, c), lambda b: (b, 0, 0, 0))
    return pl.pallas_call(
        functools.partial(_mlstm_step_kernel, scale=da ** -0.5),
        grid=(bs,),
        in_specs=[row(0), row(1), row(2), row(3), row(4),
                  pl.BlockSpec((None, 1, LANES), lambda b: (b, 0, 0)),
                  pl.BlockSpec((1, d), lambda b: (0, 0)),
                  pl.BlockSpec((None, H, da, da), lambda b: (c_row0 + b, 0, 0, 0)),
                  st(1, da), st(1, LANES)],
        out_specs=[pl.BlockSpec((None, 1, d), lambda b: (b, 0, 0)),
                   st(da, da), st(1, da), st(1, LANES)],
        out_shape=[jax.ShapeDtypeStruct((bs, 1, d), F32),
                   jax.ShapeDtypeStruct((bs, H, da, da), F32),
                   jax.ShapeDtypeStruct((bs, H, 1, da), F32),
                   jax.ShapeDtypeStruct((bs, H, 1, LANES), F32)],
        compiler_params=_cparams("parallel"),
        name="mlstm_step",
    )(h3, h3, h3, h3, h3, gc3, ng, c_all, n0, m0)


def _topk_kernel(q_ref, km_ref, o_ref):
    q = q_ref[...]
    nb = km_ref.shape[0]
    lane = lax.broadcasted_iota(jnp.int32, o_ref.shape, 1)
    gate = jnp.full(o_ref.shape, -jnp.inf, F32)
    for n in range(nb):
        gate = jnp.where(lane == n, jnp.sum(km_ref[n] * q, axis=1, keepdims=True), gate)
    rank = jnp.zeros(o_ref.shape, F32)
    for n2 in range(nb):
        g2 = gate[:, n2:n2 + 1]
        beats = (g2 > gate) | ((g2 == gate) & (n2 < lane))
        rank = rank + jnp.where(beats, 1.0, 0.0)
    lane_f = lane.astype(F32)
    out = jnp.zeros(o_ref.shape, F32)
    for r in range(MOBA_TOPK):
        idx = jnp.sum(jnp.where((rank == r) & (lane < nb), lane_f, 0.0), axis=1, keepdims=True)
        out = jnp.where(lane == r, idx, out)
    o_ref[...] = out.astype(jnp.int32)


def _topk_blocks(q4, kmean):
    bs, nb, nh, hd = kmean.shape
    assert nb <= LANES
    return pl.pallas_call(
        _topk_kernel,
        grid=(bs,),
        in_specs=[pl.BlockSpec((None, nh, hd), lambda b: (b, 0, 0)),
                  pl.BlockSpec((None, nb, nh, hd), lambda b: (b, 0, 0, 0))],
        out_specs=pl.BlockSpec((None, nh, LANES), lambda b: (b, 0, 0)),
        out_shape=jax.ShapeDtypeStruct((bs, nh, LANES), jnp.int32),
        compiler_params=_cparams("parallel"),
        name="topk_blocks",
    )(q4, kmean)


def _decode_attn_kernel(pt_ref, ix_ref, q_ref, kn_ref, vn_ref, z_ref, ck_hbm, cv_hbm, o_ref,
                        kbuf, vbuf, sems, *, per_blk, scale):
    b = pl.program_id(0)
    nbat = pl.num_programs(0)
    nh = q_ref.shape[0]
    n_pages = MOBA_TOPK * per_blk

    def page_copies(bb, slot, h, p):
        r, half = divmod(p, per_blk)
        pid = pt_ref[bb, ix_ref[bb, h * MOBA_TOPK + r] * per_blk + half]
        return (pltpu.make_async_copy(ck_hbm.at[pid, :, h, :], kbuf.at[slot, h, p], sems.at[0, slot]),
                pltpu.make_async_copy(cv_hbm.at[pid, :, h, :], vbuf.at[slot, h, p], sems.at[1, slot]))

    def for_each_copy(bb, slot, fn):
        def per_head(h, carry):
            for p in range(n_pages):
                for cp in page_copies(bb, slot, h, p):
                    fn(cp)
            return carry
        lax.fori_loop(0, nh, per_head, 0)

    @pl.when(b == 0)
    def _():
        for_each_copy(0, 0, lambda cp: cp.start())

    @pl.when(b + 1 < nbat)
    def _():
        for_each_copy(b + 1, (b + 1) % 2, lambda cp: cp.start())

    slot = b % 2
    for_each_copy(b, slot, lambda cp: cp.wait())

    def per_head(h, carry):
        row = pl.ds(h, 1)
        q = q_ref[row, :]
        scores = [jnp.sum(kbuf[slot, h, p] * q, axis=1, keepdims=True) * scale for p in range(n_pages)]
        s_own = jnp.sum(kn_ref[row, :] * q, axis=1, keepdims=True) * scale
        m = s_own
        for sc in scores:
            m = jnp.maximum(m, jnp.max(sc, axis=0, keepdims=True))
        e_own = jnp.exp(s_own - m)
        l = e_own
        acc = e_own * vn_ref[row, :]
        for p, sc in enumerate(scores):
            e = jnp.exp(sc - m)
            l = l + jnp.sum(e, axis=0, keepdims=True)
            acc = acc + jnp.sum(e * vbuf[slot, h, p], axis=0, keepdims=True)
        o_ref[row, :] = acc / l * _silu(z_ref[row, :])
        return carry

    lax.fori_loop(0, nh, per_head, 0, unroll=4 if nh % 4 == 0 else 1)


def _decode_attn(cache_k4, cache_v4, page_ids, idx, q4, kn4, vn4, z4):
    _, page, nh, hd = cache_k4.shape
    bs = page_ids.shape[0]
    per_blk = MOBA_BLOCK // page
    n_pages = MOBA_TOPK * per_blk
    vec = pl.BlockSpec((None, nh, hd), lambda b, pt, ix: (b, 0, 0))
    hbm = pl.BlockSpec(memory_space=pl.ANY)
    return pl.pallas_call(
        functools.partial(_decode_attn_kernel, per_blk=per_blk, scale=hd ** -0.5),
        grid_spec=pltpu.PrefetchScalarGridSpec(
            num_scalar_prefetch=2,
            grid=(bs,),
            in_specs=[vec, vec, vec, vec, hbm, hbm],
            out_specs=vec,
            scratch_shapes=[pltpu.VMEM((2, nh, n_pages, page, hd), F32),
                            pltpu.VMEM((2, nh, n_pages, page, hd), F32),
                            pltpu.SemaphoreType.DMA((2, 2))]),
        out_shape=jax.ShapeDtypeStruct((bs, nh, hd), F32),
        compiler_params=_cparams("arbitrary"),
        name="decode_attn",
    )(page_ids, idx, q4, kn4, vn4, z4, cache_k4, cache_v4)


def _rope_tables(pos):
    half = B_HEAD_DIM // 2
    freqs = ROPE_THETA ** (-jnp.arange(half, dtype=F32) / half)
    ang = pos.astype(F32)[:, None] * freqs[None, :]
    cos = jnp.cos(ang)
    sin = jnp.sin(ang)
    return jnp.concatenate([cos, cos], -1), jnp.concatenate([-sin, sin], -1)


def _tile(n, pref):
    return pref if n % pref == 0 else n


def _prep_weights(w_in_all, layer, b_in, w_proj_a, w_proj_b, w_out, d):
    c0 = 5 * d
    c1 = c0 + 2 * A_HEADS
    w_in = w_in_all[layer]
    w_a = w_in.astype(BF16)
    w_b = w_a[:, c1:]
    w_g = w_in[:, c0:c1]
    wc = jnp.pad(w_g, ((0, 0), (0, LANES - 2 * A_HEADS))).astype(BF16)
    wr = w_g.T.astype(BF16)
    bc = jnp.pad(b_in[c0:c1], (0, LANES - 2 * A_HEADS))[None, :]
    br = b_in[c0:c1][:, None]
    return ((w_a, b_in[None, :c0]), (w_b, b_in[None, c1:]),
            wc, wr, bc, br, w_proj_a.astype(BF16), w_proj_b.astype(BF16), w_out.astype(BF16))


_QA, _KA, _VA, _OA, _ZA = range(5)
_QB, _KB, _VB, _ZB, _GA, _GB = range(6)


def _project(xb, part, tm):
    w, b = part
    return _inproj(xb, w, b, tm, _tile(b.shape[1], 1024))


def _project_with_means(xb, parts, tm, cache_k4, page_ids):
    _, page, nh, hd = cache_k4.shape
    bs, n_pages = page_ids.shape
    n_blocks = bs * n_pages * page // MOBA_BLOCK
    tiles = [(tm, _tile(b.shape[1], 2048)) for _, b in parts]
    steps = [(xb.shape[0] // tm_) * (b.shape[1] // tn_) for (tm_, tn_), (_, b) in zip(tiles, parts)]
    bps = -(-n_blocks // sum(steps))
    hs, means, first = [], [], 0
    for (w, b), (tm_, tn_), n_steps in zip(parts, tiles, steps):
        h, km = _inproj_means(xb, w, b, tm_, tn_, page_ids, cache_k4, first, bps)
        hs.append(h)
        means.append(km)
        first += n_steps * bps
    kmean = jnp.concatenate(means)[:n_blocks].reshape(bs, n_blocks // bs, nh, hd)
    return hs[0], hs[1], kmean


def _layer_prompt(x, wts, ng, ln_g, ln_b, alpha, cache_k4, page_ids):
    bsz, t, d = x.shape
    m = bsz * t
    part_a, part_b, wc, wr, bc, br, wa, wb, wo = wts
    x2 = x.reshape(m, d)
    xb, gc, gr = _gates(x2, wc, wr, bc, br, _tile(m, 512))
    h_a, h_b, kmean_cache = _project_with_means(xb, (part_a, part_b), _tile(m, 512), cache_k4, page_ids)
    ya, c1, n1, m1 = _mlstm_prompt(h_a, gc, gr, ng, bsz, t, d)
    cos, sin = _rope_tables(jnp.arange(t, dtype=jnp.int32))
    cos = jnp.tile(cos, (bsz, 1))
    sin = jnp.tile(sin, (bsz, 1))
    q_t, k_new, k_b, v_new, v_t, kmean = _rope(h_b, _QB, _KB, _VB, d, cos, sin, B_HEAD_DIM ** -0.5 * LOG2E)
    nb = t // MOBA_BLOCK
    yb = _moba_prompt(q_t, k_b, v_t, kmean.reshape(bsz, nb, d), h_b, _ZB, bsz, t, d)
    mg = _merge(ya, yb, h_b, _GA, _GB, wa, wb, _tile(m, 1024), _tile(d, 512))
    y = _out(mg, wo, x2, ln_g, ln_b, _tile(m, 512), alpha)
    nh = d // B_HEAD_DIM
    return (kmean_cache, y.reshape(bsz, t, d), k_new.reshape(bsz, t, nh, B_HEAD_DIM),
            v_new.reshape(bsz, t, nh, B_HEAD_DIM), c1, n1.reshape(bsz, A_HEADS, d // A_HEADS), m1[:, :, 0, 0])


def _layer_sample(x, wts, ng, ln_g, ln_b, alpha, cache_k4, cache_v4, page_ids, kmean, c_all, c_row0, n0, m0):
    bs, t, d = x.shape
    part_a, part_b, wc, wr, bc, br, wa, wb, wo = wts
    _, page, nh, hd = cache_k4.shape
    past = page_ids.shape[1] * page
    da = d // A_HEADS
    x2 = x.reshape(bs, d)
    xb, gc, _ = _gates(x2, wc, wr, bc, br, bs)
    h_a = _project(xb, part_a, bs)
    h_b = _project(xb, part_b, bs)
    ya, c1, n1, m1 = _mlstm_step(h_a.reshape(bs, 1, h_a.shape[1]), gc.reshape(bs, 1, LANES), ng, c_all, c_row0,
                                 n0.reshape(bs, A_HEADS, 1, da),
                                 jnp.broadcast_to(m0[:, :, None, None], (bs, A_HEADS, 1, LANES)), d)
    cos, sin = _rope_tables(jnp.full((bs,), past, dtype=jnp.int32))
    q_r, k_new, v_new = _rope_rows(h_b, _QB, _KB, _VB, d, cos, sin)
    heads = lambda a: a.reshape(bs, nh, hd)
    q4, kn4, vn4 = heads(q_r), heads(k_new), heads(v_new)
    idx = _topk_blocks(q4, kmean)[:, :, :MOBA_TOPK].reshape(bs, nh * MOBA_TOPK)
    yb = _decode_attn(cache_k4, cache_v4, page_ids, idx, q4, kn4, vn4, heads(h_b[:, _ZB * d:(_ZB + 1) * d]))
    mg = _merge(ya.reshape(bs, d), yb.reshape(bs, d), h_b, _GA, _GB, wa, wb, bs, _tile(d, 512))
    y = _out(mg, wo, x2, ln_g, ln_b, bs, alpha)
    return (y.reshape(bs, 1, d), kn4[:, None], vn4[:, None],
            c1, n1.reshape(bs, A_HEADS, da), m1[:, :, 0, 0])


def kernel(x_prompt, x_sample, cache_k, cache_v, page_table, state_mlstm_C, state_mlstm_n, state_mlstm_m,
           w_in, b_in, mh_norm_g, w_proj_a, w_proj_b, w_out, ln_g, ln_b):
    depth = w_in.shape[0]
    d = x_prompt.shape[-1]
    n_pages = page_table.shape[1]
    page = cache_k.shape[2]
    assert x_sample.shape[1] == 1 and d % (A_HEADS * LANES) == 0
    assert x_prompt.shape[1] % MOBA_BLOCK == 0 and MOBA_BLOCK % page == 0
    assert (n_pages * page) % MOBA_BLOCK == 0 and n_pages * page // MOBA_BLOCK >= MOBA_TOPK
    alpha = (2 * depth) ** 0.25
    n_phys = cache_k.shape[1]
    bs = x_sample.shape[0]
    cache_k3 = cache_k.reshape((depth * n_phys,) + cache_k.shape[2:])
    cache_v3 = cache_v.reshape((depth * n_phys,) + cache_v.shape[2:])
    c_all = state_mlstm_C.reshape((depth * bs,) + state_mlstm_C.shape[2:])
    yp, ys = x_prompt, x_sample
    new_p = [[] for _ in range(5)]
    new_s = [[] for _ in range(5)]
    for l in range(depth):
        wts = _prep_weights(w_in, l, b_in[l], w_proj_a[l], w_proj_b[l], w_out[l], d)
        ng, lg, lb = mh_norm_g[l][None, :], ln_g[l][None, :], ln_b[l][None, :]
        page_ids = page_table + l * n_phys
        kmean, yp, *st_p = _layer_prompt(yp, wts, ng, lg, lb, alpha, cache_k3, page_ids)
        ys, *st_s = _layer_sample(ys, wts, ng, lg, lb, alpha, cache_k3, cache_v3, page_ids, kmean,
                                  c_all, l * bs, state_mlstm_n[l], state_mlstm_m[l])
        for i in range(5):
            new_p[i].append(st_p[i])
            new_s[i].append(st_s[i])
    stack = lambda ts: ts[0][None] if len(ts) == 1 else jnp.stack(ts)
    k_p, v_p, c_p, n_p, m_p = [stack(t) for t in new_p]
    k_s, v_s, c_s, n_s, m_s = [stack(t) for t in new_s]
    return (yp, ys, k_p, v_p, c_p, n_p, m_p, k_s, v_s, c_s, n_s, m_s)
```

```python
import functools
import math

import jax
import jax.numpy as jnp
from jax import lax
from jax.experimental import pallas as pl
from jax.experimental.pallas import tpu as pltpu

F32 = jnp.float32
BF16 = jnp.bfloat16

A_HEADS = 4
A_CHUNK = 128
B_HEAD_DIM = 128
MOBA_BLOCK = 256
MOBA_TOPK = 3
ROPE_THETA = 10000.0
LN_EPS = 1e-5
LOG2E = math.log2(math.e)
LANES = 128
GATE_ROWS = 8
MOBA_HEADS_PER_STEP = 8
MEANS_PER_MOBA_STEP = 4
MEANS_PER_MLSTM_STEP = 4
VMEM_LIMIT = 60 * 1024 * 1024

_NT = (((1,), (1,)), ((), ()))


def _cparams(*sem):
    return pltpu.CompilerParams(dimension_semantics=sem, vmem_limit_bytes=VMEM_LIMIT)


def _log_sigmoid(x):
    return jnp.minimum(x, 0.0) - jnp.log1p(jnp.exp(-jnp.abs(x)))


def _silu(x):
    return x * jax.nn.sigmoid(x)


def _layernorm(x):
    mu = jnp.mean(x, axis=-1, keepdims=True)
    xc = x - mu
    return xc * lax.rsqrt(jnp.mean(xc * xc, axis=-1, keepdims=True) + LN_EPS)


def _inproj_kernel(x_ref, w_ref, b_ref, o_ref):
    o_ref[...] = jnp.dot(x_ref[...], w_ref[...], preferred_element_type=F32) + b_ref[...]


def _inproj(xb, w, b, tm, tn):
    m, k = xb.shape
    n = b.shape[1]
    return pl.pallas_call(
        _inproj_kernel,
        grid=(n // tn, m // tm),
        in_specs=[pl.BlockSpec((tm, k), lambda j, i: (i, 0)),
                  pl.BlockSpec((k, tn), lambda j, i: (0, j)),
                  pl.BlockSpec((1, tn), lambda j, i: (0, j))],
        out_specs=pl.BlockSpec((tm, tn), lambda j, i: (i, j)),
        out_shape=jax.ShapeDtypeStruct((m, n), F32),
        compiler_params=_cparams("parallel", "parallel"),
        name="inproj",
    )(xb, w, b)


def _means_fetch(pt_ref, ck_hbm, pbuf, sems, step, n_steps, job):
    first_block, n_blocks, per_blk, n_pages = job
    bps = pbuf.shape[1] // per_blk

    def page_copy(s, slot, r, p):
        blk = jnp.minimum(first_block + s * bps + r, n_blocks - 1)
        g = blk * per_blk + p
        pid = pt_ref[g // n_pages, g % n_pages]
        return pltpu.make_async_copy(ck_hbm.at[pid], pbuf.at[slot, r * per_blk + p], sems.at[slot])

    def for_each_copy(s, slot, fn):
        for r in range(bps):
            for p in range(per_blk):
                fn(page_copy(s, slot, r, p))

    @pl.when(step == 0)
    def _():
        for_each_copy(0, 0, lambda cp: cp.start())

    @pl.when(step + 1 < n_steps)
    def _():
        for_each_copy(step + 1, (step + 1) % 2, lambda cp: cp.start())

    slot = step % 2
    for_each_copy(step, slot, lambda cp: cp.wait())
    return slot


def _means_reduce(pbuf, slot, km_ref, job):
    per_blk = job[2]
    rows = per_blk * pbuf.shape[2]
    for r in range(km_ref.shape[0]):
        tot = jnp.sum(pbuf[slot, r * per_blk], axis=0)
        for p in range(1, per_blk):
            tot = tot + jnp.sum(pbuf[slot, r * per_blk + p], axis=0)
        km_ref[r] = tot * (1.0 / rows)


def _means_job(page_ids, cache4, first_block):
    bs, n_pages = page_ids.shape
    per_blk = MOBA_BLOCK // cache4.shape[1]
    return (first_block, bs * n_pages // per_blk, per_blk, n_pages)


def _means_scratch(cache4, job, bps):
    _, page, nh, hd = cache4.shape
    return [pltpu.VMEM((2, bps * job[2], page, nh, hd), F32), pltpu.SemaphoreType.DMA((2,))]


def _inproj_means_kernel(pt_ref, x_ref, w_ref, b_ref, ck_hbm, o_ref, km_ref, pbuf, sems, *, job):
    ni = pl.num_programs(1)
    step = pl.program_id(0) * ni + pl.program_id(1)
    slot = _means_fetch(pt_ref, ck_hbm, pbuf, sems, step, pl.num_programs(0) * ni, job)
    o_ref[...] = jnp.dot(x_ref[...], w_ref[...], preferred_element_type=F32) + b_ref[...]
    _means_reduce(pbuf, slot, km_ref, job)


def _inproj_means(xb, w, b, tm, tn, page_ids, cache4, first_block, blocks_per_step):
    m, k = xb.shape
    n = b.shape[1]
    _, page, nh, hd = cache4.shape
    nj, ni = n // tn, m // tm
    bps = blocks_per_step
    job = _means_job(page_ids, cache4, first_block)
    return pl.pallas_call(
        functools.partial(_inproj_means_kernel, job=job),
        grid_spec=pltpu.PrefetchScalarGridSpec(
            num_scalar_prefetch=1,
            grid=(nj, ni),
            in_specs=[pl.BlockSpec((tm, k), lambda j, i, pt: (i, 0)),
                      pl.BlockSpec((k, tn), lambda j, i, pt: (0, j)),
                      pl.BlockSpec((1, tn), lambda j, i, pt: (0, j)),
                      pl.BlockSpec(memory_space=pl.ANY)],
            out_specs=[pl.BlockSpec((tm, tn), lambda j, i, pt: (i, j)),
                       pl.BlockSpec((bps, nh, hd), lambda j, i, pt: (j * ni + i, 0, 0))],
            scratch_shapes=_means_scratch(cache4, job, bps)),
        out_shape=[jax.ShapeDtypeStruct((m, n), F32),
                   jax.ShapeDtypeStruct((nj * ni * bps, nh, hd), F32)],
        compiler_params=_cparams("arbitrary", "arbitrary"),
        name="inproj_means",
    )(page_ids, xb, w, b, cache4)


def _gates_kernel(x_ref, wc_ref, wr_ref, bc_ref, br_ref, xb_ref, gc_ref, gr_ref):
    xb = x_ref[...].astype(BF16)
    xb_ref[...] = xb
    gc_ref[...] = jnp.dot(xb, wc_ref[...], preferred_element_type=F32) + bc_ref[...]
    gr_ref[...] = lax.dot_general(wr_ref[...], xb, _NT, preferred_element_type=F32) + br_ref[...]


def _gates(x, wc, wr, bc, br, tm):
    m, k = x.shape
    return pl.pallas_call(
        _gates_kernel,
        grid=(m // tm,),
        in_specs=[pl.BlockSpec((tm, k), lambda i: (i, 0)),
                  pl.BlockSpec((k, LANES), lambda i: (0, 0)),
                  pl.BlockSpec((GATE_ROWS, k), lambda i: (0, 0)),
                  pl.BlockSpec((1, LANES), lambda i: (0, 0)),
                  pl.BlockSpec((GATE_ROWS, 1), lambda i: (0, 0))],
        out_specs=[pl.BlockSpec((tm, k), lambda i: (i, 0)),
                   pl.BlockSpec((tm, LANES), lambda i: (i, 0)),
                   pl.BlockSpec((GATE_ROWS, tm), lambda i: (0, i))],
        out_shape=[jax.ShapeDtypeStruct((m, k), BF16),
                   jax.ShapeDtypeStruct((m, LANES), F32),
                   jax.ShapeDtypeStruct((GATE_ROWS, m), F32)],
        compiler_params=_cparams("parallel"),
        name="gates",
    )(x, wc, wr, bc, br)


def _rotate(x, cos, sin):
    return x * cos + pltpu.roll(x, B_HEAD_DIM // 2, 1) * sin


def _rope_kernel(q_ref, k_ref, v_ref, cos_ref, sin_ref, qt_ref, ko_ref, kb_ref, vo_ref, vt_ref, km_ref, *, scale):
    cos = cos_ref[...]
    sin = sin_ref[...]
    tm, d = q_ref.shape
    for h in range(d // B_HEAD_DIM):
        sl = slice(h * B_HEAD_DIM, (h + 1) * B_HEAD_DIM)
        qr = _rotate(q_ref[:, sl], cos, sin)
        kr = _rotate(k_ref[:, sl], cos, sin)
        v = v_ref[:, sl]
        qt_ref[sl, :] = (qr * scale).T.astype(BF16)
        ko_ref[:, sl] = kr
        kb_ref[:, sl] = kr.astype(BF16)
        km_ref[:, sl] = jnp.sum(kr, axis=0, keepdims=True) * (1.0 / tm)
        vo_ref[:, sl] = v
        vt_ref[sl, :] = v.T.astype(BF16)


def _rope(h, col_q, col_k, col_v, d, cos, sin, scale):
    m = h.shape[0]
    tm = MOBA_BLOCK
    nt = m // tm
    row = lambda c: pl.BlockSpec((tm, d), lambda i: (i, c))
    full = pl.BlockSpec((tm, d), lambda i: (i, 0))
    tab = pl.BlockSpec((tm, B_HEAD_DIM), lambda i: (i, 0))
    tr = pl.BlockSpec((None, d, tm), lambda i: (i, 0, 0))
    return pl.pallas_call(
        functools.partial(_rope_kernel, scale=scale),
        grid=(nt,),
        in_specs=[row(col_q), row(col_k), row(col_v), tab, tab],
        out_specs=[tr, full, full, full, tr, pl.BlockSpec((None, 1, d), lambda i: (i, 0, 0))],
        out_shape=[jax.ShapeDtypeStruct((nt, d, tm), BF16),
                   jax.ShapeDtypeStruct((m, d), F32),
                   jax.ShapeDtypeStruct((m, d), BF16),
                   jax.ShapeDtypeStruct((m, d), F32),
                   jax.ShapeDtypeStruct((nt, d, tm), BF16),
                   jax.ShapeDtypeStruct((nt, 1, d), F32)],
        compiler_params=_cparams("parallel"),
        name="rope",
    )(h, h, h, cos, sin)


def _rope_rows_kernel(q_ref, k_ref, v_ref, cos_ref, sin_ref, qo_ref, ko_ref, vo_ref):
    cos = cos_ref[...]
    sin = sin_ref[...]
    for h in range(q_ref.shape[1] // B_HEAD_DIM):
        sl = slice(h * B_HEAD_DIM, (h + 1) * B_HEAD_DIM)
        qo_ref[:, sl] = _rotate(q_ref[:, sl], cos, sin)
        ko_ref[:, sl] = _rotate(k_ref[:, sl], cos, sin)
    vo_ref[...] = v_ref[...]


def _rope_rows(h, col_q, col_k, col_v, d, cos, sin):
    m = h.shape[0]
    row = lambda c: pl.BlockSpec((m, d), lambda i: (0, c))
    full = pl.BlockSpec((m, d), lambda i: (0, 0))
    tab = pl.BlockSpec((m, B_HEAD_DIM), lambda i: (0, 0))
    return pl.pallas_call(
        _rope_rows_kernel,
        grid=(1,),
        in_specs=[row(col_q), row(col_k), row(col_v), tab, tab],
        out_specs=[full, full, full],
        out_shape=[jax.ShapeDtypeStruct((m, d), F32)] * 3,
        compiler_params=_cparams("arbitrary"),
        name="rope_rows",
    )(h, h, h, cos, sin)


def _mlstm_kernel(pt_ref, q_ref, k_ref, v_ref, oa_ref, za_ref, gc_ref, gr_ref, ng_ref, ck_hbm,
                  ya_ref, c_out, n_out, m_out, cm_ref, c_sc, n_sc, m_sc, pbuf, sems, *, scale, job):
    c = pl.program_id(1)
    nc = pl.num_programs(1)
    slot = _means_fetch(pt_ref, ck_hbm, pbuf, sems, pl.program_id(0) * nc + c, pl.num_programs(0) * nc, job)
    _means_reduce(pbuf, slot, cm_ref, job)
    L = q_ref.shape[0]
    H = A_HEADS
    D = q_ref.shape[1] // H
    heads = range(H)
    hsl = [slice(h * D, (h + 1) * D) for h in heads]

    @pl.when(c == 0)
    def _():
        c_sc[...] = jnp.zeros(c_sc.shape, F32)
        n_sc[...] = jnp.zeros(n_sc.shape, F32)
        m_sc[...] = jnp.zeros(m_sc.shape, F32)

    gc = gc_ref[...]
    gr = gr_ref[...]
    t_i = lax.broadcasted_iota(jnp.int32, (L, L), 0)
    s_i = lax.broadcasted_iota(jnp.int32, (L, L), 1)
    causal = s_i <= t_i
    def head_gates(h):
        ig_col, fa_col = gc[:, h:h + 1], gc[:, h + H:h + H + 1]
        ig_row, fa_row = gr[h:h + 1, :], gr[h + H:h + H + 1, :]
        lf_col = _log_sigmoid(fa_col)
        lf_row = _log_sigmoid(fa_row)
        b_col = jnp.sum(jnp.where(causal, jnp.broadcast_to(lf_row, (L, L)), 0.0), axis=1, keepdims=True)
        b_row = jnp.sum(jnp.where(t_i <= s_i, jnp.broadcast_to(lf_col, (L, L)), 0.0), axis=0, keepdims=True)
        a_row = ig_row - b_row
        a_col = ig_col - b_col
        cm_col = jnp.max(jnp.where(causal, jnp.broadcast_to(a_row, (L, L)), -jnp.inf), axis=1, keepdims=True)
        m_prev = m_sc[h]
        g_col = jnp.maximum(m_prev, cm_col)
        g_last = jnp.maximum(m_prev, jnp.max(a_row, axis=1, keepdims=True))
        return dict(
            dmat=jnp.exp(jnp.where(causal, a_row - g_col, -jnp.inf)),
            inter=jnp.exp(m_prev - g_col),
            inv_floor=jnp.exp(-(b_col + g_col)),
            m_last=jnp.sum(lf_row, axis=1, keepdims=True) + g_last,
            w_col=jnp.exp(a_col - g_last),
            decay=jnp.exp(m_prev - g_last))

    def first_matmuls(h):
        q = q_ref[:, hsl[h]]
        ks = k_ref[:, hsl[h]] * scale
        qb = q.astype(BF16)
        s_raw = lax.dot_general(qb, ks.astype(BF16), _NT, preferred_element_type=F32)
        cq = lax.dot_general(qb, c_sc[h].astype(BF16), _NT, preferred_element_type=F32)
        return head_gates(h), q, ks, s_raw, cq

    def rest(h, g, q, ks, s_raw, cq):
        v = v_ref[:, hsl[h]]
        s = s_raw * g["dmat"]
        num = g["inter"] * cq + jnp.dot(s.astype(BF16), v.astype(BF16), preferred_element_type=F32)
        nvec = n_sc[h]
        den = g["inter"] * jnp.sum(q * nvec, axis=1, keepdims=True) + jnp.sum(s, axis=1, keepdims=True)
        hh = num / jnp.maximum(jnp.abs(den), g["inv_floor"])
        ya = (_layernorm(hh) * ng_ref[:, hsl[h]] * jax.nn.sigmoid(oa_ref[:, hsl[h]])
              * _silu(za_ref[:, hsl[h]]))
        ya_ref[:, hsl[h]] = ya.astype(ya_ref.dtype)
        kw = ks * g["w_col"]
        c_sc[h] = g["decay"] * c_sc[h] + jnp.dot(v.T.astype(BF16), kw.astype(BF16), preferred_element_type=F32)
        n_sc[h] = g["decay"] * nvec + jnp.sum(kw, axis=0, keepdims=True)
        m_sc[h] = g["m_last"]

    ahead = first_matmuls(0)
    for h in heads:
        cur = ahead
        if h + 1 < H:
            ahead = first_matmuls(h + 1)
        rest(h, *cur)

    @pl.when(c == nc - 1)
    def _():
        c_out[...] = c_sc[...]
        n_out[...] = n_sc[...]
        for h in heads:
            m_out[h] = jnp.broadcast_to(m_sc[h], m_out.shape[1:])


def _mlstm_steps(bsz, t):
    L = t if t <= A_CHUNK else math.gcd(t, A_CHUNK)
    return L, bsz * (t // L)


def _mlstm_prompt(h, gc, gr, ng, bsz, t, d, page_ids, cache4, first_block, blocks_per_step):
    da = d // A_HEADS
    L, n_steps = _mlstm_steps(bsz, t)
    nc = t // L
    H = A_HEADS
    bps = blocks_per_step
    job = _means_job(page_ids, cache4, first_block)
    blk = lambda g: pl.BlockSpec((L, d), lambda b, c, pt: (b * nc + c, g))
    return pl.pallas_call(
        functools.partial(_mlstm_kernel, scale=da ** -0.5, job=job),
        grid_spec=pltpu.PrefetchScalarGridSpec(
            num_scalar_prefetch=1,
            grid=(bsz, nc),
            in_specs=[blk(0), blk(1), blk(2), blk(3), blk(4),
                      pl.BlockSpec((L, LANES), lambda b, c, pt: (b * nc + c, 0)),
                      pl.BlockSpec((GATE_ROWS, L), lambda b, c, pt: (0, b * nc + c)),
                      pl.BlockSpec((1, d), lambda b, c, pt: (0, 0)),
                      pl.BlockSpec(memory_space=pl.ANY)],
            out_specs=[pl.BlockSpec((L, d), lambda b, c, pt: (b * nc + c, 0)),
                       pl.BlockSpec((None, H, da, da), lambda b, c, pt: (b, 0, 0, 0)),
                       pl.BlockSpec((None, H, 1, da), lambda b, c, pt: (b, 0, 0, 0)),
                       pl.BlockSpec((None, H, 1, LANES), lambda b, c, pt: (b, 0, 0, 0)),
                       pl.BlockSpec((bps,) + cache4.shape[2:], lambda b, c, pt: (b * nc + c, 0, 0))],
            scratch_shapes=([pltpu.VMEM((H, da, da), F32), pltpu.VMEM((H, 1, da), F32), pltpu.VMEM((H, 1, 1), F32)]
                            + _means_scratch(cache4, job, bps))),
        out_shape=[jax.ShapeDtypeStruct((bsz * t, d), BF16),
                   jax.ShapeDtypeStruct((bsz, H, da, da), F32),
                   jax.ShapeDtypeStruct((bsz, H, 1, da), F32),
                   jax.ShapeDtypeStruct((bsz, H, 1, LANES), F32),
                   jax.ShapeDtypeStruct((n_steps * bps,) + cache4.shape[2:], F32)],
        compiler_params=_cparams("arbitrary", "arbitrary"),
        name="mlstm_prompt",
    )(page_ids, h, h, h, h, h, gc, gr, ng, cache4)


def _moba_kernel(pt_ref, qt_ref, k_ref, vt_ref, km_ref, z_ref, ck_hbm, o_ref, cm_ref,
                 mem_sc, acc_sc, pbuf, sems, *, job):
    i = pl.program_id(2)
    step = (pl.program_id(0) * pl.num_programs(1) + pl.program_id(1)) * pl.num_programs(2) + i
    n_steps = pl.num_programs(0) * pl.num_programs(1) * pl.num_programs(2)
    slot = _means_fetch(pt_ref, ck_hbm, pbuf, sems, step, n_steps, job)
    _means_reduce(pbuf, slot, cm_ref, job)

    blk = qt_ref.shape[1]
    nb = km_ref.shape[0]
    hd = B_HEAD_DIM
    heads = range(qt_ref.shape[0] // hd)
    hsl = [slice(h * hd, (h + 1) * hd) for h in heads]
    qts = [qt_ref[sl, :] for sl in hsl]

    row = lax.broadcasted_iota(jnp.int32, (nb, blk), 0)
    past = row < i
    key_i = lax.broadcasted_iota(jnp.int32, (blk, blk), 0)
    qry_i = lax.broadcasted_iota(jnp.int32, (blk, blk), 1)
    own = pl.ds(pl.multiple_of(i * blk, blk), blk)
    gates = [jnp.dot(km_ref[:, hsl[h]].astype(BF16), qts[h], preferred_element_type=F32) for h in heads]
    s_own = [jnp.dot(k_ref[own, hsl[h]], qts[h], preferred_element_type=F32) for h in heads]
    for h in heads:
        g = jnp.where(past, gates[h], -jnp.inf)
        rank = jnp.zeros(g.shape, F32)
        for n2 in range(nb):
            g2 = g[n2:n2 + 1, :]
            beats = (g2 > g) | ((g2 == g) & (n2 < row))
            rank = rank + jnp.where(beats, 1.0, 0.0)
        mem_sc[h] = jnp.where(past & (rank < MOBA_TOPK), 1.0, 0.0)
    m0, l0, p_own = [], [], []
    for h in heads:
        s = jnp.where(key_i <= qry_i, s_own[h], -jnp.inf)
        m = jnp.max(s, axis=0, keepdims=True)
        p = jnp.exp2(s - m)
        m0.append(m)
        l0.append(jnp.sum(p, axis=0, keepdims=True))
        p_own.append(p.astype(BF16))
    for h in heads:
        acc_sc[h] = jnp.dot(vt_ref[i, hsl[h], :], p_own[h], preferred_element_type=F32)

    def body(jj, carry):
        ms, ls = carry
        j0 = 2 * jj
        rows0 = pl.ds(pl.multiple_of(j0 * blk, blk), blk)
        rows1 = pl.ds(pl.multiple_of(j0 * blk + blk, blk), blk)
        new_m, new_l = [], []
        ss = []
        for h in heads:
            s0 = jnp.dot(k_ref[rows0, hsl[h]], qts[h], preferred_element_type=F32)
            s1 = jnp.dot(k_ref[rows1, hsl[h]], qts[h], preferred_element_type=F32)
            s0 = jnp.where(mem_sc[h, pl.ds(j0, 1), :] > 0.0, s0, -jnp.inf)
            s1 = jnp.where(mem_sc[h, pl.ds(j0 + 1, 1), :] > 0.0, s1, -jnp.inf)
            ss.append((s0, s1))
        ps = []
        for h in heads:
            s0, s1 = ss[h]
            m_new = jnp.maximum(ms[h], jnp.maximum(jnp.max(s0, axis=0, keepdims=True),
                                                   jnp.max(s1, axis=0, keepdims=True)))
            alpha = jnp.exp2(ms[h] - m_new)
            p0 = jnp.exp2(s0 - m_new)
            p1 = jnp.exp2(s1 - m_new)
            new_l.append(alpha * ls[h] + jnp.sum(p0, axis=0, keepdims=True) + jnp.sum(p1, axis=0, keepdims=True))
            new_m.append(m_new)
            ps.append((alpha, p0.astype(BF16), p1.astype(BF16)))
        for h in heads:
            alpha, p0, p1 = ps[h]
            acc_sc[h] = (alpha * acc_sc[h]
                         + jnp.dot(vt_ref[j0, hsl[h], :], p0, preferred_element_type=F32)
                         + jnp.dot(vt_ref[j0 + 1, hsl[h], :], p1, preferred_element_type=F32))
        return tuple(new_m), tuple(new_l)

    _, l_fin = lax.fori_loop(0, (i + 1) // 2, body, (tuple(m0), tuple(l0)))
    outs = [(acc_sc[h] / l_fin[h]).T for h in heads]
    for h in heads:
        o_ref[:, hsl[h]] = (outs[h] * _silu(z_ref[:, hsl[h]])).astype(o_ref.dtype)


def _moba_steps(bsz, t, d):
    nh = d // B_HEAD_DIM
    hps = MOBA_HEADS_PER_STEP if nh % MOBA_HEADS_PER_STEP == 0 else 1
    return hps, bsz * (nh // hps) * (t // MOBA_BLOCK)


def _moba_prompt(qt, kb, vt, kmean, h, col_z, bsz, t, d, page_ids, cache4, first_block, blocks_per_step):
    nh = d // B_HEAD_DIM
    nb = t // MOBA_BLOCK
    blk = MOBA_BLOCK
    hps, n_steps = _moba_steps(bsz, t, d)
    w = hps * B_HEAD_DIM
    ng = nh // hps
    bps = blocks_per_step
    job = _means_job(page_ids, cache4, first_block)
    return pl.pallas_call(
        functools.partial(_moba_kernel, job=job),
        grid_spec=pltpu.PrefetchScalarGridSpec(
            num_scalar_prefetch=1,
            grid=(bsz, ng, nb),
            in_specs=[pl.BlockSpec((None, w, blk), lambda b, g, i, pt: (b * nb + i, g, 0)),
                      pl.BlockSpec((t, w), lambda b, g, i, pt: (b, g)),
                      pl.BlockSpec((nb, w, blk), lambda b, g, i, pt: (b, g, 0)),
                      pl.BlockSpec((None, nb, w), lambda b, g, i, pt: (b, 0, g)),
                      pl.BlockSpec((blk, w), lambda b, g, i, pt: (b * nb + i, col_z * ng + g)),
                      pl.BlockSpec(memory_space=pl.ANY)],
            out_specs=[pl.BlockSpec((blk, w), lambda b, g, i, pt: (b * nb + i, g)),
                       pl.BlockSpec((bps, nh, B_HEAD_DIM), lambda b, g, i, pt: ((b * ng + g) * nb + i, 0, 0))],
            scratch_shapes=([pltpu.VMEM((hps, nb, blk), F32), pltpu.VMEM((hps, B_HEAD_DIM, blk), F32)]
                            + _means_scratch(cache4, job, bps))),
        out_shape=[jax.ShapeDtypeStruct((bsz * t, d), BF16),
                   jax.ShapeDtypeStruct((n_steps * bps, nh, B_HEAD_DIM), F32)],
        compiler_params=_cparams("arbitrary", "arbitrary", "arbitrary"),
        name="moba_prompt",
    )(page_ids, qt, kb, vt, kmean, h, cache4)


def _merge_kernel(ya_ref, yb_ref, ga_ref, gb_ref, wa_ref, wb_ref, o_ref):
    pa = jnp.dot(ya_ref[...].astype(BF16), wa_ref[...], preferred_element_type=F32)
    pb = jnp.dot(yb_ref[...].astype(BF16), wb_ref[...], preferred_element_type=F32)
    o_ref[...] = (jax.nn.sigmoid(ga_ref[...]) * pa + jax.nn.sigmoid(gb_ref[...]) * pb).astype(o_ref.dtype)


def _merge(ya, yb, h, col_ga, col_gb, wa, wb, tm, tn):
    m, d = ya.shape
    nj = d // tn
    return pl.pallas_call(
        _merge_kernel,
        grid=(m // tm, nj),
        in_specs=[pl.BlockSpec((tm, d), lambda i, j: (i, 0)),
                  pl.BlockSpec((tm, d), lambda i, j: (i, 0)),
                  pl.BlockSpec((tm, tn), lambda i, j: (i, col_ga * nj + j)),
                  pl.BlockSpec((tm, tn), lambda i, j: (i, col_gb * nj + j)),
                  pl.BlockSpec((d, tn), lambda i, j: (0, j)),
                  pl.BlockSpec((d, tn), lambda i, j: (0, j))],
        out_specs=pl.BlockSpec((tm, tn), lambda i, j: (i, j)),
        out_shape=jax.ShapeDtypeStruct((m, d), BF16),
        compiler_params=_cparams("parallel", "arbitrary"),
        name="merge",
    )(ya, yb, h, h, wa, wb)


def _out_kernel(mg_ref, wo_ref, x_ref, g_ref, b_ref, y_ref, *, alpha):
    tm = mg_ref.shape[0]
    strip = tm // 4 if tm % 32 == 0 else tm
    rows = [slice(r, r + strip) for r in range(0, tm, strip)]
    outs = [jnp.dot(mg_ref[rs, :], wo_ref[...], preferred_element_type=F32) for rs in rows]
    for rs, out in zip(rows, outs):
        y_ref[rs, :] = _layernorm(alpha * x_ref[rs, :] + out) * g_ref[...] + b_ref[...]


def _out(mg, wo, x, g, b, tm, alpha):
    m, d = x.shape
    row = pl.BlockSpec((tm, d), lambda i: (i, 0))
    vec = pl.BlockSpec((1, d), lambda i: (0, 0))
    return pl.pallas_call(
        functools.partial(_out_kernel, alpha=alpha),
        grid=(m // tm,),
        in_specs=[row, pl.BlockSpec((d, d), lambda i: (0, 0)), row, vec, vec],
        out_specs=row,
        out_shape=jax.ShapeDtypeStruct((m, d), F32),
        compiler_params=_cparams("parallel"),
        name="outproj",
    )(mg, wo, x, g, b)


def _mlstm_step_kernel(q_ref, k_ref, v_ref, oa_ref, za_ref, gc_ref, ng_ref, c_ref, n_ref, m_ref,
                       ya_ref, c_out, n_out, m_out, *, scale):
    H = A_HEADS
    da = q_ref.shape[1] // H
    gc = gc_ref[...]
    eye = lax.broadcasted_iota(jnp.int32, (da, da), 0) == lax.broadcasted_iota(jnp.int32, (da, da), 1)
    for h in range(H):
        sl = slice(h * da, (h + 1) * da)
        ig = gc[:, h:h + 1]
        lf = _log_sigmoid(gc[:, h + H:h + H + 1])
        m_prev = m_ref[h][:, :1]
        a = ig - lf
        g = jnp.maximum(m_prev, a)
        m_t = lf + g
        inter = jnp.exp(m_prev - g)
        dm = jnp.exp(a - g)

        q = q_ref[:, sl]
        ks = k_ref[:, sl] * scale
        v = v_ref[:, sl]
        cmat = c_ref[h]
        nvec = n_ref[h]
        q8 = jnp.broadcast_to(q, (8, da)).astype(BF16)
        cq = lax.dot_general(q8, cmat.astype(BF16), _NT, preferred_element_type=F32)[0:1]
        s = jnp.sum(q * ks, axis=1, keepdims=True) * dm
        num = inter * cq + s * v
        den = inter * jnp.sum(nvec * q, axis=1, keepdims=True) + s
        hh = num / jnp.maximum(jnp.abs(den), jnp.exp(-m_t))

        v_col = jnp.sum(jnp.where(eye, jnp.broadcast_to(v, (da, da)), 0.0), axis=1, keepdims=True)
        c_out[h] = inter * cmat + (dm * v_col) * ks
        n_out[h] = inter * nvec + dm * ks
        m_out[h] = jnp.broadcast_to(m_t, m_out.shape[1:])
        ya = _layernorm(hh) * ng_ref[:, sl] * jax.nn.sigmoid(oa_ref[:, sl]) * _silu(za_ref[:, sl])
        ya_ref[:, sl] = ya.astype(ya_ref.dtype)


def _mlstm_step(h3, gc3, ng, c_all, c_row0, n0, m0, d):
    bs = h3.shape[0]
    H = A_HEADS
    da = d // H
    row = lambda g: pl.BlockSpec((None, 1, d), lambda b: (b, 0, g))
    st = lambda r, c: pl.BlockSpec((None, H, r, c), lambda b: (b, 0, 0, 0))
    return pl.pallas_call(
        functools.partial(_mlstm_step_kernel, scale=da ** -0.5),
        grid=(bs,),
        in_specs=[row(0), row(1), row(2), row(3), row(4),
                  pl.BlockSpec((None, 1, LANES), lambda b: (b, 0, 0)),
                  pl.BlockSpec((1, d), lambda b: (0, 0)),
                  pl.BlockSpec((None, H, da, da), lambda b: (c_row0 + b, 0, 0, 0)),
                  st(1, da), st(1, LANES)],
        out_specs=[pl.BlockSpec((None, 1, d), lambda b: (b, 0, 0)),
                   st(da, da), st(1, da), st(1, LANES)],
        out_shape=[jax.ShapeDtypeStruct((bs, 1, d), F32),
                   jax.ShapeDtypeStruct((bs, H, da, da), F32),
                   jax.ShapeDtypeStruct((bs, H, 1, da), F32),
                   jax.ShapeDtypeStruct((bs, H, 1, LANES), F32)],
        compiler_params=_cparams("parallel"),
        name="mlstm_step",
    )(h3, h3, h3, h3, h3, gc3, ng, c_all, n0, m0)


def _topk_kernel(q_ref, km_ref, o_ref):
    q = q_ref[...]
    nb = km_ref.shape[0]
    lane = lax.broadcasted_iota(jnp.int32, o_ref.shape, 1)
    gate = jnp.full(o_ref.shape, -jnp.inf, F32)
    for n in range(nb):
        gate = jnp.where(lane == n, jnp.sum(km_ref[n] * q, axis=1, keepdims=True), gate)
    rank = jnp.zeros(o_ref.shape, F32)
    for n2 in range(nb):
        g2 = gate[:, n2:n2 + 1]
        beats = (g2 > gate) | ((g2 == gate) & (n2 < lane))
        rank = rank + jnp.where(beats, 1.0, 0.0)
    lane_f = lane.astype(F32)
    out = jnp.zeros(o_ref.shape, F32)
    for r in range(MOBA_TOPK):
        idx = jnp.sum(jnp.where((rank == r) & (lane < nb), lane_f, 0.0), axis=1, keepdims=True)
        out = jnp.where(lane == r, idx, out)
    o_ref[...] = out.astype(jnp.int32)


def _topk_blocks(q4, kmean):
    bs, nb, nh, hd = kmean.shape
    assert nb <= LANES
    return pl.pallas_call(
        _topk_kernel,
        grid=(bs,),
        in_specs=[pl.BlockSpec((None, nh, hd), lambda b: (b, 0, 0)),
                  pl.BlockSpec((None, nb, nh, hd), lambda b: (b, 0, 0, 0))],
        out_specs=pl.BlockSpec((None, nh, LANES), lambda b: (b, 0, 0)),
        out_shape=jax.ShapeDtypeStruct((bs, nh, LANES), jnp.int32),
        compiler_params=_cparams("parallel"),
        name="topk_blocks",
    )(q4, kmean)


def _decode_attn_kernel(pt_ref, ix_ref, q_ref, kn_ref, vn_ref, z_ref, ck_hbm, cv_hbm, o_ref,
                        kbuf, vbuf, sems, *, per_blk, scale):
    b = pl.program_id(0)
    nbat = pl.num_programs(0)
    nh = q_ref.shape[0]
    n_pages = MOBA_TOPK * per_blk

    def page_copies(bb, slot, h, p):
        r, half = divmod(p, per_blk)
        pid = pt_ref[bb, ix_ref[bb, h * MOBA_TOPK + r] * per_blk + half]
        return (pltpu.make_async_copy(ck_hbm.at[pid, :, h, :], kbuf.at[slot, h, p], sems.at[0, slot]),
                pltpu.make_async_copy(cv_hbm.at[pid, :, h, :], vbuf.at[slot, h, p], sems.at[1, slot]))

    def for_each_copy(bb, slot, fn):
        def per_head(h, carry):
            for p in range(n_pages):
                for cp in page_copies(bb, slot, h, p):
                    fn(cp)
            return carry
        lax.fori_loop(0, nh, per_head, 0)

    @pl.when(b == 0)
    def _():
        for_each_copy(0, 0, lambda cp: cp.start())

    @pl.when(b + 1 < nbat)
    def _():
        for_each_copy(b + 1, (b + 1) % 2, lambda cp: cp.start())

    slot = b % 2
    for_each_copy(b, slot, lambda cp: cp.wait())

    def per_head(h, carry):
        row = pl.ds(h, 1)
        q = q_ref[row, :]
        scores = [jnp.sum(kbuf[slot, h, p] * q, axis=1, keepdims=True) * scale for p in range(n_pages)]
        s_own = jnp.sum(kn_ref[row, :] * q, axis=1, keepdims=True) * scale
        m = s_own
        for sc in scores:
            m = jnp.maximum(m, jnp.max(sc, axis=0, keepdims=True))
        e_own = jnp.exp(s_own - m)
        l = e_own
        acc = e_own * vn_ref[row, :]
        for p, sc in enumerate(scores):
            e = jnp.exp(sc - m)
            l = l + jnp.sum(e, axis=0, keepdims=True)
            acc = acc + jnp.sum(e * vbuf[slot, h, p], axis=0, keepdims=True)
        o_ref[row, :] = acc / l * _silu(z_ref[row, :])
        return carry

    lax.fori_loop(0, nh, per_head, 0, unroll=4 if nh % 4 == 0 else 1)


def _decode_attn(cache_k4, cache_v4, page_ids, idx, q4, kn4, vn4, z4):
    _, page, nh, hd = cache_k4.shape
    bs = page_ids.shape[0]
    per_blk = MOBA_BLOCK // page
    n_pages = MOBA_TOPK * per_blk
    vec = pl.BlockSpec((None, nh, hd), lambda b, pt, ix: (b, 0, 0))
    hbm = pl.BlockSpec(memory_space=pl.ANY)
    return pl.pallas_call(
        functools.partial(_decode_attn_kernel, per_blk=per_blk, scale=hd ** -0.5),
        grid_spec=pltpu.PrefetchScalarGridSpec(
            num_scalar_prefetch=2,
            grid=(bs,),
            in_specs=[vec, vec, vec, vec, hbm, hbm],
            out_specs=vec,
            scratch_shapes=[pltpu.VMEM((2, nh, n_pages, page, hd), F32),
                            pltpu.VMEM((2, nh, n_pages, page, hd), F32),
                            pltpu.SemaphoreType.DMA((2, 2))]),
        out_shape=jax.ShapeDtypeStruct((bs, nh, hd), F32),
        compiler_params=_cparams("arbitrary"),
        name="decode_attn",
    )(page_ids, idx, q4, kn4, vn4, z4, cache_k4, cache_v4)


def _rope_tables(pos):
    half = B_HEAD_DIM // 2
    freqs = ROPE_THETA ** (-jnp.arange(half, dtype=F32) / half)
    ang = pos.astype(F32)[:, None] * freqs[None, :]
    cos = jnp.cos(ang)
    sin = jnp.sin(ang)
    return jnp.concatenate([cos, cos], -1), jnp.concatenate([-sin, sin], -1)


def _tile(n, pref):
    return pref if n % pref == 0 else n


def _prep_weights(w_in_all, layer, b_in, w_proj_a, w_proj_b, w_out, d):
    c0 = 5 * d
    c1 = c0 + 2 * A_HEADS
    w_in = w_in_all[layer]
    w_a = w_in.astype(BF16)
    w_b = w_a[:, c1:]
    w_g = w_in[:, c0:c1]
    wc = jnp.pad(w_g, ((0, 0), (0, LANES - 2 * A_HEADS))).astype(BF16)
    wr = w_g.T.astype(BF16)
    bc = jnp.pad(b_in[c0:c1], (0, LANES - 2 * A_HEADS))[None, :]
    br = b_in[c0:c1][:, None]
    return ((w_a, b_in[None, :c0]), (w_b, b_in[None, c1:]),
            wc, wr, bc, br, w_proj_a.astype(BF16), w_proj_b.astype(BF16), w_out.astype(BF16))


_QA, _KA, _VA, _OA, _ZA = range(5)
_QB, _KB, _VB, _ZB, _GA, _GB = range(6)


def _project(xb, part, tm):
    w, b = part
    return _inproj(xb, w, b, tm, _tile(b.shape[1], 1024))


def _project_with_means(xb, parts, tm, cache_k4, page_ids, first_block):
    bs, n_pages = page_ids.shape
    n_blocks = bs * n_pages * cache_k4.shape[1] // MOBA_BLOCK
    tiles = [(tm, _tile(b.shape[1], 2048)) for _, b in parts]
    steps = [(xb.shape[0] // tm_) * (b.shape[1] // tn_) for (tm_, tn_), (_, b) in zip(tiles, parts)]
    bps = max(1, -(-(n_blocks - first_block) // sum(steps)))
    hs, means, first = [], [], first_block
    for (w, b), (tm_, tn_), n_steps in zip(parts, tiles, steps):
        h, km = _inproj_means(xb, w, b, tm_, tn_, page_ids, cache_k4, first, bps)
        hs.append(h)
        means.append(km)
        first += n_steps * bps
    return hs[0], hs[1], means


def _layer_prompt(x, wts, ng, ln_g, ln_b, alpha, cache_k4, page_ids):
    bsz, t, d = x.shape
    m = bsz * t
    part_a, part_b, wc, wr, bc, br, wa, wb, wo = wts
    x2 = x.reshape(m, d)
    first_mlstm = _moba_steps(bsz, t, d)[1] * MEANS_PER_MOBA_STEP
    first_proj = first_mlstm + _mlstm_steps(bsz, t)[1] * MEANS_PER_MLSTM_STEP
    xb, gc, gr = _gates(x2, wc, wr, bc, br, _tile(m, 512))
    h_a, h_b, means_proj = _project_with_means(xb, (part_a, part_b), _tile(m, 512), cache_k4, page_ids, first_proj)
    ya, c1, n1, m1, means_mlstm = _mlstm_prompt(h_a, gc, gr, ng, bsz, t, d, page_ids, cache_k4,
                                                first_mlstm, MEANS_PER_MLSTM_STEP)
    cos, sin = _rope_tables(jnp.arange(t, dtype=jnp.int32))
    cos = jnp.tile(cos, (bsz, 1))
    sin = jnp.tile(sin, (bsz, 1))
    q_t, k_new, k_b, v_new, v_t, kmean = _rope(h_b, _QB, _KB, _VB, d, cos, sin, B_HEAD_DIM ** -0.5 * LOG2E)
    nb = t // MOBA_BLOCK
    yb, means_moba = _moba_prompt(q_t, k_b, v_t, kmean.reshape(bsz, nb, d), h_b, _ZB, bsz, t, d,
                                  page_ids, cache_k4, 0, MEANS_PER_MOBA_STEP)
    bs_c, n_pages = page_ids.shape
    n_blocks = bs_c * n_pages * cache_k4.shape[1] // MOBA_BLOCK
    kmean_cache = jnp.concatenate([means_moba, means_mlstm] + means_proj)[:n_blocks]
    kmean_cache = kmean_cache.reshape((bs_c, n_blocks // bs_c) + cache_k4.shape[2:])
    mg = _merge(ya, yb, h_b, _GA, _GB, wa, wb, _tile(m, 1024), _tile(d, 512))
    y = _out(mg, wo, x2, ln_g, ln_b, _tile(m, 512), alpha)
    nh = d // B_HEAD_DIM
    return (kmean_cache, y.reshape(bsz, t, d), k_new.reshape(bsz, t, nh, B_HEAD_DIM),
            v_new.reshape(bsz, t, nh, B_HEAD_DIM), c1, n1.reshape(bsz, A_HEADS, d // A_HEADS), m1[:, :, 0, 0])


def _layer_sample(x, wts, ng, ln_g, ln_b, alpha, cache_k4, cache_v4, page_ids, kmean, c_all, c_row0, n0, m0):
    bs, t, d = x.shape
    part_a, part_b, wc, wr, bc, br, wa, wb, wo = wts
    _, page, nh, hd = cache_k4.shape
    past = page_ids.shape[1] * page
    da = d // A_HEADS
    x2 = x.reshape(bs, d)
    xb, gc, _ = _gates(x2, wc, wr, bc, br, bs)
    h_a = _project(xb, part_a, bs)
    h_b = _project(xb, part_b, bs)
    ya, c1, n1, m1 = _mlstm_step(h_a.reshape(bs, 1, h_a.shape[1]), gc.reshape(bs, 1, LANES), ng, c_all, c_row0,
                                 n0.reshape(bs, A_HEADS, 1, da),
                                 jnp.broadcast_to(m0[:, :, None, None], (bs, A_HEADS, 1, LANES)), d)
    cos, sin = _rope_tables(jnp.full((bs,), past, dtype=jnp.int32))
    q_r, k_new, v_new = _rope_rows(h_b, _QB, _KB, _VB, d, cos, sin)
    heads = lambda a: a.reshape(bs, nh, hd)
    q4, kn4, vn4 = heads(q_r), heads(k_new), heads(v_new)
    idx = _topk_blocks(q4, kmean)[:, :, :MOBA_TOPK].reshape(bs, nh * MOBA_TOPK)
    yb = _decode_attn(cache_k4, cache_v4, page_ids, idx, q4, kn4, vn4, heads(h_b[:, _ZB * d:(_ZB + 1) * d]))
    mg = _merge(ya.reshape(bs, d), yb.reshape(bs, d), h_b, _GA, _GB, wa, wb, bs, _tile(d, 512))
    y = _out(mg, wo, x2, ln_g, ln_b, bs, alpha)
    return (y.reshape(bs, 1, d), kn4[:, None], vn4[:, None],
            c1, n1.reshape(bs, A_HEADS, da), m1[:, :, 0, 0])


def kernel(x_prompt, x_sample, cache_k, cache_v, page_table, state_mlstm_C, state_mlstm_n, state_mlstm_m,
           w_in, b_in, mh_norm_g, w_proj_a, w_proj_b, w_out, ln_g, ln_b):
    depth = w_in.shape[0]
    d = x_prompt.shape[-1]
    n_pages = page_table.shape[1]
    page = cache_k.shape[2]
    assert x_sample.shape[1] == 1 and d % (A_HEADS * LANES) == 0
    assert x_prompt.shape[1] % MOBA_BLOCK == 0 and MOBA_BLOCK % page == 0
    assert (n_pages * page) % MOBA_BLOCK == 0 and n_pages * page // MOBA_BLOCK >= MOBA_TOPK
    alpha = (2 * depth) ** 0.25
    n_phys = cache_k.shape[1]
    bs = x_sample.shape[0]
    cache_k3 = cache_k.reshape((depth * n_phys,) + cache_k.shape[2:])
    cache_v3 = cache_v.reshape((depth * n_phys,) + cache_v.shape[2:])
    c_all = state_mlstm_C.reshape((depth * bs,) + state_mlstm_C.shape[2:])
    yp, ys = x_prompt, x_sample
    new_p = [[] for _ in range(5)]
    new_s = [[] for _ in range(5)]
    for l in range(depth):
        wts = _prep_weights(w_in, l, b_in[l], w_proj_a[l], w_proj_b[l], w_out[l], d)
        ng, lg, lb = mh_norm_g[l][None, :], ln_g[l][None, :], ln_b[l][None, :]
        page_ids = page_table + l * n_phys
        kmean, yp, *st_p = _layer_prompt(yp, wts, ng, lg, lb, alpha, cache_k3, page_ids)
        ys, *st_s = _layer_sample(ys, wts, ng, lg, lb, alpha, cache_k3, cache_v3, page_ids, kmean,
                                  c_all, l * bs, state_mlstm_n[l], state_mlstm_m[l])
        for i in range(5):
            new_p[i].append(st_p[i])
            new_s[i].append(st_s[i])
    stack = lambda ts: ts[0][None] if len(ts) == 1 else jnp.stack(ts)
    k_p, v_p, c_p, n_p, m_p = [stack(t) for t in new_p]
    k_s, v_s, c_s, n_s, m_s = [stack(t) for t in new_s]
    return (yp, ys, k_p, v_p, c_p, n_p, m_p, k_s, v_s, c_s, n_s, m_s)
```

```python
import functools
import math

import jax
import jax.numpy as jnp
from jax import lax
from jax.experimental import pallas as pl
from jax.experimental.pallas import tpu as pltpu

F32 = jnp.float32
BF16 = jnp.bfloat16

A_HEADS = 4
A_CHUNK = 128
B_HEAD_DIM = 128
MOBA_BLOCK = 256
MOBA_TOPK = 3
ROPE_THETA = 10000.0
LN_EPS = 1e-5
LOG2E = math.log2(math.e)
LANES = 128
GATE_ROWS = 8
MOBA_HEADS_PER_STEP = 8
MEANS_PER_MOBA_STEP = 4
MEANS_PER_MLSTM_STEP = 4
VMEM_LIMIT = 60 * 1024 * 1024

_NT = (((1,), (1,)), ((), ()))


def _cparams(*sem):
    return pltpu.CompilerParams(dimension_semantics=sem, vmem_limit_bytes=VMEM_LIMIT)


def _log_sigmoid(x):
    return jnp.minimum(x, 0.0) - jnp.log1p(jnp.exp(-jnp.abs(x)))


def _silu(x):
    return x * jax.nn.sigmoid(x)


def _layernorm(x):
    mu = jnp.mean(x, axis=-1, keepdims=True)
    xc = x - mu
    return xc * lax.rsqrt(jnp.mean(xc * xc, axis=-1, keepdims=True) + LN_EPS)


def _inproj_kernel(x_ref, w_ref, b_ref, o_ref):
    o_ref[...] = jnp.dot(x_ref[...], w_ref[...], preferred_element_type=F32) + b_ref[...]


def _inproj(xb, w, b, tm, tn):
    m, k = xb.shape
    n = b.shape[1]
    return pl.pallas_call(
        _inproj_kernel,
        grid=(n // tn, m // tm),
        in_specs=[pl.BlockSpec((tm, k), lambda j, i: (i, 0)),
                  pl.BlockSpec((k, tn), lambda j, i: (0, j)),
                  pl.BlockSpec((1, tn), lambda j, i: (0, j))],
        out_specs=pl.BlockSpec((tm, tn), lambda j, i: (i, j)),
        out_shape=jax.ShapeDtypeStruct((m, n), F32),
        compiler_params=_cparams("parallel", "parallel"),
        name="inproj",
    )(xb, w, b)


def _means_fetch(pt_ref, ck_hbm, pbuf, sems, step, n_steps, job):
    first_block, n_blocks, per_blk, n_pages = job
    bps = pbuf.shape[1] // per_blk

    def page_copy(s, slot, r, p):
        blk = jnp.minimum(first_block + s * bps + r, n_blocks - 1)
        g = blk * per_blk + p
        pid = pt_ref[g // n_pages, g % n_pages]
        return pltpu.make_async_copy(ck_hbm.at[pid], pbuf.at[slot, r * per_blk + p], sems.at[slot])

    def for_each_copy(s, slot, fn):
        for r in range(bps):
            for p in range(per_blk):
                fn(page_copy(s, slot, r, p))

    @pl.when(step == 0)
    def _():
        for_each_copy(0, 0, lambda cp: cp.start())

    @pl.when(step + 1 < n_steps)
    def _():
        for_each_copy(step + 1, (step + 1) % 2, lambda cp: cp.start())

    slot = step % 2
    for_each_copy(step, slot, lambda cp: cp.wait())
    return slot


def _means_reduce(pbuf, slot, km_ref, job):
    per_blk = job[2]
    rows = per_blk * pbuf.shape[2]
    for r in range(km_ref.shape[0]):
        tot = jnp.sum(pbuf[slot, r * per_blk], axis=0)
        for p in range(1, per_blk):
            tot = tot + jnp.sum(pbuf[slot, r * per_blk + p], axis=0)
        km_ref[r] = tot * (1.0 / rows)


def _means_job(page_ids, cache4, first_block):
    bs, n_pages = page_ids.shape
    per_blk = MOBA_BLOCK // cache4.shape[1]
    return (first_block, bs * n_pages // per_blk, per_blk, n_pages)


def _means_scratch(cache4, job, bps):
    _, page, nh, hd = cache4.shape
    return [pltpu.VMEM((2, bps * job[2], page, nh, hd), F32), pltpu.SemaphoreType.DMA((2,))]


def _inproj_means_kernel(pt_ref, x_ref, w_ref, b_ref, ck_hbm, o_ref, km_ref, pbuf, sems, *, job):
    ni = pl.num_programs(1)
    step = pl.program_id(0) * ni + pl.program_id(1)
    slot = _means_fetch(pt_ref, ck_hbm, pbuf, sems, step, pl.num_programs(0) * ni, job)
    o_ref[...] = jnp.dot(x_ref[...], w_ref[...], preferred_element_type=F32) + b_ref[...]
    _means_reduce(pbuf, slot, km_ref, job)


def _inproj_means(xb, w, b, tm, tn, page_ids, cache4, first_block, blocks_per_step):
    m, k = xb.shape
    n = b.shape[1]
    _, page, nh, hd = cache4.shape
    nj, ni = n // tn, m // tm
    bps = blocks_per_step
    job = _means_job(page_ids, cache4, first_block)
    return pl.pallas_call(
        functools.partial(_inproj_means_kernel, job=job),
        grid_spec=pltpu.PrefetchScalarGridSpec(
            num_scalar_prefetch=1,
            grid=(nj, ni),
            in_specs=[pl.BlockSpec((tm, k), lambda j, i, pt: (i, 0)),
                      pl.BlockSpec((k, tn), lambda j, i, pt: (0, j)),
                      pl.BlockSpec((1, tn), lambda j, i, pt: (0, j)),
                      pl.BlockSpec(memory_space=pl.ANY)],
            out_specs=[pl.BlockSpec((tm, tn), lambda j, i, pt: (i, j)),
                       pl.BlockSpec((bps, nh, hd), lambda j, i, pt: (j * ni + i, 0, 0))],
            scratch_shapes=_means_scratch(cache4, job, bps)),
        out_shape=[jax.ShapeDtypeStruct((m, n), F32),
                   jax.ShapeDtypeStruct((nj * ni * bps, nh, hd), F32)],
        compiler_params=_cparams("arbitrary", "arbitrary"),
        name="inproj_means",
    )(page_ids, xb, w, b, cache4)


def _gates_kernel(x_ref, wc_ref, wr_ref, bc_ref, br_ref, xb_ref, gc_ref, gr_ref):
    xb = x_ref[...].astype(BF16)
    xb_ref[...] = xb
    gc_ref[...] = jnp.dot(xb, wc_ref[...], preferred_element_type=F32) + bc_ref[...]
    gr_ref[...] = lax.dot_general(wr_ref[...], xb, _NT, preferred_element_type=F32) + br_ref[...]


def _gates(x, wc, wr, bc, br, tm):
    m, k = x.shape
    return pl.pallas_call(
        _gates_kernel,
        grid=(m // tm,),
        in_specs=[pl.BlockSpec((tm, k), lambda i: (i, 0)),
                  pl.BlockSpec((k, LANES), lambda i: (0, 0)),
                  pl.BlockSpec((GATE_ROWS, k), lambda i: (0, 0)),
                  pl.BlockSpec((1, LANES), lambda i: (0, 0)),
                  pl.BlockSpec((GATE_ROWS, 1), lambda i: (0, 0))],
        out_specs=[pl.BlockSpec((tm, k), lambda i: (i, 0)),
                   pl.BlockSpec((tm, LANES), lambda i: (i, 0)),
                   pl.BlockSpec((GATE_ROWS, tm), lambda i: (0, i))],
        out_shape=[jax.ShapeDtypeStruct((m, k), BF16),
                   jax.ShapeDtypeStruct((m, LANES), F32),
                   jax.ShapeDtypeStruct((GATE_ROWS, m), F32)],
        compiler_params=_cparams("parallel"),
        name="gates",
    )(x, wc, wr, bc, br)


def _rotate(x, cos, sin):
    return x * cos + pltpu.roll(x, B_HEAD_DIM // 2, 1) * sin


def _rope_kernel(q_ref, k_ref, v_ref, cos_ref, sin_ref, qt_ref, ko_ref, kb_ref, vo_ref, vt_ref, km_ref, *, scale):
    cos = cos_ref[...]
    sin = sin_ref[...]
    tm, d = q_ref.shape
    for h in range(d // B_HEAD_DIM):
        sl = slice(h * B_HEAD_DIM, (h + 1) * B_HEAD_DIM)
        qr = _rotate(q_ref[:, sl], cos, sin)
        kr = _rotate(k_ref[:, sl], cos, sin)
        v = v_ref[:, sl]
        qt_ref[sl, :] = (qr * scale).T.astype(BF16)
        ko_ref[:, sl] = kr
        kb_ref[:, sl] = kr.astype(BF16)
        km_ref[:, sl] = jnp.sum(kr, axis=0, keepdims=True) * (1.0 / tm)
        vo_ref[:, sl] = v
        vt_ref[sl, :] = v.T.astype(BF16)


def _rope(h, col_q, col_k, col_v, d, cos, sin, scale):
    m = h.shape[0]
    tm = MOBA_BLOCK
    nt = m // tm
    row = lambda c: pl.BlockSpec((tm, d), lambda i: (i, c))
    full = pl.BlockSpec((tm, d), lambda i: (i, 0))
    tab = pl.BlockSpec((tm, B_HEAD_DIM), lambda i: (i, 0))
    tr = pl.BlockSpec((None, d, tm), lambda i: (i, 0, 0))
    return pl.pallas_call(
        functools.partial(_rope_kernel, scale=scale),
        grid=(nt,),
        in_specs=[row(col_q), row(col_k), row(col_v), tab, tab],
        out_specs=[tr, full, full, full, tr, pl.BlockSpec((None, 1, d), lambda i: (i, 0, 0))],
        out_shape=[jax.ShapeDtypeStruct((nt, d, tm), BF16),
                   jax.ShapeDtypeStruct((m, d), F32),
                   jax.ShapeDtypeStruct((m, d), BF16),
                   jax.ShapeDtypeStruct((m, d), F32),
                   jax.ShapeDtypeStruct((nt, d, tm), BF16),
                   jax.ShapeDtypeStruct((nt, 1, d), F32)],
        compiler_params=_cparams("parallel"),
        name="rope",
    )(h, h, h, cos, sin)


def _rope_rows_kernel(q_ref, k_ref, v_ref, cos_ref, sin_ref, qo_ref, ko_ref, vo_ref):
    cos = cos_ref[...]
    sin = sin_ref[...]
    for h in range(q_ref.shape[1] // B_HEAD_DIM):
        sl = slice(h * B_HEAD_DIM, (h + 1) * B_HEAD_DIM)
        qo_ref[:, sl] = _rotate(q_ref[:, sl], cos, sin)
        ko_ref[:, sl] = _rotate(k_ref[:, sl], cos, sin)
    vo_ref[...] = v_ref[...]


def _rope_rows(h, col_q, col_k, col_v, d, cos, sin):
    m = h.shape[0]
    row = lambda c: pl.BlockSpec((m, d), lambda i: (0, c))
    full = pl.BlockSpec((m, d), lambda i: (0, 0))
    tab = pl.BlockSpec((m, B_HEAD_DIM), lambda i: (0, 0))
    return pl.pallas_call(
        _rope_rows_kernel,
        grid=(1,),
        in_specs=[row(col_q), row(col_k), row(col_v), tab, tab],
        out_specs=[full, full, full],
        out_shape=[jax.ShapeDtypeStruct((m, d), F32)] * 3,
        compiler_params=_cparams("arbitrary"),
        name="rope_rows",
    )(h, h, h, cos, sin)


def _mlstm_kernel(pt_ref, q_ref, k_ref, v_ref, oa_ref, za_ref, gc_ref, gr_ref, ng_ref, ck_hbm,
                  ya_ref, c_out, n_out, m_out, cm_ref, c_sc, n_sc, m_sc, pbuf, sems, *, scale, job):
    c = pl.program_id(1)
    nc = pl.num_programs(1)
    slot = _means_fetch(pt_ref, ck_hbm, pbuf, sems, pl.program_id(0) * nc + c, pl.num_programs(0) * nc, job)
    L = q_ref.shape[0]
    H = A_HEADS
    D = q_ref.shape[1] // H
    heads = range(H)
    hsl = [slice(h * D, (h + 1) * D) for h in heads]

    @pl.when(c == 0)
    def _():
        c_sc[...] = jnp.zeros(c_sc.shape, F32)
        n_sc[...] = jnp.zeros(n_sc.shape, F32)
        m_sc[...] = jnp.zeros(m_sc.shape, F32)

    _means_reduce(pbuf, slot, cm_ref, job)

    gc = gc_ref[...]
    gr = gr_ref[...]
    t_i = lax.broadcasted_iota(jnp.int32, (L, L), 0)
    s_i = lax.broadcasted_iota(jnp.int32, (L, L), 1)
    causal = s_i <= t_i
    def head_gates(h):
        ig_col, fa_col = gc[:, h:h + 1], gc[:, h + H:h + H + 1]
        ig_row, fa_row = gr[h:h + 1, :], gr[h + H:h + H + 1, :]
        lf_col = _log_sigmoid(fa_col)
        lf_row = _log_sigmoid(fa_row)
        b_col = jnp.sum(jnp.where(causal, jnp.broadcast_to(lf_row, (L, L)), 0.0), axis=1, keepdims=True)
        b_row = jnp.sum(jnp.where(t_i <= s_i, jnp.broadcast_to(lf_col, (L, L)), 0.0), axis=0, keepdims=True)
        a_row = ig_row - b_row
        a_col = ig_col - b_col
        cm_col = jnp.max(jnp.where(causal, jnp.broadcast_to(a_row, (L, L)), -jnp.inf), axis=1, keepdims=True)
        m_prev = m_sc[h]
        g_col = jnp.maximum(m_prev, cm_col)
        g_last = jnp.maximum(m_prev, jnp.max(a_row, axis=1, keepdims=True))
        return dict(
            dmat=jnp.exp(jnp.where(causal, a_row - g_col, -jnp.inf)),
            inter=jnp.exp(m_prev - g_col),
            inv_floor=jnp.exp(-(b_col + g_col)),
            m_last=jnp.sum(lf_row, axis=1, keepdims=True) + g_last,
            w_col=jnp.exp(a_col - g_last),
            decay=jnp.exp(m_prev - g_last))

    def first_matmuls(h):
        q = q_ref[:, hsl[h]]
        ks = k_ref[:, hsl[h]] * scale
        qb = q.astype(BF16)
        s_raw = lax.dot_general(qb, ks.astype(BF16), _NT, preferred_element_type=F32)
        cq = lax.dot_general(qb, c_sc[h].astype(BF16), _NT, preferred_element_type=F32)
        return head_gates(h), q, ks, s_raw, cq

    def rest(h, g, q, ks, s_raw, cq):
        v = v_ref[:, hsl[h]]
        s = s_raw * g["dmat"]
        num = g["inter"] * cq + jnp.dot(s.astype(BF16), v.astype(BF16), preferred_element_type=F32)
        nvec = n_sc[h]
        den = g["inter"] * jnp.sum(q * nvec, axis=1, keepdims=True) + jnp.sum(s, axis=1, keepdims=True)
        hh = num / jnp.maximum(jnp.abs(den), g["inv_floor"])
        ya = (_layernorm(hh) * ng_ref[:, hsl[h]] * jax.nn.sigmoid(oa_ref[:, hsl[h]])
              * _silu(za_ref[:, hsl[h]]))
        ya_ref[:, hsl[h]] = ya.astype(ya_ref.dtype)
        kw = ks * g["w_col"]
        c_sc[h] = g["decay"] * c_sc[h] + jnp.dot(v.T.astype(BF16), kw.astype(BF16), preferred_element_type=F32)
        n_sc[h] = g["decay"] * nvec + jnp.sum(kw, axis=0, keepdims=True)
        m_sc[h] = g["m_last"]

    ahead = first_matmuls(0)
    for h in heads:
        cur = ahead
        if h + 1 < H:
            ahead = first_matmuls(h + 1)
        rest(h, *cur)

    @pl.when(c == nc - 1)
    def _():
        c_out[...] = c_sc[...]
        n_out[...] = n_sc[...]
        for h in heads:
            m_out[h] = jnp.broadcast_to(m_sc[h], m_out.shape[1:])


def _mlstm_steps(bsz, t):
    L = t if t <= A_CHUNK else math.gcd(t, A_CHUNK)
    return L, bsz * (t // L)


def _mlstm_prompt(h, gc, gr, ng, bsz, t, d, page_ids, cache4, first_block, blocks_per_step):
    da = d // A_HEADS
    L, n_steps = _mlstm_steps(bsz, t)
    nc = t // L
    H = A_HEADS
    bps = blocks_per_step
    job = _means_job(page_ids, cache4, first_block)
    blk = lambda g: pl.BlockSpec((L, d), lambda b, c, pt: (b * nc + c, g))
    return pl.pallas_call(
        functools.partial(_mlstm_kernel, scale=da ** -0.5, job=job),
        grid_spec=pltpu.PrefetchScalarGridSpec(
            num_scalar_prefetch=1,
            grid=(bsz, nc),
            in_specs=[blk(0), blk(1), blk(2), blk(3), blk(4),
                      pl.BlockSpec((L, LANES), lambda b, c, pt: (b * nc + c, 0)),
                      pl.BlockSpec((GATE_ROWS, L), lambda b, c, pt: (0, b * nc + c)),
                      pl.BlockSpec((1, d), lambda b, c, pt: (0, 0)),
                      pl.BlockSpec(memory_space=pl.ANY)],
            out_specs=[pl.BlockSpec((L, d), lambda b, c, pt: (b * nc + c, 0)),
                       pl.BlockSpec((None, H, da, da), lambda b, c, pt: (b, 0, 0, 0)),
                       pl.BlockSpec((None, H, 1, da), lambda b, c, pt: (b, 0, 0, 0)),
                       pl.BlockSpec((None, H, 1, LANES), lambda b, c, pt: (b, 0, 0, 0)),
                       pl.BlockSpec((bps,) + cache4.shape[2:], lambda b, c, pt: (b * nc + c, 0, 0))],
            scratch_shapes=([pltpu.VMEM((H, da, da), F32), pltpu.VMEM((H, 1, da), F32), pltpu.VMEM((H, 1, 1), F32)]
                            + _means_scratch(cache4, job, bps))),
        out_shape=[jax.ShapeDtypeStruct((bsz * t, d), BF16),
                   jax.ShapeDtypeStruct((bsz, H, da, da), F32),
                   jax.ShapeDtypeStruct((bsz, H, 1, da), F32),
                   jax.ShapeDtypeStruct((bsz, H, 1, LANES), F32),
                   jax.ShapeDtypeStruct((n_steps * bps,) + cache4.shape[2:], F32)],
        compiler_params=_cparams("arbitrary", "arbitrary"),
        name="mlstm_prompt",
    )(page_ids, h, h, h, h, h, gc, gr, ng, cache4)


def _moba_kernel(pt_ref, qt_ref, k_ref, vt_ref, km_ref, z_ref, ck_hbm, o_ref, cm_ref,
                 mem_sc, acc_sc, pbuf, sems, *, job):
    i = pl.program_id(2)
    step = (pl.program_id(0) * pl.num_programs(1) + pl.program_id(1)) * pl.num_programs(2) + i
    n_steps = pl.num_programs(0) * pl.num_programs(1) * pl.num_programs(2)
    slot = _means_fetch(pt_ref, ck_hbm, pbuf, sems, step, n_steps, job)
    _means_reduce(pbuf, slot, cm_ref, job)

    blk = qt_ref.shape[1]
    nb = km_ref.shape[0]
    hd = B_HEAD_DIM
    heads = range(qt_ref.shape[0] // hd)
    hsl = [slice(h * hd, (h + 1) * hd) for h in heads]
    qts = [qt_ref[sl, :] for sl in hsl]

    row = lax.broadcasted_iota(jnp.int32, (nb, blk), 0)
    past = row < i
    key_i = lax.broadcasted_iota(jnp.int32, (blk, blk), 0)
    qry_i = lax.broadcasted_iota(jnp.int32, (blk, blk), 1)
    own = pl.ds(pl.multiple_of(i * blk, blk), blk)
    gates = [jnp.dot(km_ref[:, hsl[h]].astype(BF16), qts[h], preferred_element_type=F32) for h in heads]
    s_own = [jnp.dot(k_ref[own, hsl[h]], qts[h], preferred_element_type=F32) for h in heads]
    for h in heads:
        g = jnp.where(past, gates[h], -jnp.inf)
        rank = jnp.zeros(g.shape, F32)
        for n2 in range(nb):
            g2 = g[n2:n2 + 1, :]
            beats = (g2 > g) | ((g2 == g) & (n2 < row))
            rank = rank + jnp.where(beats, 1.0, 0.0)
        mem_sc[h] = jnp.where(past & (rank < MOBA_TOPK), 1.0, 0.0)
    m0, l0, p_own = [], [], []
    for h in heads:
        s = jnp.where(key_i <= qry_i, s_own[h], -jnp.inf)
        m = jnp.max(s, axis=0, keepdims=True)
        p = jnp.exp2(s - m)
        m0.append(m)
        l0.append(jnp.sum(p, axis=0, keepdims=True))
        p_own.append(p.astype(BF16))
    for h in heads:
        acc_sc[h] = jnp.dot(vt_ref[i, hsl[h], :], p_own[h], preferred_element_type=F32)

    def body(jj, carry):
        ms, ls = carry
        j0 = 2 * jj
        rows0 = pl.ds(pl.multiple_of(j0 * blk, blk), blk)
        rows1 = pl.ds(pl.multiple_of(j0 * blk + blk, blk), blk)
        new_m, new_l = [], []
        ss = []
        for h in heads:
            s0 = jnp.dot(k_ref[rows0, hsl[h]], qts[h], preferred_element_type=F32)
            s1 = jnp.dot(k_ref[rows1, hsl[h]], qts[h], preferred_element_type=F32)
            s0 = jnp.where(mem_sc[h, pl.ds(j0, 1), :] > 0.0, s0, -jnp.inf)
            s1 = jnp.where(mem_sc[h, pl.ds(j0 + 1, 1), :] > 0.0, s1, -jnp.inf)
            ss.append((s0, s1))
        ps = []
        for h in heads:
            s0, s1 = ss[h]
            m_new = jnp.maximum(ms[h], jnp.maximum(jnp.max(s0, axis=0, keepdims=True),
                                                   jnp.max(s1, axis=0, keepdims=True)))
            alpha = jnp.exp2(ms[h] - m_new)
            p0 = jnp.exp2(s0 - m_new)
            p1 = jnp.exp2(s1 - m_new)
            new_l.append(alpha * ls[h] + jnp.sum(p0, axis=0, keepdims=True) + jnp.sum(p1, axis=0, keepdims=True))
            new_m.append(m_new)
            ps.append((alpha, p0.astype(BF16), p1.astype(BF16)))
        for h in heads:
            alpha, p0, p1 = ps[h]
            acc_sc[h] = (alpha * acc_sc[h]
                         + jnp.dot(vt_ref[j0, hsl[h], :], p0, preferred_element_type=F32)
                         + jnp.dot(vt_ref[j0 + 1, hsl[h], :], p1, preferred_element_type=F32))
        return tuple(new_m), tuple(new_l)

    _, l_fin = lax.fori_loop(0, (i + 1) // 2, body, (tuple(m0), tuple(l0)))
    outs = [(acc_sc[h] / l_fin[h]).T for h in heads]
    for h in heads:
        o_ref[:, hsl[h]] = (outs[h] * _silu(z_ref[:, hsl[h]])).astype(o_ref.dtype)


def _moba_steps(bsz, t, d):
    nh = d // B_HEAD_DIM
    hps = MOBA_HEADS_PER_STEP if nh % MOBA_HEADS_PER_STEP == 0 else 1
    return hps, bsz * (nh // hps) * (t // MOBA_BLOCK)


def _moba_prompt(qt, kb, vt, kmean, h, col_z, bsz, t, d, page_ids, cache4, first_block, blocks_per_step):
    nh = d // B_HEAD_DIM
    nb = t // MOBA_BLOCK
    blk = MOBA_BLOCK
    hps, n_steps = _moba_steps(bsz, t, d)
    w = hps * B_HEAD_DIM
    ng = nh // hps
    bps = blocks_per_step
    job = _means_job(page_ids, cache4, first_block)
    return pl.pallas_call(
        functools.partial(_moba_kernel, job=job),
        grid_spec=pltpu.PrefetchScalarGridSpec(
            num_scalar_prefetch=1,
            grid=(bsz, ng, nb),
            in_specs=[pl.BlockSpec((None, w, blk), lambda b, g, i, pt: (b * nb + i, g, 0)),
                      pl.BlockSpec((t, w), lambda b, g, i, pt: (b, g)),
                      pl.BlockSpec((nb, w, blk), lambda b, g, i, pt: (b, g, 0)),
                      pl.BlockSpec((None, nb, w), lambda b, g, i, pt: (b, 0, g)),
                      pl.BlockSpec((blk, w), lambda b, g, i, pt: (b * nb + i, col_z * ng + g)),
                      pl.BlockSpec(memory_space=pl.ANY)],
            out_specs=[pl.BlockSpec((blk, w), lambda b, g, i, pt: (b * nb + i, g)),
                       pl.BlockSpec((bps, nh, B_HEAD_DIM), lambda b, g, i, pt: ((b * ng + g) * nb + i, 0, 0))],
            scratch_shapes=([pltpu.VMEM((hps, nb, blk), F32), pltpu.VMEM((hps, B_HEAD_DIM, blk), F32)]
                            + _means_scratch(cache4, job, bps))),
        out_shape=[jax.ShapeDtypeStruct((bsz * t, d), BF16),
                   jax.ShapeDtypeStruct((n_steps * bps, nh, B_HEAD_DIM), F32)],
        compiler_params=_cparams("arbitrary", "arbitrary", "arbitrary"),
        name="moba_prompt",
    )(page_ids, qt, kb, vt, kmean, h, cache4)


def _merge_kernel(ya_ref, yb_ref, ga_ref, gb_ref, wa_ref, wb_ref, o_ref):
    pa = jnp.dot(ya_ref[...].astype(BF16), wa_ref[...], preferred_element_type=F32)
    pb = jnp.dot(yb_ref[...].astype(BF16), wb_ref[...], preferred_element_type=F32)
    o_ref[...] = (jax.nn.sigmoid(ga_ref[...]) * pa + jax.nn.sigmoid(gb_ref[...]) * pb).astype(o_ref.dtype)


def _merge(ya, yb, h, col_ga, col_gb, wa, wb, tm, tn):
    m, d = ya.shape
    nj = d // tn
    return pl.pallas_call(
        _merge_kernel,
        grid=(m // tm, nj),
        in_specs=[pl.BlockSpec((tm, d), lambda i, j: (i, 0)),
                  pl.BlockSpec((tm, d), lambda i, j: (i, 0)),
                  pl.BlockSpec((tm, tn), lambda i, j: (i, col_ga * nj + j)),
                  pl.BlockSpec((tm, tn), lambda i, j: (i, col_gb * nj + j)),
                  pl.BlockSpec((d, tn), lambda i, j: (0, j)),
                  pl.BlockSpec((d, tn), lambda i, j: (0, j))],
        out_specs=pl.BlockSpec((tm, tn), lambda i, j: (i, j)),
        out_shape=jax.ShapeDtypeStruct((m, d), BF16),
        compiler_params=_cparams("parallel", "arbitrary"),
        name="merge",
    )(ya, yb, h, h, wa, wb)


def _out_kernel(mg_ref, wo_ref, x_ref, g_ref, b_ref, y_ref, *, alpha):
    tm = mg_ref.shape[0]
    strip = tm // 4 if tm % 32 == 0 else tm
    rows = [slice(r, r + strip) for r in range(0, tm, strip)]
    outs = [jnp.dot(mg_ref[rs, :], wo_ref[...], preferred_element_type=F32) for rs in rows]
    for rs, out in zip(rows, outs):
        y_ref[rs, :] = _layernorm(alpha * x_ref[rs, :] + out) * g_ref[...] + b_ref[...]


def _out(mg, wo, x, g, b, tm, alpha):
    m, d = x.shape
    row = pl.BlockSpec((tm, d), lambda i: (i, 0))
    vec = pl.BlockSpec((1, d), lambda i: (0, 0))
    return pl.pallas_call(
        functools.partial(_out_kernel, alpha=alpha),
        grid=(m // tm,),
        in_specs=[row, pl.BlockSpec((d, d), lambda i: (0, 0)), row, vec, vec],
        out_specs=row,
        out_shape=jax.ShapeDtypeStruct((m, d), F32),
        compiler_params=_cparams("parallel"),
        name="outproj",
    )(mg, wo, x, g, b)


def _mlstm_step_kernel(q_ref, k_ref, v_ref, oa_ref, za_ref, gc_ref, ng_ref, c_ref, n_ref, m_ref,
                       ya_ref, c_out, n_out, m_out, *, scale):
    H = A_HEADS
    da = q_ref.shape[1] // H
    gc = gc_ref[...]
    eye = lax.broadcasted_iota(jnp.int32, (da, da), 0) == lax.broadcasted_iota(jnp.int32, (da, da), 1)
    for h in range(H):
        sl = slice(h * da, (h + 1) * da)
        ig = gc[:, h:h + 1]
        lf = _log_sigmoid(gc[:, h + H:h + H + 1])
        m_prev = m_ref[h][:, :1]
        a = ig - lf
        g = jnp.maximum(m_prev, a)
        m_t = lf + g
        inter = jnp.exp(m_prev - g)
        dm = jnp.exp(a - g)

        q = q_ref[:, sl]
        ks = k_ref[:, sl] * scale
        v = v_ref[:, sl]
        cmat = c_ref[h]
        nvec = n_ref[h]
        q8 = jnp.broadcast_to(q, (8, da)).astype(BF16)
        cq = lax.dot_general(q8, cmat.astype(BF16), _NT, preferred_element_type=F32)[0:1]
        s = jnp.sum(q * ks, axis=1, keepdims=True) * dm
        num = inter * cq + s * v
        den = inter * jnp.sum(nvec * q, axis=1, keepdims=True) + s
        hh = num / jnp.maximum(jnp.abs(den), jnp.exp(-m_t))

        v_col = jnp.sum(jnp.where(eye, jnp.broadcast_to(v, (da, da)), 0.0), axis=1, keepdims=True)
        c_out[h] = inter * cmat + (dm * v_col) * ks
        n_out[h] = inter * nvec + dm * ks
        m_out[h] = jnp.broadcast_to(m_t, m_out.shape[1:])
        ya = _layernorm(hh) * ng_ref[:, sl] * jax.nn.sigmoid(oa_ref[:, sl]) * _silu(za_ref[:, sl])
        ya_ref[:, sl] = ya.astype(ya_ref.dtype)


def _mlstm_step(h3, gc3, ng, c_all, c_row0, n0, m0, d):
    bs = h3.shape[0]
    H = A_HEADS
    da = d // H
    row = lambda g: pl.BlockSpec((None, 1, d), lambda b: (b, 0, g))
    st = lambda r, c: pl.BlockSpec((None, H, r, c), lambda b: (b, 0, 0, 0))
    return pl.pallas_call(
        functools.partial(_mlstm_step_kernel, scale=da ** -0.5),
        grid=(bs,),
        in_specs=[row(0), row(1), row(2), row(3), row(4),
                  pl.BlockSpec((None, 1, LANES), lambda b: (b, 0, 0)),
                  pl.BlockSpec((1, d), lambda b: (0, 0)),
                  pl.BlockSpec((None, H, da, da), lambda b: (c_row0 + b, 0, 0, 0)),
                  st(1, da), st(1, LANES)],
        out_specs=[pl.BlockSpec((None, 1, d), lambda b: (b, 0, 0)),
                   st(da, da), st(1, da), st(1, LANES)],
        out_shape=[jax.ShapeDtypeStruct((bs, 1, d), F32),
                   jax.ShapeDtypeStruct((bs, H, da, da), F32),
                   jax.ShapeDtypeStruct((bs, H, 1, da), F32),
                   jax.ShapeDtypeStruct((bs, H, 1, LANES), F32)],
        compiler_params=_cparams("parallel"),
        name="mlstm_step",
    )(h3, h3, h3, h3, h3, gc3, ng, c_all, n0, m0)


def _topk_kernel(q_ref, km_ref, o_ref):
    nb = km_ref.shape[1]
    shape = o_ref.shape[1:]
    lane = lax.broadcasted_iota(jnp.int32, shape, 1)
    lane_f = lane.astype(F32)
    for s in range(q_ref.shape[0]):
        q = q_ref[s]
        gate = jnp.full(shape, -jnp.inf, F32)
        for n in range(nb):
            gate = jnp.where(lane == n, jnp.sum(km_ref[s, n] * q, axis=1, keepdims=True), gate)
        rank = jnp.zeros(shape, F32)
        for n2 in range(nb):
            g2 = gate[:, n2:n2 + 1]
            beats = (g2 > gate) | ((g2 == gate) & (n2 < lane))
            rank = rank + jnp.where(beats, 1.0, 0.0)
        out = jnp.zeros(shape, F32)
        for r in range(MOBA_TOPK):
            idx = jnp.sum(jnp.where((rank == r) & (lane < nb), lane_f, 0.0), axis=1, keepdims=True)
            out = jnp.where(lane == r, idx, out)
        o_ref[s] = out.astype(jnp.int32)


def _topk_blocks(q4, kmean):
    bs, nb, nh, hd = kmean.shape
    assert nb <= LANES
    g = 4 if bs % 4 == 0 else 1
    return pl.pallas_call(
        _topk_kernel,
        grid=(bs // g,),
        in_specs=[pl.BlockSpec((g, nh, hd), lambda b: (b, 0, 0)),
                  pl.BlockSpec((g, nb, nh, hd), lambda b: (b, 0, 0, 0))],
        out_specs=pl.BlockSpec((g, nh, LANES), lambda b: (b, 0, 0)),
        out_shape=jax.ShapeDtypeStruct((bs, nh, LANES), jnp.int32),
        compiler_params=_cparams("parallel"),
        name="topk_blocks",
    )(q4, kmean)


def _decode_attn_kernel(pt_ref, ix_ref, q_ref, kn_ref, vn_ref, z_ref, ck_hbm, cv_hbm, o_ref,
                        kbuf, vbuf, sems, *, per_blk, scale):
    b = pl.program_id(0)
    nbat = pl.num_programs(0)
    nh = q_ref.shape[0]
    n_pages = MOBA_TOPK * per_blk

    def page_copies(bb, slot, h, p):
        r, half = divmod(p, per_blk)
        pid = pt_ref[bb, ix_ref[bb, h * MOBA_TOPK + r] * per_blk + half]
        return (pltpu.make_async_copy(ck_hbm.at[pid, :, h, :], kbuf.at[slot, h, p], sems.at[0, slot]),
                pltpu.make_async_copy(cv_hbm.at[pid, :, h, :], vbuf.at[slot, h, p], sems.at[1, slot]))

    def for_each_copy(bb, slot, fn):
        def per_head(h, carry):
            for p in range(n_pages):
                for cp in page_copies(bb, slot, h, p):
                    fn(cp)
            return carry
        lax.fori_loop(0, nh, per_head, 0)

    @pl.when(b == 0)
    def _():
        for_each_copy(0, 0, lambda cp: cp.start())

    slot = b % 2
    for_each_copy(b, slot, lambda cp: cp.wait())
    nxt = jnp.minimum(b + 1, nbat - 1)
    nslot = (b + 1) % 2

    def per_head(h, carry):
        for p in range(n_pages):
            for cp in page_copies(nxt, nslot, h, p):
                cp.start()
        row = pl.ds(h, 1)
        q = q_ref[row, :]
        scores = [jnp.sum(kbuf[slot, h, p] * q, axis=1, keepdims=True) * scale for p in range(n_pages)]
        s_own = jnp.sum(kn_ref[row, :] * q, axis=1, keepdims=True) * scale
        m = s_own
        for sc in scores:
            m = jnp.maximum(m, jnp.max(sc, axis=0, keepdims=True))
        e_own = jnp.exp(s_own - m)
        l = e_own
        acc = e_own * vn_ref[row, :]
        for p, sc in enumerate(scores):
            e = jnp.exp(sc - m)
            l = l + jnp.sum(e, axis=0, keepdims=True)
            acc = acc + jnp.sum(e * vbuf[slot, h, p], axis=0, keepdims=True)
        o_ref[row, :] = acc / l * _silu(z_ref[row, :])
        return carry

    lax.fori_loop(0, nh, per_head, 0, unroll=4 if nh % 4 == 0 else 1)

    @pl.when(b == nbat - 1)
    def _():
        for_each_copy(nxt, nslot, lambda cp: cp.wait())


def _decode_attn(cache_k4, cache_v4, page_ids, idx, q4, kn4, vn4, z4):
    _, page, nh, hd = cache_k4.shape
    bs = page_ids.shape[0]
    per_blk = MOBA_BLOCK // page
    n_pages = MOBA_TOPK * per_blk
    vec = pl.BlockSpec((None, nh, hd), lambda b, pt, ix: (b, 0, 0))
    hbm = pl.BlockSpec(memory_space=pl.ANY)
    return pl.pallas_call(
        functools.partial(_decode_attn_kernel, per_blk=per_blk, scale=hd ** -0.5),
        grid_spec=pltpu.PrefetchScalarGridSpec(
            num_scalar_prefetch=2,
            grid=(bs,),
            in_specs=[vec, vec, vec, vec, hbm, hbm],
            out_specs=vec,
            scratch_shapes=[pltpu.VMEM((2, nh, n_pages, page, hd), F32),
                            pltpu.VMEM((2, nh, n_pages, page, hd), F32),
                            pltpu.SemaphoreType.DMA((2, 2))]),
        out_shape=jax.ShapeDtypeStruct((bs, nh, hd), F32),
        compiler_params=_cparams("arbitrary"),
        name="decode_attn",
    )(page_ids, idx, q4, kn4, vn4, z4, cache_k4, cache_v4)


def _rope_tables(pos):
    half = B_HEAD_DIM // 2
    freqs = ROPE_THETA ** (-jnp.arange(half, dtype=F32) / half)
    ang = pos.astype(F32)[:, None] * freqs[None, :]
    cos = jnp.cos(ang)
    sin = jnp.sin(ang)
    return jnp.concatenate([cos, cos], -1), jnp.concatenate([-sin, sin], -1)


def _tile(n, pref):
    return pref if n % pref == 0 else n


def _prep_weights(w_in_all, layer, b_in, w_proj_a, w_proj_b, w_out, d):
    c0 = 5 * d
    c1 = c0 + 2 * A_HEADS
    w_in = w_in_all[layer]
    w_a = w_in.astype(BF16)
    w_b = w_a[:, c1:]
    w_g = w_in[:, c0:c1]
    wc = jnp.pad(w_g, ((0, 0), (0, LANES - 2 * A_HEADS))).astype(BF16)
    wr = w_g.T.astype(BF16)
    bc = jnp.pad(b_in[c0:c1], (0, LANES - 2 * A_HEADS))[None, :]
    br = b_in[c0:c1][:, None]
    return ((w_a, b_in[None, :c0]), (w_b, b_in[None, c1:]),
            wc, wr, bc, br, w_proj_a.astype(BF16), w_proj_b.astype(BF16), w_out.astype(BF16))


_QA, _KA, _VA, _OA, _ZA = range(5)
_QB, _KB, _VB, _ZB, _GA, _GB = range(6)


def _project(xb, part, tm):
    w, b = part
    return _inproj(xb, w, b, tm, _tile(b.shape[1], 1024))


def _project_with_means(xb, parts, tm, cache_k4, page_ids, first_block):
    bs, n_pages = page_ids.shape
    n_blocks = bs * n_pages * cache_k4.shape[1] // MOBA_BLOCK
    tiles = [(tm, _tile(b.shape[1], 2048)) for _, b in parts]
    steps = [(xb.shape[0] // tm_) * (b.shape[1] // tn_) for (tm_, tn_), (_, b) in zip(tiles, parts)]
    bps = max(1, -(-(n_blocks - first_block) // sum(steps)))
    hs, means, first = [], [], first_block
    for (w, b), (tm_, tn_), n_steps in zip(parts, tiles, steps):
        h, km = _inproj_means(xb, w, b, tm_, tn_, page_ids, cache_k4, first, bps)
        hs.append(h)
        means.append(km)
        first += n_steps * bps
    return hs[0], hs[1], means


def _layer_prompt(x, wts, ng, ln_g, ln_b, alpha, cache_k4, page_ids):
    bsz, t, d = x.shape
    m = bsz * t
    part_a, part_b, wc, wr, bc, br, wa, wb, wo = wts
    x2 = x.reshape(m, d)
    first_mlstm = _moba_steps(bsz, t, d)[1] * MEANS_PER_MOBA_STEP
    first_proj = first_mlstm + _mlstm_steps(bsz, t)[1] * MEANS_PER_MLSTM_STEP
    xb, gc, gr = _gates(x2, wc, wr, bc, br, _tile(m, 512))
    h_a, h_b, means_proj = _project_with_means(xb, (part_a, part_b), _tile(m, 512), cache_k4, page_ids, first_proj)
    ya, c1, n1, m1, means_mlstm = _mlstm_prompt(h_a, gc, gr, ng, bsz, t, d, page_ids, cache_k4,
                                                first_mlstm, MEANS_PER_MLSTM_STEP)
    cos, sin = _rope_tables(jnp.arange(t, dtype=jnp.int32))
    cos = jnp.tile(cos, (bsz, 1))
    sin = jnp.tile(sin, (bsz, 1))
    q_t, k_new, k_b, v_new, v_t, kmean = _rope(h_b, _QB, _KB, _VB, d, cos, sin, B_HEAD_DIM ** -0.5 * LOG2E)
    nb = t // MOBA_BLOCK
    yb, means_moba = _moba_prompt(q_t, k_b, v_t, kmean.reshape(bsz, nb, d), h_b, _ZB, bsz, t, d,
                                  page_ids, cache_k4, 0, MEANS_PER_MOBA_STEP)
    bs_c, n_pages = page_ids.shape
    n_blocks = bs_c * n_pages * cache_k4.shape[1] // MOBA_BLOCK
    kmean_cache = jnp.concatenate([means_moba, means_mlstm] + means_proj)[:n_blocks]
    kmean_cache = kmean_cache.reshape((bs_c, n_blocks // bs_c) + cache_k4.shape[2:])
    mg = _merge(ya, yb, h_b, _GA, _GB, wa, wb, _tile(m, 1024), _tile(d, 512))
    y = _out(mg, wo, x2, ln_g, ln_b, _tile(m, 512), alpha)
    nh = d // B_HEAD_DIM
    return (kmean_cache, y.reshape(bsz, t, d), k_new.reshape(bsz, t, nh, B_HEAD_DIM),
            v_new.reshape(bsz, t, nh, B_HEAD_DIM), c1, n1.reshape(bsz, A_HEADS, d // A_HEADS), m1[:, :, 0, 0])


def _layer_sample(x, wts, ng, ln_g, ln_b, alpha, cache_k4, cache_v4, page_ids, kmean, c_all, c_row0, n0, m0):
    bs, t, d = x.shape
    part_a, part_b, wc, wr, bc, br, wa, wb, wo = wts
    _, page, nh, hd = cache_k4.shape
    past = page_ids.shape[1] * page
    da = d // A_HEADS
    x2 = x.reshape(bs, d)
    xb, gc, _ = _gates(x2, wc, wr, bc, br, bs)
    h_a = _project(xb, part_a, bs)
    h_b = _project(xb, part_b, bs)
    ya, c1, n1, m1 = _mlstm_step(h_a.reshape(bs, 1, h_a.shape[1]), gc.reshape(bs, 1, LANES), ng, c_all, c_row0,
                                 n0.reshape(bs, A_HEADS, 1, da),
                                 jnp.broadcast_to(m0[:, :, None, None], (bs, A_HEADS, 1, LANES)), d)
    cos, sin = _rope_tables(jnp.full((bs,), past, dtype=jnp.int32))
    q_r, k_new, v_new = _rope_rows(h_b, _QB, _KB, _VB, d, cos, sin)
    heads = lambda a: a.reshape(bs, nh, hd)
    q4, kn4, vn4 = heads(q_r), heads(k_new), heads(v_new)
    idx = _topk_blocks(q4, kmean)[:, :, :MOBA_TOPK].reshape(bs, nh * MOBA_TOPK)
    yb = _decode_attn(cache_k4, cache_v4, page_ids, idx, q4, kn4, vn4, heads(h_b[:, _ZB * d:(_ZB + 1) * d]))
    mg = _merge(ya.reshape(bs, d), yb.reshape(bs, d), h_b, _GA, _GB, wa, wb, bs, _tile(d, 512))
    y = _out(mg, wo, x2, ln_g, ln_b, bs, alpha)
    return (y.reshape(bs, 1, d), kn4[:, None], vn4[:, None],
            c1, n1.reshape(bs, A_HEADS, da), m1[:, :, 0, 0])


def kernel(x_prompt, x_sample, cache_k, cache_v, page_table, state_mlstm_C, state_mlstm_n, state_mlstm_m,
           w_in, b_in, mh_norm_g, w_proj_a, w_proj_b, w_out, ln_g, ln_b):
    depth = w_in.shape[0]
    d = x_prompt.shape[-1]
    n_pages = page_table.shape[1]
    page = cache_k.shape[2]
    assert x_sample.shape[1] == 1 and d % (A_HEADS * LANES) == 0
    assert x_prompt.shape[1] % MOBA_BLOCK == 0 and MOBA_BLOCK % page == 0
    assert (n_pages * page) % MOBA_BLOCK == 0 and n_pages * page // MOBA_BLOCK >= MOBA_TOPK
    alpha = (2 * depth) ** 0.25
    n_phys = cache_k.shape[1]
    bs = x_sample.shape[0]
    cache_k3 = cache_k.reshape((depth * n_phys,) + cache_k.shape[2:])
    cache_v3 = cache_v.reshape((depth * n_phys,) + cache_v.shape[2:])
    c_all = state_mlstm_C.reshape((depth * bs,) + state_mlstm_C.shape[2:])
    yp, ys = x_prompt, x_sample
    new_p = [[] for _ in range(5)]
    new_s = [[] for _ in range(5)]
    for l in range(depth):
        wts = _prep_weights(w_in, l, b_in[l], w_proj_a[l], w_proj_b[l], w_out[l], d)
        ng, lg, lb = mh_norm_g[l][None, :], ln_g[l][None, :], ln_b[l][None, :]
        page_ids = page_table + l * n_phys
        kmean, yp, *st_p = _layer_prompt(yp, wts, ng, lg, lb, alpha, cache_k3, page_ids)
        ys, *st_s = _layer_sample(ys, wts, ng, lg, lb, alpha, cache_k3, cache_v3, page_ids, kmean,
                                  c_all, l * bs, state_mlstm_n[l], state_mlstm_m[l])
        for i in range(5):
            new_p[i].append(st_p[i])
            new_s[i].append(st_s[i])
    stack = lambda ts: ts[0][None] if len(ts) == 1 else jnp.stack(ts)
    k_p, v_p, c_p, n_p, m_p = [stack(t) for t in new_p]
    k_s, v_s, c_s, n_s, m_s = [stack(t) for t in new_s]
    return (yp, ys, k_p, v_p, c_p, n_p, m_p, k_s, v_s, c_s, n_s, m_s)
```

```python
import functools
import math

import jax
import jax.numpy as jnp
from jax import lax
from jax.experimental import pallas as pl
from jax.experimental.pallas import tpu as pltpu

F32 = jnp.float32
BF16 = jnp.bfloat16

A_HEADS = 4
A_CHUNK = 128
B_HEAD_DIM = 128
MOBA_BLOCK = 256
MOBA_TOPK = 3
ROPE_THETA = 10000.0
LN_EPS = 1e-5
LOG2E = math.log2(math.e)
LANES = 128
GATE_ROWS = 8
MOBA_HEADS_PER_STEP = 8
MEANS_PER_MOBA_STEP = 4
MEANS_PER_MLSTM_STEP = 4
VMEM_LIMIT = 60 * 1024 * 1024

_NT = (((1,), (1,)), ((), ()))


def _cparams(*sem):
    return pltpu.CompilerParams(dimension_semantics=sem, vmem_limit_bytes=VMEM_LIMIT)


def _log_sigmoid(x):
    return jnp.minimum(x, 0.0) - jnp.log1p(jnp.exp(-jnp.abs(x)))


def _silu(x):
    return x * jax.nn.sigmoid(x)


def _layernorm(x):
    mu = jnp.mean(x, axis=-1, keepdims=True)
    xc = x - mu
    return xc * lax.rsqrt(jnp.mean(xc * xc, axis=-1, keepdims=True) + LN_EPS)


def _inproj_kernel(x_ref, w_ref, b_ref, o_ref):
    o_ref[...] = jnp.dot(x_ref[...], w_ref[...], preferred_element_type=F32) + b_ref[...]


def _inproj(xb, w, b, tm, tn):
    m, k = xb.shape
    n = b.shape[1]
    return pl.pallas_call(
        _inproj_kernel,
        grid=(n // tn, m // tm),
        in_specs=[pl.BlockSpec((tm, k), lambda j, i: (i, 0)),
                  pl.BlockSpec((k, tn), lambda j, i: (0, j)),
                  pl.BlockSpec((1, tn), lambda j, i: (0, j))],
        out_specs=pl.BlockSpec((tm, tn), lambda j, i: (i, j)),
        out_shape=jax.ShapeDtypeStruct((m, n), F32),
        compiler_params=_cparams("parallel", "parallel"),
        name="inproj",
    )(xb, w, b)


def _means_fetch(pt_ref, ck_hbm, pbuf, sems, step, n_steps, job):
    first_block, n_blocks, per_blk, n_pages = job
    bps = pbuf.shape[1] // per_blk

    def page_copy(s, slot, r, p):
        blk = jnp.minimum(first_block + s * bps + r, n_blocks - 1)
        g = blk * per_blk + p
        pid = pt_ref[g // n_pages, g % n_pages]
        return pltpu.make_async_copy(ck_hbm.at[pid], pbuf.at[slot, r * per_blk + p], sems.at[slot])

    def for_each_copy(s, slot, fn):
        for r in range(bps):
            for p in range(per_blk):
                fn(page_copy(s, slot, r, p))

    @pl.when(step == 0)
    def _():
        for_each_copy(0, 0, lambda cp: cp.start())

    @pl.when(step + 1 < n_steps)
    def _():
        for_each_copy(step + 1, (step + 1) % 2, lambda cp: cp.start())

    slot = step % 2
    for_each_copy(step, slot, lambda cp: cp.wait())
    return slot


def _means_reduce(pbuf, slot, km_ref, job):
    per_blk = job[2]
    rows = per_blk * pbuf.shape[2]
    for r in range(km_ref.shape[0]):
        tot = jnp.sum(pbuf[slot, r * per_blk], axis=0)
        for p in range(1, per_blk):
            tot = tot + jnp.sum(pbuf[slot, r * per_blk + p], axis=0)
        km_ref[r] = tot * (1.0 / rows)


def _means_job(page_ids, cache4, first_block):
    bs, n_pages = page_ids.shape
    per_blk = MOBA_BLOCK // cache4.shape[1]
    return (first_block, bs * n_pages // per_blk, per_blk, n_pages)


def _means_scratch(cache4, job, bps):
    _, page, nh, hd = cache4.shape
    return [pltpu.VMEM((2, bps * job[2], page, nh, hd), F32), pltpu.SemaphoreType.DMA((2,))]


def _inproj_means_kernel(pt_ref, x_ref, w_ref, b_ref, ck_hbm, o_ref, km_ref, pbuf, sems, *, job):
    ni = pl.num_programs(1)
    step = pl.program_id(0) * ni + pl.program_id(1)
    slot = _means_fetch(pt_ref, ck_hbm, pbuf, sems, step, pl.num_programs(0) * ni, job)
    o_ref[...] = jnp.dot(x_ref[...], w_ref[...], preferred_element_type=F32) + b_ref[...]
    _means_reduce(pbuf, slot, km_ref, job)


def _inproj_means(xb, w, b, tm, tn, page_ids, cache4, first_block, blocks_per_step):
    m, k = xb.shape
    n = b.shape[1]
    _, page, nh, hd = cache4.shape
    nj, ni = n // tn, m // tm
    bps = blocks_per_step
    job = _means_job(page_ids, cache4, first_block)
    return pl.pallas_call(
        functools.partial(_inproj_means_kernel, job=job),
        grid_spec=pltpu.PrefetchScalarGridSpec(
            num_scalar_prefetch=1,
            grid=(nj, ni),
            in_specs=[pl.BlockSpec((tm, k), lambda j, i, pt: (i, 0)),
                      pl.BlockSpec((k, tn), lambda j, i, pt: (0, j)),
                      pl.BlockSpec((1, tn), lambda j, i, pt: (0, j)),
                      pl.BlockSpec(memory_space=pl.ANY)],
            out_specs=[pl.BlockSpec((tm, tn), lambda j, i, pt: (i, j)),
                       pl.BlockSpec((bps, nh, hd), lambda j, i, pt: (j * ni + i, 0, 0))],
            scratch_shapes=_means_scratch(cache4, job, bps)),
        out_shape=[jax.ShapeDtypeStruct((m, n), F32),
                   jax.ShapeDtypeStruct((nj * ni * bps, nh, hd), F32)],
        compiler_params=_cparams("arbitrary", "arbitrary"),
        name="inproj_means",
    )(page_ids, xb, w, b, cache4)


def _gates_kernel(x_ref, wc_ref, wr_ref, bc_ref, br_ref, xb_ref, gc_ref, gr_ref):
    xb = x_ref[...].astype(BF16)
    xb_ref[...] = xb
    gc_ref[...] = jnp.dot(xb, wc_ref[...], preferred_element_type=F32) + bc_ref[...]
    gr_ref[...] = lax.dot_general(wr_ref[...], xb, _NT, preferred_element_type=F32) + br_ref[...]


def _gates(x, wc, wr, bc, br, tm):
    m, k = x.shape
    return pl.pallas_call(
        _gates_kernel,
        grid=(m // tm,),
        in_specs=[pl.BlockSpec((tm, k), lambda i: (i, 0)),
                  pl.BlockSpec((k, LANES), lambda i: (0, 0)),
                  pl.BlockSpec((GATE_ROWS, k), lambda i: (0, 0)),
                  pl.BlockSpec((1, LANES), lambda i: (0, 0)),
                  pl.BlockSpec((GATE_ROWS, 1), lambda i: (0, 0))],
        out_specs=[pl.BlockSpec((tm, k), lambda i: (i, 0)),
                   pl.BlockSpec((tm, LANES), lambda i: (i, 0)),
                   pl.BlockSpec((GATE_ROWS, tm), lambda i: (0, i))],
        out_shape=[jax.ShapeDtypeStruct((m, k), BF16),
                   jax.ShapeDtypeStruct((m, LANES), F32),
                   jax.ShapeDtypeStruct((GATE_ROWS, m), F32)],
        compiler_params=_cparams("parallel"),
        name="gates",
    )(x, wc, wr, bc, br)


def _rotate(x, cos, sin):
    return x * cos + pltpu.roll(x, B_HEAD_DIM // 2, 1) * sin


def _rope_kernel(q_ref, k_ref, v_ref, cos_ref, sin_ref, qt_ref, ko_ref, kb_ref, vo_ref, vt_ref, km_ref, *, scale):
    cos = cos_ref[...]
    sin = sin_ref[...]
    tm, d = q_ref.shape
    for h in range(d // B_HEAD_DIM):
        sl = slice(h * B_HEAD_DIM, (h + 1) * B_HEAD_DIM)
        qr = _rotate(q_ref[:, sl], cos, sin)
        kr = _rotate(k_ref[:, sl], cos, sin)
        v = v_ref[:, sl]
        qt_ref[sl, :] = (qr * scale).T.astype(BF16)
        ko_ref[:, sl] = kr
        kb_ref[:, sl] = kr.astype(BF16)
        km_ref[:, sl] = jnp.sum(kr, axis=0, keepdims=True) * (1.0 / tm)
        vo_ref[:, sl] = v
        vt_ref[sl, :] = v.T.astype(BF16)


def _rope(h, col_q, col_k, col_v, d, cos, sin, scale):
    m = h.shape[0]
    tm = MOBA_BLOCK
    nt = m // tm
    row = lambda c: pl.BlockSpec((tm, d), lambda i: (i, c))
    full = pl.BlockSpec((tm, d), lambda i: (i, 0))
    tab = pl.BlockSpec((tm, B_HEAD_DIM), lambda i: (i, 0))
    tr = pl.BlockSpec((None, d, tm), lambda i: (i, 0, 0))
    return pl.pallas_call(
        functools.partial(_rope_kernel, scale=scale),
        grid=(nt,),
        in_specs=[row(col_q), row(col_k), row(col_v), tab, tab],
        out_specs=[tr, full, full, full, tr, pl.BlockSpec((None, 1, d), lambda i: (i, 0, 0))],
        out_shape=[jax.ShapeDtypeStruct((nt, d, tm), BF16),
                   jax.ShapeDtypeStruct((m, d), F32),
                   jax.ShapeDtypeStruct((m, d), BF16),
                   jax.ShapeDtypeStruct((m, d), F32),
                   jax.ShapeDtypeStruct((nt, d, tm), BF16),
                   jax.ShapeDtypeStruct((nt, 1, d), F32)],
        compiler_params=_cparams("parallel"),
        name="rope",
    )(h, h, h, cos, sin)


def _rope_rows_kernel(q_ref, k_ref, v_ref, cos_ref, sin_ref, qo_ref, ko_ref, vo_ref):
    cos = cos_ref[...]
    sin = sin_ref[...]
    for h in range(q_ref.shape[1] // B_HEAD_DIM):
        sl = slice(h * B_HEAD_DIM, (h + 1) * B_HEAD_DIM)
        qo_ref[:, sl] = _rotate(q_ref[:, sl], cos, sin)
        ko_ref[:, sl] = _rotate(k_ref[:, sl], cos, sin)
    vo_ref[...] = v_ref[...]


def _rope_rows(h, col_q, col_k, col_v, d, cos, sin):
    m = h.shape[0]
    row = lambda c: pl.BlockSpec((m, d), lambda i: (0, c))
    full = pl.BlockSpec((m, d), lambda i: (0, 0))
    tab = pl.BlockSpec((m, B_HEAD_DIM), lambda i: (0, 0))
    return pl.pallas_call(
        _rope_rows_kernel,
        grid=(1,),
        in_specs=[row(col_q), row(col_k), row(col_v), tab, tab],
        out_specs=[full, full, full],
        out_shape=[jax.ShapeDtypeStruct((m, d), F32)] * 3,
        compiler_params=_cparams("arbitrary"),
        name="rope_rows",
    )(h, h, h, cos, sin)


def _mlstm_kernel(pt_ref, q_ref, k_ref, v_ref, oa_ref, za_ref, gc_ref, gr_ref, ng_ref, ck_hbm,
                  ya_ref, c_out, n_out, m_out, cm_ref, c_sc, n_sc, m_sc, pbuf, sems, *, scale, job):
    c = pl.program_id(1)
    nc = pl.num_programs(1)
    slot = _means_fetch(pt_ref, ck_hbm, pbuf, sems, pl.program_id(0) * nc + c, pl.num_programs(0) * nc, job)
    _means_reduce(pbuf, slot, cm_ref, job)
    L = q_ref.shape[0]
    H = A_HEADS
    D = q_ref.shape[1] // H
    heads = range(H)
    hsl = [slice(h * D, (h + 1) * D) for h in heads]

    @pl.when(c == 0)
    def _():
        c_sc[...] = jnp.zeros(c_sc.shape, F32)
        n_sc[...] = jnp.zeros(n_sc.shape, F32)
        m_sc[...] = jnp.zeros(m_sc.shape, F32)

    gc = gc_ref[...]
    gr = gr_ref[...]
    t_i = lax.broadcasted_iota(jnp.int32, (L, L), 0)
    s_i = lax.broadcasted_iota(jnp.int32, (L, L), 1)
    causal = s_i <= t_i
    def head_gates(h):
        ig_col, fa_col = gc[:, h:h + 1], gc[:, h + H:h + H + 1]
        ig_row, fa_row = gr[h:h + 1, :], gr[h + H:h + H + 1, :]
        lf_col = _log_sigmoid(fa_col)
        lf_row = _log_sigmoid(fa_row)
        b_col = jnp.sum(jnp.where(causal, jnp.broadcast_to(lf_row, (L, L)), 0.0), axis=1, keepdims=True)
        b_row = jnp.sum(jnp.where(t_i <= s_i, jnp.broadcast_to(lf_col, (L, L)), 0.0), axis=0, keepdims=True)
        a_row = ig_row - b_row
        a_col = ig_col - b_col
        cm_col = jnp.max(jnp.where(causal, jnp.broadcast_to(a_row, (L, L)), -jnp.inf), axis=1, keepdims=True)
        m_prev = m_sc[h]
        g_col = jnp.maximum(m_prev, cm_col)
        g_last = jnp.maximum(m_prev, jnp.max(a_row, axis=1, keepdims=True))
        return dict(
            dmat=jnp.exp(jnp.where(causal, a_row - g_col, -jnp.inf)),
            inter=jnp.exp(m_prev - g_col),
            inv_floor=jnp.exp(-(b_col + g_col)),
            m_last=jnp.sum(lf_row, axis=1, keepdims=True) + g_last,
            w_col=jnp.exp(a_col - g_last),
            decay=jnp.exp(m_prev - g_last))

    def first_matmuls(h):
        q = q_ref[:, hsl[h]]
        ks = k_ref[:, hsl[h]] * scale
        qb = q.astype(BF16)
        s_raw = lax.dot_general(qb, ks.astype(BF16), _NT, preferred_element_type=F32)
        cq = lax.dot_general(qb, c_sc[h].astype(BF16), _NT, preferred_element_type=F32)
        return head_gates(h), q, ks, s_raw, cq

    def rest(h, g, q, ks, s_raw, cq):
        v = v_ref[:, hsl[h]]
        s = s_raw * g["dmat"]
        num = g["inter"] * cq + jnp.dot(s.astype(BF16), v.astype(BF16), preferred_element_type=F32)
        nvec = n_sc[h]
        den = g["inter"] * jnp.sum(q * nvec, axis=1, keepdims=True) + jnp.sum(s, axis=1, keepdims=True)
        hh = num / jnp.maximum(jnp.abs(den), g["inv_floor"])
        ya = (_layernorm(hh) * ng_ref[:, hsl[h]] * jax.nn.sigmoid(oa_ref[:, hsl[h]])
              * _silu(za_ref[:, hsl[h]]))
        ya_ref[:, hsl[h]] = ya.astype(ya_ref.dtype)
        kw = ks * g["w_col"]
        c_sc[h] = g["decay"] * c_sc[h] + jnp.dot(v.T.astype(BF16), kw.astype(BF16), preferred_element_type=F32)
        n_sc[h] = g["decay"] * nvec + jnp.sum(kw, axis=0, keepdims=True)
        m_sc[h] = g["m_last"]

    ahead = first_matmuls(0)
    for h in heads:
        cur = ahead
        if h + 1 < H:
            ahead = first_matmuls(h + 1)
        rest(h, *cur)

    @pl.when(c == nc - 1)
    def _():
        c_out[...] = c_sc[...]
        n_out[...] = n_sc[...]
        for h in heads:
            m_out[h] = jnp.broadcast_to(m_sc[h], m_out.shape[1:])


def _mlstm_steps(bsz, t):
    L = t if t <= A_CHUNK else math.gcd(t, A_CHUNK)
    return L, bsz * (t // L)


def _mlstm_prompt(h, gc, gr, ng, bsz, t, d, page_ids, cache4, first_block, blocks_per_step):
    da = d // A_HEADS
    L, n_steps = _mlstm_steps(bsz, t)
    nc = t // L
    H = A_HEADS
    bps = blocks_per_step
    job = _means_job(page_ids, cache4, first_block)
    blk = lambda g: pl.BlockSpec((L, d), lambda b, c, pt: (b * nc + c, g))
    return pl.pallas_call(
        functools.partial(_mlstm_kernel, scale=da ** -0.5, job=job),
        grid_spec=pltpu.PrefetchScalarGridSpec(
            num_scalar_prefetch=1,
            grid=(bsz, nc),
            in_specs=[blk(0), blk(1), blk(2), blk(3), blk(4),
                      pl.BlockSpec((L, LANES), lambda b, c, pt: (b * nc + c, 0)),
                      pl.BlockSpec((GATE_ROWS, L), lambda b, c, pt: (0, b * nc + c)),
                      pl.BlockSpec((1, d), lambda b, c, pt: (0, 0)),
                      pl.BlockSpec(memory_space=pl.ANY)],
            out_specs=[pl.BlockSpec((L, d), lambda b, c, pt: (b * nc + c, 0)),
                       pl.BlockSpec((None, H, da, da), lambda b, c, pt: (b, 0, 0, 0)),
                       pl.BlockSpec((None, H, 1, da), lambda b, c, pt: (b, 0, 0, 0)),
                       pl.BlockSpec((None, H, 1, LANES), lambda b, c, pt: (b, 0, 0, 0)),
                       pl.BlockSpec((bps,) + cache4.shape[2:], lambda b, c, pt: (b * nc + c, 0, 0))],
            scratch_shapes=([pltpu.VMEM((H, da, da), F32), pltpu.VMEM((H, 1, da), F32), pltpu.VMEM((H, 1, 1), F32)]
                            + _means_scratch(cache4, job, bps))),
        out_shape=[jax.ShapeDtypeStruct((bsz * t, d), BF16),
                   jax.ShapeDtypeStruct((bsz, H, da, da), F32),
                   jax.ShapeDtypeStruct((bsz, H, 1, da), F32),
                   jax.ShapeDtypeStruct((bsz, H, 1, LANES), F32),
                   jax.ShapeDtypeStruct((n_steps * bps,) + cache4.shape[2:], F32)],
        compiler_params=_cparams("arbitrary", "arbitrary"),
        name="mlstm_prompt",
    )(page_ids, h, h, h, h, h, gc, gr, ng, cache4)


def _moba_kernel(pt_ref, qt_ref, k_ref, vt_ref, km_ref, z_ref, ck_hbm, o_ref, cm_ref,
                 mem_sc, acc_sc, pbuf, sems, *, job):
    i = pl.program_id(2)
    step = (pl.program_id(0) * pl.num_programs(1) + pl.program_id(1)) * pl.num_programs(2) + i
    n_steps = pl.num_programs(0) * pl.num_programs(1) * pl.num_programs(2)
    slot = _means_fetch(pt_ref, ck_hbm, pbuf, sems, step, n_steps, job)
    _means_reduce(pbuf, slot, cm_ref, job)

    blk = qt_ref.shape[1]
    nb = km_ref.shape[0]
    hd = B_HEAD_DIM
    heads = range(qt_ref.shape[0] // hd)
    hsl = [slice(h * hd, (h + 1) * hd) for h in heads]
    qts = [qt_ref[sl, :] for sl in hsl]

    row = lax.broadcasted_iota(jnp.int32, (nb, blk), 0)
    past = row < i
    key_i = lax.broadcasted_iota(jnp.int32, (blk, blk), 0)
    qry_i = lax.broadcasted_iota(jnp.int32, (blk, blk), 1)
    own = pl.ds(pl.multiple_of(i * blk, blk), blk)
    gates = [jnp.dot(km_ref[:, hsl[h]].astype(BF16), qts[h], preferred_element_type=F32) for h in heads]
    s_own = [jnp.dot(k_ref[own, hsl[h]], qts[h], preferred_element_type=F32) for h in heads]
    for h in heads:
        g = jnp.where(past, gates[h], -jnp.inf)
        rank = jnp.zeros(g.shape, F32)
        for n2 in range(nb):
            g2 = g[n2:n2 + 1, :]
            beats = (g2 > g) | ((g2 == g) & (n2 < row))
            rank = rank + jnp.where(beats, 1.0, 0.0)
        mem_sc[h] = jnp.where(past & (rank < MOBA_TOPK), 1.0, 0.0)
    m0, l0, p_own = [], [], []
    for h in heads:
        s = jnp.where(key_i <= qry_i, s_own[h], -jnp.inf)
        m = jnp.max(s, axis=0, keepdims=True)
        p = jnp.exp2(s - m)
        m0.append(m)
        l0.append(jnp.sum(p, axis=0, keepdims=True))
        p_own.append(p.astype(BF16))
    for h in heads:
        acc_sc[h] = jnp.dot(vt_ref[i, hsl[h], :], p_own[h], preferred_element_type=F32)

    def body(jj, carry):
        ms, ls = carry
        j0 = 2 * jj
        rows0 = pl.ds(pl.multiple_of(j0 * blk, blk), blk)
        rows1 = pl.ds(pl.multiple_of(j0 * blk + blk, blk), blk)
        new_m, new_l = [], []
        ss = []
        for h in heads:
            s0 = jnp.dot(k_ref[rows0, hsl[h]], qts[h], preferred_element_type=F32)
            s1 = jnp.dot(k_ref[rows1, hsl[h]], qts[h], preferred_element_type=F32)
            s0 = jnp.where(mem_sc[h, pl.ds(j0, 1), :] > 0.0, s0, -jnp.inf)
            s1 = jnp.where(mem_sc[h, pl.ds(j0 + 1, 1), :] > 0.0, s1, -jnp.inf)
            ss.append((s0, s1))
        ps = []
        for h in heads:
            s0, s1 = ss[h]
            m_new = jnp.maximum(ms[h], jnp.maximum(jnp.max(s0, axis=0, keepdims=True),
                                                   jnp.max(s1, axis=0, keepdims=True)))
            alpha = jnp.exp2(ms[h] - m_new)
            p0 = jnp.exp2(s0 - m_new)
            p1 = jnp.exp2(s1 - m_new)
            new_l.append(alpha * ls[h] + jnp.sum(p0, axis=0, keepdims=True) + jnp.sum(p1, axis=0, keepdims=True))
            new_m.append(m_new)
            ps.append((alpha, p0.astype(BF16), p1.astype(BF16)))
        for h in heads:
            alpha, p0, p1 = ps[h]
            acc_sc[h] = (alpha * acc_sc[h]
                         + jnp.dot(vt_ref[j0, hsl[h], :], p0, preferred_element_type=F32)
                         + jnp.dot(vt_ref[j0 + 1, hsl[h], :], p1, preferred_element_type=F32))
        return tuple(new_m), tuple(new_l)

    _, l_fin = lax.fori_loop(0, (i + 1) // 2, body, (tuple(m0), tuple(l0)))
    outs = [(acc_sc[h] / l_fin[h]).T for h in heads]
    for h in heads:
        o_ref[:, hsl[h]] = (outs[h] * _silu(z_ref[:, hsl[h]])).astype(o_ref.dtype)


def _moba_steps(bsz, t, d):
    nh = d // B_HEAD_DIM
    hps = MOBA_HEADS_PER_STEP if nh % MOBA_HEADS_PER_STEP == 0 else 1
    return hps, bsz * (nh // hps) * (t // MOBA_BLOCK)


def _moba_prompt(qt, kb, vt, kmean, h, col_z, bsz, t, d, page_ids, cache4, first_block, blocks_per_step):
    nh = d // B_HEAD_DIM
    nb = t // MOBA_BLOCK
    blk = MOBA_BLOCK
    hps, n_steps = _moba_steps(bsz, t, d)
    w = hps * B_HEAD_DIM
    ng = nh // hps
    bps = blocks_per_step
    job = _means_job(page_ids, cache4, first_block)
    return pl.pallas_call(
        functools.partial(_moba_kernel, job=job),
        grid_spec=pltpu.PrefetchScalarGridSpec(
            num_scalar_prefetch=1,
            grid=(bsz, ng, nb),
            in_specs=[pl.BlockSpec((None, w, blk), lambda b, g, i, pt: (b * nb + i, g, 0)),
                      pl.BlockSpec((t, w), lambda b, g, i, pt: (b, g)),
                      pl.BlockSpec((nb, w, blk), lambda b, g, i, pt: (b, g, 0)),
                      pl.BlockSpec((None, nb, w), lambda b, g, i, pt: (b, 0, g)),
                      pl.BlockSpec((blk, w), lambda b, g, i, pt: (b * nb + i, col_z * ng + g)),
                      pl.BlockSpec(memory_space=pl.ANY)],
            out_specs=[pl.BlockSpec((blk, w), lambda b, g, i, pt: (b * nb + i, g)),
                       pl.BlockSpec((bps, nh, B_HEAD_DIM), lambda b, g, i, pt: ((b * ng + g) * nb + i, 0, 0))],
            scratch_shapes=([pltpu.VMEM((hps, nb, blk), F32), pltpu.VMEM((hps, B_HEAD_DIM, blk), F32)]
                            + _means_scratch(cache4, job, bps))),
        out_shape=[jax.ShapeDtypeStruct((bsz * t, d), BF16),
                   jax.ShapeDtypeStruct((n_steps * bps, nh, B_HEAD_DIM), F32)],
        compiler_params=_cparams("arbitrary", "arbitrary", "arbitrary"),
        name="moba_prompt",
    )(page_ids, qt, kb, vt, kmean, h, cache4)


def _merge_kernel(ya_ref, yb_ref, ga_ref, gb_ref, wa_ref, wb_ref, o_ref):
    pa = jnp.dot(ya_ref[...].astype(BF16), wa_ref[...], preferred_element_type=F32)
    pb = jnp.dot(yb_ref[...].astype(BF16), wb_ref[...], preferred_element_type=F32)
    o_ref[...] = (jax.nn.sigmoid(ga_ref[...]) * pa + jax.nn.sigmoid(gb_ref[...]) * pb).astype(o_ref.dtype)


def _merge(ya, yb, h, col_ga, col_gb, wa, wb, tm, tn):
    m, d = ya.shape
    nj = d // tn
    return pl.pallas_call(
        _merge_kernel,
        grid=(m // tm, nj),
        in_specs=[pl.BlockSpec((tm, d), lambda i, j: (i, 0)),
                  pl.BlockSpec((tm, d), lambda i, j: (i, 0)),
                  pl.BlockSpec((tm, tn), lambda i, j: (i, col_ga * nj + j)),
                  pl.BlockSpec((tm, tn), lambda i, j: (i, col_gb * nj + j)),
                  pl.BlockSpec((d, tn), lambda i, j: (0, j)),
                  pl.BlockSpec((d, tn), lambda i, j: (0, j))],
        out_specs=pl.BlockSpec((tm, tn), lambda i, j: (i, j)),
        out_shape=jax.ShapeDtypeStruct((m, d), BF16),
        compiler_params=_cparams("parallel", "arbitrary"),
        name="merge",
    )(ya, yb, h, h, wa, wb)


def _out_kernel(mg_ref, wo_ref, x_ref, g_ref, b_ref, y_ref, *, alpha):
    tm = mg_ref.shape[0]
    strip = tm // 4 if tm % 32 == 0 else tm
    rows = [slice(r, r + strip) for r in range(0, tm, strip)]
    outs = [jnp.dot(mg_ref[rs, :], wo_ref[...], preferred_element_type=F32) for rs in rows]
    for rs, out in zip(rows, outs):
        y_ref[rs, :] = _layernorm(alpha * x_ref[rs, :] + out) * g_ref[...] + b_ref[...]


def _out(mg, wo, x, g, b, tm, alpha):
    m, d = x.shape
    row = pl.BlockSpec((tm, d), lambda i: (i, 0))
    vec = pl.BlockSpec((1, d), lambda i: (0, 0))
    return pl.pallas_call(
        functools.partial(_out_kernel, alpha=alpha),
        grid=(m // tm,),
        in_specs=[row, pl.BlockSpec((d, d), lambda i: (0, 0)), row, vec, vec],
        out_specs=row,
        out_shape=jax.ShapeDtypeStruct((m, d), F32),
        compiler_params=_cparams("parallel"),
        name="outproj",
    )(mg, wo, x, g, b)


def _mlstm_step_kernel(q_ref, k_ref, v_ref, oa_ref, za_ref, gc_ref, ng_ref, c_ref, n_ref, m_ref,
                       ya_ref, c_out, n_out, m_out, *, scale):
    H = A_HEADS
    da = q_ref.shape[1] // H
    gc = gc_ref[...]
    eye = lax.broadcasted_iota(jnp.int32, (da, da), 0) == lax.broadcasted_iota(jnp.int32, (da, da), 1)
    for h in range(H):
        sl = slice(h * da, (h + 1) * da)
        ig = gc[:, h:h + 1]
        lf = _log_sigmoid(gc[:, h + H:h + H + 1])
        m_prev = m_ref[h][:, :1]
        a = ig - lf
        g = jnp.maximum(m_prev, a)
        m_t = lf + g
        inter = jnp.exp(m_prev - g)
        dm = jnp.exp(a - g)

        q = q_ref[:, sl]
        ks = k_ref[:, sl] * scale
        v = v_ref[:, sl]
        cmat = c_ref[h]
        nvec = n_ref[h]
        q8 = jnp.broadcast_to(q, (8, da)).astype(BF16)
        cq = lax.dot_general(q8, cmat.astype(BF16), _NT, preferred_element_type=F32)[0:1]
        s = jnp.sum(q * ks, axis=1, keepdims=True) * dm
        num = inter * cq + s * v
        den = inter * jnp.sum(nvec * q, axis=1, keepdims=True) + s
        hh = num / jnp.maximum(jnp.abs(den), jnp.exp(-m_t))

        v_col = jnp.sum(jnp.where(eye, jnp.broadcast_to(v, (da, da)), 0.0), axis=1, keepdims=True)
        c_out[h] = inter * cmat + (dm * v_col) * ks
        n_out[h] = inter * nvec + dm * ks
        m_out[h] = jnp.broadcast_to(m_t, m_out.shape[1:])
        ya = _layernorm(hh) * ng_ref[:, sl] * jax.nn.sigmoid(oa_ref[:, sl]) * _silu(za_ref[:, sl])
        ya_ref[:, sl] = ya.astype(ya_ref.dtype)


def _mlstm_step(h3, gc3, ng, c_all, c_row0, n0, m0, d):
    bs = h3.shape[0]
    H = A_HEADS
    da = d // H
    row = lambda g: pl.BlockSpec((None, 1, d), lambda b: (b, 0, g))
    st = lambda r, c: pl.BlockSpec((None, H, r, c), lambda b: (b, 0, 0, 0))
    return pl.pallas_call(
        functools.partial(_mlstm_step_kernel, scale=da ** -0.5),
        grid=(bs,),
        in_specs=[row(0), row(1), row(2), row(3), row(4),
                  pl.BlockSpec((None, 1, LANES), lambda b: (b, 0, 0)),
                  pl.BlockSpec((1, d), lambda b: (0, 0)),
                  pl.BlockSpec((None, H, da, da), lambda b: (c_row0 + b, 0, 0, 0)),
                  st(1, da), st(1, LANES)],
        out_specs=[pl.BlockSpec((None, 1, d), lambda b: (b, 0, 0)),
                   st(da, da), st(1, da), st(1, LANES)],
        out_shape=[jax.ShapeDtypeStruct((bs, 1, d), F32),
                   jax.ShapeDtypeStruct((bs, H, da, da), F32),
                   jax.ShapeDtypeStruct((bs, H, 1, da), F32),
                   jax.ShapeDtypeStruct((bs, H, 1, LANES), F32)],
        compiler_params=_cparams("parallel"),
        name="mlstm_step",
    )(h3, h3, h3, h3, h3, gc3, ng, c_all, n0, m0)


def _topk_kernel(q_ref, km_ref, o_ref):
    nb = km_ref.shape[1]
    shape = o_ref.shape[1:]
    lane = lax.broadcasted_iota(jnp.int32, shape, 1)
    lane_f = lane.astype(F32)
    for s in range(q_ref.shape[0]):
        q = q_ref[s]
        gate = jnp.full(shape, -jnp.inf, F32)
        for n in range(nb):
            gate = jnp.where(lane == n, jnp.sum(km_ref[s, n] * q, axis=1, keepdims=True), gate)
        rank = jnp.zeros(shape, F32)
        for n2 in range(nb):
            g2 = gate[:, n2:n2 + 1]
            beats = (g2 > gate) | ((g2 == gate) & (n2 < lane))
            rank = rank + jnp.where(beats, 1.0, 0.0)
        out = jnp.zeros(shape, F32)
        for r in range(MOBA_TOPK):
            idx = jnp.sum(jnp.where((rank == r) & (lane < nb), lane_f, 0.0), axis=1, keepdims=True)
            out = jnp.where(lane == r, idx, out)
        o_ref[s] = out.astype(jnp.int32)


def _topk_blocks(q4, kmean):
    bs, nb, nh, hd = kmean.shape
    assert nb <= LANES
    g = 4 if bs % 4 == 0 else 1
    return pl.pallas_call(
        _topk_kernel,
        grid=(bs // g,),
        in_specs=[pl.BlockSpec((g, nh, hd), lambda b: (b, 0, 0)),
                  pl.BlockSpec((g, nb, nh, hd), lambda b: (b, 0, 0, 0))],
        out_specs=pl.BlockSpec((g, nh, LANES), lambda b: (b, 0, 0)),
        out_shape=jax.ShapeDtypeStruct((bs, nh, LANES), jnp.int32),
        compiler_params=_cparams("parallel"),
        name="topk_blocks",
    )(q4, kmean)


def _decode_attn_kernel(pt_ref, ix_ref, q_ref, kn_ref, vn_ref, z_ref, ck_hbm, cv_hbm, o_ref,
                        kbuf, vbuf, sems, *, per_blk, scale):
    b = pl.program_id(0)
    nbat = pl.num_programs(0)
    nh = q_ref.shape[0]
    n_pages = MOBA_TOPK * per_blk

    def page_copies(bb, slot, h, p):
        r, half = divmod(p, per_blk)
        pid = pt_ref[bb, ix_ref[bb, h * MOBA_TOPK + r] * per_blk + half]
        return (pltpu.make_async_copy(ck_hbm.at[pid, :, h, :], kbuf.at[slot, h, p], sems.at[0, slot]),
                pltpu.make_async_copy(cv_hbm.at[pid, :, h, :], vbuf.at[slot, h, p], sems.at[1, slot]))

    def for_each_copy(bb, slot, fn):
        def per_head(h, carry):
            for p in range(n_pages):
                for cp in page_copies(bb, slot, h, p):
                    fn(cp)
            return carry
        lax.fori_loop(0, nh, per_head, 0)

    @pl.when(b == 0)
    def _():
        for_each_copy(0, 0, lambda cp: cp.start())

    @pl.when(b + 1 < nbat)
    def _():
        for_each_copy(b + 1, (b + 1) % 2, lambda cp: cp.start())

    slot = b % 2
    for_each_copy(b, slot, lambda cp: cp.wait())

    def per_head(h, carry):
        row = pl.ds(h, 1)
        q = q_ref[row, :]
        scores = [jnp.sum(kbuf[slot, h, p] * q, axis=1, keepdims=True) * scale for p in range(n_pages)]
        s_own = jnp.sum(kn_ref[row, :] * q, axis=1, keepdims=True) * scale
        m = s_own
        for sc in scores:
            m = jnp.maximum(m, jnp.max(sc, axis=0, keepdims=True))
        e_own = jnp.exp(s_own - m)
        l = e_own
        acc = e_own * vn_ref[row, :]
        for p, sc in enumerate(scores):
            e = jnp.exp(sc - m)
            l = l + jnp.sum(e, axis=0, keepdims=True)
            acc = acc + jnp.sum(e * vbuf[slot, h, p], axis=0, keepdims=True)
        o_ref[row, :] = acc / l * _silu(z_ref[row, :])
        return carry

    lax.fori_loop(0, nh, per_head, 0, unroll=4 if nh % 4 == 0 else 1)


def _decode_attn(cache_k4, cache_v4, page_ids, idx, q4, kn4, vn4, z4):
    _, page, nh, hd = cache_k4.shape
    bs = page_ids.shape[0]
    per_blk = MOBA_BLOCK // page
    n_pages = MOBA_TOPK * per_blk
    vec = pl.BlockSpec((None, nh, hd), lambda b, pt, ix: (b, 0, 0))
    hbm = pl.BlockSpec(memory_space=pl.ANY)
    return pl.pallas_call(
        functools.partial(_decode_attn_kernel, per_blk=per_blk, scale=hd ** -0.5),
        grid_spec=pltpu.PrefetchScalarGridSpec(
            num_scalar_prefetch=2,
            grid=(bs,),
            in_specs=[vec, vec, vec, vec, hbm, hbm],
            out_specs=vec,
            scratch_shapes=[pltpu.VMEM((2, nh, n_pages, page, hd), F32),
                            pltpu.VMEM((2, nh, n_pages, page, hd), F32),
                            pltpu.SemaphoreType.DMA((2, 2))]),
        out_shape=jax.ShapeDtypeStruct((bs, nh, hd), F32),
        compiler_params=_cparams("arbitrary"),
        name="decode_attn",
    )(page_ids, idx, q4, kn4, vn4, z4, cache_k4, cache_v4)


def _rope_tables(pos):
    half = B_HEAD_DIM // 2
    freqs = ROPE_THETA ** (-jnp.arange(half, dtype=F32) / half)
    ang = pos.astype(F32)[:, None] * freqs[None, :]
    cos = jnp.cos(ang)
    sin = jnp.sin(ang)
    return jnp.concatenate([cos, cos], -1), jnp.concatenate([-sin, sin], -1)


def _tile(n, pref):
    return pref if n % pref == 0 else n


def _prep_weights(w_in_all, layer, b_in, w_proj_a, w_proj_b, w_out, d):
    c0 = 5 * d
    c1 = c0 + 2 * A_HEADS
    w_in = w_in_all[layer]
    w_a = w_in.astype(BF16)
    w_b = w_a[:, c1:]
    w_g = w_in[:, c0:c1]
    wc = jnp.pad(w_g, ((0, 0), (0, LANES - 2 * A_HEADS))).astype(BF16)
    wr = w_g.T.astype(BF16)
    bc = jnp.pad(b_in[c0:c1], (0, LANES - 2 * A_HEADS))[None, :]
    br = b_in[c0:c1][:, None]
    return ((w_a, b_in[None, :c0]), (w_b, b_in[None, c1:]),
            wc, wr, bc, br, w_proj_a.astype(BF16), w_proj_b.astype(BF16), w_out.astype(BF16))


_QA, _KA, _VA, _OA, _ZA = range(5)
_QB, _KB, _VB, _ZB, _GA, _GB = range(6)


def _project(xb, part, tm):
    w, b = part
    return _inproj(xb, w, b, tm, _tile(b.shape[1], 1024))


def _project_with_means(xb, parts, tm, cache_k4, page_ids, first_block):
    bs, n_pages = page_ids.shape
    n_blocks = bs * n_pages * cache_k4.shape[1] // MOBA_BLOCK
    tiles = [(tm, _tile(b.shape[1], 2048)) for _, b in parts]
    steps = [(xb.shape[0] // tm_) * (b.shape[1] // tn_) for (tm_, tn_), (_, b) in zip(tiles, parts)]
    bps = max(1, -(-(n_blocks - first_block) // sum(steps)))
    hs, means, first = [], [], first_block
    for (w, b), (tm_, tn_), n_steps in zip(parts, tiles, steps):
        h, km = _inproj_means(xb, w, b, tm_, tn_, page_ids, cache_k4, first, bps)
        hs.append(h)
        means.append(km)
        first += n_steps * bps
    return hs[0], hs[1], means


def _layer_prompt(x, wts, ng, ln_g, ln_b, alpha, cache_k4, page_ids):
    bsz, t, d = x.shape
    m = bsz * t
    part_a, part_b, wc, wr, bc, br, wa, wb, wo = wts
    x2 = x.reshape(m, d)
    first_mlstm = _moba_steps(bsz, t, d)[1] * MEANS_PER_MOBA_STEP
    first_proj = first_mlstm + _mlstm_steps(bsz, t)[1] * MEANS_PER_MLSTM_STEP
    xb, gc, gr = _gates(x2, wc, wr, bc, br, _tile(m, 512))
    h_a, h_b, means_proj = _project_with_means(xb, (part_a, part_b), _tile(m, 512), cache_k4, page_ids, first_proj)
    ya, c1, n1, m1, means_mlstm = _mlstm_prompt(h_a, gc, gr, ng, bsz, t, d, page_ids, cache_k4,
                                                first_mlstm, MEANS_PER_MLSTM_STEP)
    cos, sin = _rope_tables(jnp.arange(t, dtype=jnp.int32))
    cos = jnp.tile(cos, (bsz, 1))
    sin = jnp.tile(sin, (bsz, 1))
    q_t, k_new, k_b, v_new, v_t, kmean = _rope(h_b, _QB, _KB, _VB, d, cos, sin, B_HEAD_DIM ** -0.5 * LOG2E)
    nb = t // MOBA_BLOCK
    yb, means_moba = _moba_prompt(q_t, k_b, v_t, kmean.reshape(bsz, nb, d), h_b, _ZB, bsz, t, d,
                                  page_ids, cache_k4, 0, MEANS_PER_MOBA_STEP)
    bs_c, n_pages = page_ids.shape
    n_blocks = bs_c * n_pages * cache_k4.shape[1] // MOBA_BLOCK
    kmean_cache = jnp.concatenate([means_moba, means_mlstm] + means_proj)[:n_blocks]
    kmean_cache = kmean_cache.reshape((bs_c, n_blocks // bs_c) + cache_k4.shape[2:])
    mg = _merge(ya, yb, h_b, _GA, _GB, wa, wb, _tile(m, 1024), _tile(d, 512))
    y = _out(mg, wo, x2, ln_g, ln_b, _tile(m, 512), alpha)
    nh = d // B_HEAD_DIM
    return (kmean_cache, y.reshape(bsz, t, d), k_new.reshape(bsz, t, nh, B_HEAD_DIM),
            v_new.reshape(bsz, t, nh, B_HEAD_DIM), c1, n1.reshape(bsz, A_HEADS, d // A_HEADS), m1[:, :, 0, 0])


def _layer_sample(x, wts, ng, ln_g, ln_b, alpha, cache_k4, cache_v4, page_ids, kmean, c_all, c_row0, n0, m0):
    bs, t, d = x.shape
    part_a, part_b, wc, wr, bc, br, wa, wb, wo = wts
    _, page, nh, hd = cache_k4.shape
    past = page_ids.shape[1] * page
    da = d // A_HEADS
    x2 = x.reshape(bs, d)
    xb, gc, _ = _gates(x2, wc, wr, bc, br, bs)
    h_a = _project(xb, part_a, bs)
    h_b = _project(xb, part_b, bs)
    ya, c1, n1, m1 = _mlstm_step(h_a.reshape(bs, 1, h_a.shape[1]), gc.reshape(bs, 1, LANES), ng, c_all, c_row0,
                                 n0.reshape(bs, A_HEADS, 1, da),
                                 jnp.broadcast_to(m0[:, :, None, None], (bs, A_HEADS, 1, LANES)), d)
    cos, sin = _rope_tables(jnp.full((bs,), past, dtype=jnp.int32))
    q_r, k_new, v_new = _rope_rows(h_b, _QB, _KB, _VB, d, cos, sin)
    heads = lambda a: a.reshape(bs, nh, hd)
    q4, kn4, vn4 = heads(q_r), heads(k_new), heads(v_new)
    idx = _topk_blocks(q4, kmean)[:, :, :MOBA_TOPK].reshape(bs, nh * MOBA_TOPK)
    yb = _decode_attn(cache_k4, cache_v4, page_ids, idx, q4, kn4, vn4, heads(h_b[:, _ZB * d:(_ZB + 1) * d]))
    mg = _merge(ya.reshape(bs, d), yb.reshape(bs, d), h_b, _GA, _GB, wa, wb, bs, _tile(d, 512))
    y = _out(mg, wo, x2, ln_g, ln_b, bs, alpha)
    return (y.reshape(bs, 1, d), kn4[:, None], vn4[:, None],
            c1, n1.reshape(bs, A_HEADS, da), m1[:, :, 0, 0])


def kernel(x_prompt, x_sample, cache_k, cache_v, page_table, state_mlstm_C, state_mlstm_n, state_mlstm_m,
           w_in, b_in, mh_norm_g, w_proj_a, w_proj_b, w_out, ln_g, ln_b):
    depth = w_in.shape[0]
    d = x_prompt.shape[-1]
    n_pages = page_table.shape[1]
    page = cache_k.shape[2]
    assert x_sample.shape[1] == 1 and d % (A_HEADS * LANES) == 0
    assert x_prompt.shape[1] % MOBA_BLOCK == 0 and MOBA_BLOCK % page == 0
    assert (n_pages * page) % MOBA_BLOCK == 0 and n_pages * page // MOBA_BLOCK >= MOBA_TOPK
    alpha = (2 * depth) ** 0.25
    n_phys = cache_k.shape[1]
    bs = x_sample.shape[0]
    cache_k3 = cache_k.reshape((depth * n_phys,) + cache_k.shape[2:])
    cache_v3 = cache_v.reshape((depth * n_phys,) + cache_v.shape[2:])
    c_all = state_mlstm_C.reshape((depth * bs,) + state_mlstm_C.shape[2:])
    yp, ys = x_prompt, x_sample
    new_p = [[] for _ in range(5)]
    new_s = [[] for _ in range(5)]
    for l in range(depth):
        wts = _prep_weights(w_in, l, b_in[l], w_proj_a[l], w_proj_b[l], w_out[l], d)
        ng, lg, lb = mh_norm_g[l][None, :], ln_g[l][None, :], ln_b[l][None, :]
        page_ids = page_table + l * n_phys
        kmean, yp, *st_p = _layer_prompt(yp, wts, ng, lg, lb, alpha, cache_k3, page_ids)
        ys, *st_s = _layer_sample(ys, wts, ng, lg, lb, alpha, cache_k3, cache_v3, page_ids, kmean,
                                  c_all, l * bs, state_mlstm_n[l], state_mlstm_m[l])
        for i in range(5):
            new_p[i].append(st_p[i])
            new_s[i].append(st_s[i])
    stack = lambda ts: ts[0][None] if len(ts) == 1 else jnp.stack(ts)
    k_p, v_p, c_p, n_p, m_p = [stack(t) for t in new_p]
    k_s, v_s, c_s, n_s, m_s = [stack(t) for t in new_s]
    return (yp, ys, k_p, v_p, c_p, n_p, m_p, k_s, v_s, c_s, n_s, m_s)
```

```python
import functools
import math

import jax
import jax.numpy as jnp
from jax import lax
from jax.experimental import pallas as pl
from jax.experimental.pallas import tpu as pltpu

F32 = jnp.float32
BF16 = jnp.bfloat16

A_HEADS = 4
A_CHUNK = 128
B_HEAD_DIM = 128
MOBA_BLOCK = 256
MOBA_TOPK = 3
ROPE_THETA = 10000.0
LN_EPS = 1e-5
LOG2E = math.log2(math.e)
LANES = 128
GATE_ROWS = 8
MOBA_HEADS_PER_STEP = 8
MEANS_PER_MOBA_STEP = 4
MEANS_PER_MLSTM_STEP = 4
VMEM_LIMIT = 60 * 1024 * 1024

_NT = (((1,), (1,)), ((), ()))


def _cparams(*sem):
    return pltpu.CompilerParams(dimension_semantics=sem, vmem_limit_bytes=VMEM_LIMIT)


def _log_sigmoid(x):
    return jnp.minimum(x, 0.0) - jnp.log1p(jnp.exp(-jnp.abs(x)))


def _silu(x):
    return x * jax.nn.sigmoid(x)


def _layernorm(x):
    mu = jnp.mean(x, axis=-1, keepdims=True)
    xc = x - mu
    return xc * lax.rsqrt(jnp.mean(xc * xc, axis=-1, keepdims=True) + LN_EPS)


def _inproj_kernel(x_ref, w_ref, b_ref, o_ref):
    o_ref[...] = jnp.dot(x_ref[...], w_ref[...], preferred_element_type=F32) + b_ref[...]


def _inproj(xb, w, b, tm, tn):
    m, k = xb.shape
    n = b.shape[1]
    return pl.pallas_call(
        _inproj_kernel,
        grid=(n // tn, m // tm),
        in_specs=[pl.BlockSpec((tm, k), lambda j, i: (i, 0)),
                  pl.BlockSpec((k, tn), lambda j, i: (0, j)),
                  pl.BlockSpec((1, tn), lambda j, i: (0, j))],
        out_specs=pl.BlockSpec((tm, tn), lambda j, i: (i, j)),
        out_shape=jax.ShapeDtypeStruct((m, n), F32),
        compiler_params=_cparams("parallel", "parallel"),
        name="inproj",
    )(xb, w, b)


def _means_fetch(pt_ref, ck_hbm, pbuf, sems, step, n_steps, job):
    first_block, n_blocks, per_blk, n_pages = job
    bps = pbuf.shape[1] // per_blk

    def page_copy(s, slot, r, p):
        blk = jnp.minimum(first_block + s * bps + r, n_blocks - 1)
        g = blk * per_blk + p
        pid = pt_ref[g // n_pages, g % n_pages]
        return pltpu.make_async_copy(ck_hbm.at[pid], pbuf.at[slot, r * per_blk + p], sems.at[slot])

    def for_each_copy(s, slot, fn):
        for r in range(bps):
            for p in range(per_blk):
                fn(page_copy(s, slot, r, p))

    @pl.when(step == 0)
    def _():
        for_each_copy(0, 0, lambda cp: cp.start())

    @pl.when(step + 1 < n_steps)
    def _():
        for_each_copy(step + 1, (step + 1) % 2, lambda cp: cp.start())

    slot = step % 2
    for_each_copy(step, slot, lambda cp: cp.wait())
    return slot


def _means_reduce(pbuf, slot, km_ref, job):
    per_blk = job[2]
    rows = per_blk * pbuf.shape[2]
    for r in range(km_ref.shape[0]):
        tot = jnp.sum(pbuf[slot, r * per_blk], axis=0)
        for p in range(1, per_blk):
            tot = tot + jnp.sum(pbuf[slot, r * per_blk + p], axis=0)
        km_ref[r] = tot * (1.0 / rows)


def _means_job(page_ids, cache4, first_block):
    bs, n_pages = page_ids.shape
    per_blk = MOBA_BLOCK // cache4.shape[1]
    return (first_block, bs * n_pages // per_blk, per_blk, n_pages)


def _means_scratch(cache4, job, bps):
    _, page, nh, hd = cache4.shape
    return [pltpu.VMEM((2, bps * job[2], page, nh, hd), F32), pltpu.SemaphoreType.DMA((2,))]


def _inproj_means_kernel(pt_ref, x_ref, w_ref, b_ref, ck_hbm, o_ref, km_ref, pbuf, sems, *, job):
    ni = pl.num_programs(1)
    step = pl.program_id(0) * ni + pl.program_id(1)
    slot = _means_fetch(pt_ref, ck_hbm, pbuf, sems, step, pl.num_programs(0) * ni, job)
    o_ref[...] = jnp.dot(x_ref[...], w_ref[...], preferred_element_type=F32) + b_ref[...]
    _means_reduce(pbuf, slot, km_ref, job)


def _inproj_means(xb, w, b, tm, tn, page_ids, cache4, first_block, blocks_per_step):
    m, k = xb.shape
    n = b.shape[1]
    _, page, nh, hd = cache4.shape
    nj, ni = n // tn, m // tm
    bps = blocks_per_step
    job = _means_job(page_ids, cache4, first_block)
    return pl.pallas_call(
        functools.partial(_inproj_means_kernel, job=job),
        grid_spec=pltpu.PrefetchScalarGridSpec(
            num_scalar_prefetch=1,
            grid=(nj, ni),
            in_specs=[pl.BlockSpec((tm, k), lambda j, i, pt: (i, 0)),
                      pl.BlockSpec((k, tn), lambda j, i, pt: (0, j)),
                      pl.BlockSpec((1, tn), lambda j, i, pt: (0, j)),
                      pl.BlockSpec(memory_space=pl.ANY)],
            out_specs=[pl.BlockSpec((tm, tn), lambda j, i, pt: (i, j)),
                       pl.BlockSpec((bps, nh, hd), lambda j, i, pt: (j * ni + i, 0, 0))],
            scratch_shapes=_means_scratch(cache4, job, bps)),
        out_shape=[jax.ShapeDtypeStruct((m, n), F32),
                   jax.ShapeDtypeStruct((nj * ni * bps, nh, hd), F32)],
        compiler_params=_cparams("arbitrary", "arbitrary"),
        name="inproj_means",
    )(page_ids, xb, w, b, cache4)


def _gates_kernel(x_ref, wc_ref, wr_ref, bc_ref, br_ref, xb_ref, gc_ref, gr_ref):
    xb = x_ref[...].astype(BF16)
    xb_ref[...] = xb
    gc_ref[...] = jnp.dot(xb, wc_ref[...], preferred_element_type=F32) + bc_ref[...]
    gr_ref[...] = lax.dot_general(wr_ref[...], xb, _NT, preferred_element_type=F32) + br_ref[...]


def _gates(x, wc, wr, bc, br, tm):
    m, k = x.shape
    return pl.pallas_call(
        _gates_kernel,
        grid=(m // tm,),
        in_specs=[pl.BlockSpec((tm, k), lambda i: (i, 0)),
                  pl.BlockSpec((k, LANES), lambda i: (0, 0)),
                  pl.BlockSpec((GATE_ROWS, k), lambda i: (0, 0)),
                  pl.BlockSpec((1, LANES), lambda i: (0, 0)),
                  pl.BlockSpec((GATE_ROWS, 1), lambda i: (0, 0))],
        out_specs=[pl.BlockSpec((tm, k), lambda i: (i, 0)),
                   pl.BlockSpec((tm, LANES), lambda i: (i, 0)),
                   pl.BlockSpec((GATE_ROWS, tm), lambda i: (0, i))],
        out_shape=[jax.ShapeDtypeStruct((m, k), BF16),
                   jax.ShapeDtypeStruct((m, LANES), F32),
                   jax.ShapeDtypeStruct((GATE_ROWS, m), F32)],
        compiler_params=_cparams("parallel"),
        name="gates",
    )(x, wc, wr, bc, br)


def _rotate(x, cos, sin):
    return x * cos + pltpu.roll(x, B_HEAD_DIM // 2, 1) * sin


def _rope_kernel(q_ref, k_ref, v_ref, cos_ref, sin_ref, qt_ref, ko_ref, kb_ref, vo_ref, vt_ref, km_ref, *, scale):
    cos = cos_ref[...]
    sin = sin_ref[...]
    tm, d = q_ref.shape
    for h in range(d // B_HEAD_DIM):
        sl = slice(h * B_HEAD_DIM, (h + 1) * B_HEAD_DIM)
        qr = _rotate(q_ref[:, sl], cos, sin)
        kr = _rotate(k_ref[:, sl], cos, sin)
        v = v_ref[:, sl]
        qt_ref[sl, :] = (qr * scale).T.astype(BF16)
        ko_ref[:, sl] = kr
        kb_ref[:, sl] = kr.astype(BF16)
        km_ref[:, sl] = jnp.sum(kr, axis=0, keepdims=True) * (1.0 / tm)
        vo_ref[:, sl] = v
        vt_ref[sl, :] = v.T.astype(BF16)


def _rope(h, col_q, col_k, col_v, d, cos, sin, scale):
    m = h.shape[0]
    tm = MOBA_BLOCK
    nt = m // tm
    row = lambda c: pl.BlockSpec((tm, d), lambda i: (i, c))
    full = pl.BlockSpec((tm, d), lambda i: (i, 0))
    tab = pl.BlockSpec((tm, B_HEAD_DIM), lambda i: (i, 0))
    tr = pl.BlockSpec((None, d, tm), lambda i: (i, 0, 0))
    return pl.pallas_call(
        functools.partial(_rope_kernel, scale=scale),
        grid=(nt,),
        in_specs=[row(col_q), row(col_k), row(col_v), tab, tab],
        out_specs=[tr, full, full, full, tr, pl.BlockSpec((None, 1, d), lambda i: (i, 0, 0))],
        out_shape=[jax.ShapeDtypeStruct((nt, d, tm), BF16),
                   jax.ShapeDtypeStruct((m, d), F32),
                   jax.ShapeDtypeStruct((m, d), BF16),
                   jax.ShapeDtypeStruct((m, d), F32),
                   jax.ShapeDtypeStruct((nt, d, tm), BF16),
                   jax.ShapeDtypeStruct((nt, 1, d), F32)],
        compiler_params=_cparams("parallel"),
        name="rope",
    )(h, h, h, cos, sin)


def _rope_rows_kernel(q_ref, k_ref, v_ref, cos_ref, sin_ref, qo_ref, ko_ref, vo_ref):
    cos = cos_ref[...]
    sin = sin_ref[...]
    for h in range(q_ref.shape[1] // B_HEAD_DIM):
        sl = slice(h * B_HEAD_DIM, (h + 1) * B_HEAD_DIM)
        qo_ref[:, sl] = _rotate(q_ref[:, sl], cos, sin)
        ko_ref[:, sl] = _rotate(k_ref[:, sl], cos, sin)
    vo_ref[...] = v_ref[...]


def _rope_rows(h, col_q, col_k, col_v, d, cos, sin):
    m = h.shape[0]
    row = lambda c: pl.BlockSpec((m, d), lambda i: (0, c))
    full = pl.BlockSpec((m, d), lambda i: (0, 0))
    tab = pl.BlockSpec((m, B_HEAD_DIM), lambda i: (0, 0))
    return pl.pallas_call(
        _rope_rows_kernel,
        grid=(1,),
        in_specs=[row(col_q), row(col_k), row(col_v), tab, tab],
        out_specs=[full, full, full],
        out_shape=[jax.ShapeDtypeStruct((m, d), F32)] * 3,
        compiler_params=_cparams("arbitrary"),
        name="rope_rows",
    )(h, h, h, cos, sin)


def _mlstm_kernel(pt_ref, q_ref, k_ref, v_ref, oa_ref, za_ref, gc_ref, gr_ref, ng_ref, ck_hbm,
                  ya_ref, c_out, n_out, m_out, cm_ref, c_sc, n_sc, m_sc, pbuf, sems, *, scale, job):
    c = pl.program_id(1)
    nc = pl.num_programs(1)
    slot = _means_fetch(pt_ref, ck_hbm, pbuf, sems, pl.program_id(0) * nc + c, pl.num_programs(0) * nc, job)
    _means_reduce(pbuf, slot, cm_ref, job)
    L = q_ref.shape[0]
    H = A_HEADS
    D = q_ref.shape[1] // H
    heads = range(H)
    hsl = [slice(h * D, (h + 1) * D) for h in heads]

    @pl.when(c == 0)
    def _():
        c_sc[...] = jnp.zeros(c_sc.shape, F32)
        n_sc[...] = jnp.zeros(n_sc.shape, F32)
        m_sc[...] = jnp.zeros(m_sc.shape, F32)

    gc = gc_ref[...]
    gr = gr_ref[...]
    t_i = lax.broadcasted_iota(jnp.int32, (L, L), 0)
    s_i = lax.broadcasted_iota(jnp.int32, (L, L), 1)
    causal = s_i <= t_i
    def head_gates(h):
        ig_col, fa_col = gc[:, h:h + 1], gc[:, h + H:h + H + 1]
        ig_row, fa_row = gr[h:h + 1, :], gr[h + H:h + H + 1, :]
        lf_col = _log_sigmoid(fa_col)
        lf_row = _log_sigmoid(fa_row)
        b_col = jnp.sum(jnp.where(causal, jnp.broadcast_to(lf_row, (L, L)), 0.0), axis=1, keepdims=True)
        b_row = jnp.sum(jnp.where(t_i <= s_i, jnp.broadcast_to(lf_col, (L, L)), 0.0), axis=0, keepdims=True)
        a_row = ig_row - b_row
        a_col = ig_col - b_col
        cm_col = jnp.max(jnp.where(causal, jnp.broadcast_to(a_row, (L, L)), -jnp.inf), axis=1, keepdims=True)
        m_prev = m_sc[h]
        g_col = jnp.maximum(m_prev, cm_col)
        g_last = jnp.maximum(m_prev, jnp.max(a_row, axis=1, keepdims=True))
        return dict(
            dmat=jnp.exp(jnp.where(causal, a_row - g_col, -jnp.inf)),
            inter=jnp.exp(m_prev - g_col),
            inv_floor=jnp.exp(-(b_col + g_col)),
            m_last=jnp.sum(lf_row, axis=1, keepdims=True) + g_last,
            w_col=jnp.exp(a_col - g_last),
            decay=jnp.exp(m_prev - g_last))

    def first_matmuls(h):
        q = q_ref[:, hsl[h]]
        ks = k_ref[:, hsl[h]] * scale
        qb = q.astype(BF16)
        s_raw = lax.dot_general(qb, ks.astype(BF16), _NT, preferred_element_type=F32)
        cq = lax.dot_general(qb, c_sc[h].astype(BF16), _NT, preferred_element_type=F32)
        return head_gates(h), q, ks, s_raw, cq

    def rest(h, g, q, ks, s_raw, cq):
        v = v_ref[:, hsl[h]]
        s = s_raw * g["dmat"]
        num = g["inter"] * cq + jnp.dot(s.astype(BF16), v.astype(BF16), preferred_element_type=F32)
        nvec = n_sc[h]
        den = g["inter"] * jnp.sum(q * nvec, axis=1, keepdims=True) + jnp.sum(s, axis=1, keepdims=True)
        hh = num / jnp.maximum(jnp.abs(den), g["inv_floor"])
        ya = (_layernorm(hh) * ng_ref[:, hsl[h]] * jax.nn.sigmoid(oa_ref[:, hsl[h]])
              * _silu(za_ref[:, hsl[h]]))
        ya_ref[:, hsl[h]] = ya.astype(ya_ref.dtype)
        kw = ks * g["w_col"]
        c_sc[h] = g["decay"] * c_sc[h] + jnp.dot(v.T.astype(BF16), kw.astype(BF16), preferred_element_type=F32)
        n_sc[h] = g["decay"] * nvec + jnp.sum(kw, axis=0, keepdims=True)
        m_sc[h] = g["m_last"]

    ahead = first_matmuls(0)
    for h in heads:
        cur = ahead
        if h + 1 < H:
            ahead = first_matmuls(h + 1)
        rest(h, *cur)

    @pl.when(c == nc - 1)
    def _():
        c_out[...] = c_sc[...]
        n_out[...] = n_sc[...]
        for h in heads:
            m_out[h] = jnp.broadcast_to(m_sc[h], m_out.shape[1:])


def _mlstm_steps(bsz, t):
    L = t if t <= A_CHUNK else math.gcd(t, A_CHUNK)
    return L, bsz * (t // L)


def _mlstm_prompt(h, gc, gr, ng, bsz, t, d, page_ids, cache4, first_block, blocks_per_step):
    da = d // A_HEADS
    L, n_steps = _mlstm_steps(bsz, t)
    nc = t // L
    H = A_HEADS
    bps = blocks_per_step
    job = _means_job(page_ids, cache4, first_block)
    blk = lambda g: pl.BlockSpec((L, d), lambda b, c, pt: (b * nc + c, g))
    return pl.pallas_call(
        functools.partial(_mlstm_kernel, scale=da ** -0.5, job=job),
        grid_spec=pltpu.PrefetchScalarGridSpec(
            num_scalar_prefetch=1,
            grid=(bsz, nc),
            in_specs=[blk(0), blk(1), blk(2), blk(3), blk(4),
                      pl.BlockSpec((L, LANES), lambda b, c, pt: (b * nc + c, 0)),
                      pl.BlockSpec((GATE_ROWS, L), lambda b, c, pt: (0, b * nc + c)),
                      pl.BlockSpec((1, d), lambda b, c, pt: (0, 0)),
                      pl.BlockSpec(memory_space=pl.ANY)],
            out_specs=[pl.BlockSpec((L, d), lambda b, c, pt: (b * nc + c, 0)),
                       pl.BlockSpec((None, H, da, da), lambda b, c, pt: (b, 0, 0, 0)),
                       pl.BlockSpec((None, H, 1, da), lambda b, c, pt: (b, 0, 0, 0)),
                       pl.BlockSpec((None, H, 1, LANES), lambda b, c, pt: (b, 0, 0, 0)),
                       pl.BlockSpec((bps,) + cache4.shape[2:], lambda b, c, pt: (b * nc + c, 0, 0))],
            scratch_shapes=([pltpu.VMEM((H, da, da), F32), pltpu.VMEM((H, 1, da), F32), pltpu.VMEM((H, 1, 1), F32)]
                            + _means_scratch(cache4, job, bps))),
        out_shape=[jax.ShapeDtypeStruct((bsz * t, d), BF16),
                   jax.ShapeDtypeStruct((bsz, H, da, da), F32),
                   jax.ShapeDtypeStruct((bsz, H, 1, da), F32),
                   jax.ShapeDtypeStruct((bsz, H, 1, LANES), F32),
                   jax.ShapeDtypeStruct((n_steps * bps,) + cache4.shape[2:], F32)],
        compiler_params=_cparams("arbitrary", "arbitrary"),
        name="mlstm_prompt",
    )(page_ids, h, h, h, h, h, gc, gr, ng, cache4)


def _moba_kernel(pt_ref, qt_ref, k_ref, vt_ref, km_ref, z_ref, ck_hbm, o_ref, cm_ref,
                 mem_sc, acc_sc, pbuf, sems, *, job):
    i = pl.program_id(2)
    step = (pl.program_id(0) * pl.num_programs(1) + pl.program_id(1)) * pl.num_programs(2) + i
    n_steps = pl.num_programs(0) * pl.num_programs(1) * pl.num_programs(2)
    slot = _means_fetch(pt_ref, ck_hbm, pbuf, sems, step, n_steps, job)
    _means_reduce(pbuf, slot, cm_ref, job)

    blk = qt_ref.shape[1]
    nb = km_ref.shape[0]
    hd = B_HEAD_DIM
    heads = range(qt_ref.shape[0] // hd)
    hsl = [slice(h * hd, (h + 1) * hd) for h in heads]
    qts = [qt_ref[sl, :] for sl in hsl]

    row = lax.broadcasted_iota(jnp.int32, (nb, blk), 0)
    past = row < i
    key_i = lax.broadcasted_iota(jnp.int32, (blk, blk), 0)
    qry_i = lax.broadcasted_iota(jnp.int32, (blk, blk), 1)
    own = pl.ds(pl.multiple_of(i * blk, blk), blk)
    gates = [jnp.dot(km_ref[:, hsl[h]].astype(BF16), qts[h], preferred_element_type=F32) for h in heads]
    s_own = [jnp.dot(k_ref[own, hsl[h]], qts[h], preferred_element_type=F32) for h in heads]
    for h in heads:
        g = jnp.where(past, gates[h], -jnp.inf)
        rank = jnp.zeros(g.shape, F32)
        for n2 in range(nb):
            g2 = g[n2:n2 + 1, :]
            beats = (g2 > g) | ((g2 == g) & (n2 < row))
            rank = rank + jnp.where(beats, 1.0, 0.0)
        mem_sc[h] = jnp.where(past & (rank < MOBA_TOPK), 1.0, 0.0)
    m0, l0, p_own = [], [], []
    for h in heads:
        s = jnp.where(key_i <= qry_i, s_own[h], -jnp.inf)
        m = jnp.max(s, axis=0, keepdims=True)
        p = jnp.exp2(s - m)
        m0.append(m)
        l0.append(jnp.sum(p, axis=0, keepdims=True))
        p_own.append(p.astype(BF16))
    for h in heads:
        acc_sc[h] = jnp.dot(vt_ref[i, hsl[h], :], p_own[h], preferred_element_type=F32)

    def body(jj, carry):
        ms, ls = carry
        j0 = 2 * jj
        rows0 = pl.ds(pl.multiple_of(j0 * blk, blk), blk)
        rows1 = pl.ds(pl.multiple_of(j0 * blk + blk, blk), blk)
        new_m, new_l = [], []
        ss = []
        for h in heads:
            s0 = jnp.dot(k_ref[rows0, hsl[h]], qts[h], preferred_element_type=F32)
            s1 = jnp.dot(k_ref[rows1, hsl[h]], qts[h], preferred_element_type=F32)
            s0 = jnp.where(mem_sc[h, pl.ds(j0, 1), :] > 0.0, s0, -jnp.inf)
            s1 = jnp.where(mem_sc[h, pl.ds(j0 + 1, 1), :] > 0.0, s1, -jnp.inf)
            ss.append((s0, s1))
        ps = []
        for h in heads:
            s0, s1 = ss[h]
            m_new = jnp.maximum(ms[h], jnp.maximum(jnp.max(s0, axis=0, keepdims=True),
                                                   jnp.max(s1, axis=0, keepdims=True)))
            alpha = jnp.exp2(ms[h] - m_new)
            p0 = jnp.exp2(s0 - m_new)
            p1 = jnp.exp2(s1 - m_new)
            new_l.append(alpha * ls[h] + jnp.sum(p0, axis=0, keepdims=True) + jnp.sum(p1, axis=0, keepdims=True))
            new_m.append(m_new)
            ps.append((alpha, p0.astype(BF16), p1.astype(BF16)))
        for h in heads:
            alpha, p0, p1 = ps[h]
            acc_sc[h] = (alpha * acc_sc[h]
                         + jnp.dot(vt_ref[j0, hsl[h], :], p0, preferred_element_type=F32)
                         + jnp.dot(vt_ref[j0 + 1, hsl[h], :], p1, preferred_element_type=F32))
        return tuple(new_m), tuple(new_l)

    _, l_fin = lax.fori_loop(0, (i + 1) // 2, body, (tuple(m0), tuple(l0)))
    outs = [(acc_sc[h] / l_fin[h]).T for h in heads]
    for h in heads:
        o_ref[:, hsl[h]] = (outs[h] * _silu(z_ref[:, hsl[h]])).astype(o_ref.dtype)


def _moba_steps(bsz, t, d):
    nh = d // B_HEAD_DIM
    hps = MOBA_HEADS_PER_STEP if nh % MOBA_HEADS_PER_STEP == 0 else 1
    return hps, bsz * (nh // hps) * (t // MOBA_BLOCK)


def _moba_prompt(qt, kb, vt, kmean, h, col_z, bsz, t, d, page_ids, cache4, first_block, blocks_per_step):
    nh = d // B_HEAD_DIM
    nb = t // MOBA_BLOCK
    blk = MOBA_BLOCK
    hps, n_steps = _moba_steps(bsz, t, d)
    w = hps * B_HEAD_DIM
    ng = nh // hps
    bps = blocks_per_step
    job = _means_job(page_ids, cache4, first_block)
    return pl.pallas_call(
        functools.partial(_moba_kernel, job=job),
        grid_spec=pltpu.PrefetchScalarGridSpec(
            num_scalar_prefetch=1,
            grid=(bsz, ng, nb),
            in_specs=[pl.BlockSpec((None, w, blk), lambda b, g, i, pt: (b * nb + i, g, 0)),
                      pl.BlockSpec((t, w), lambda b, g, i, pt: (b, g)),
                      pl.BlockSpec((nb, w, blk), lambda b, g, i, pt: (b, g, 0)),
                      pl.BlockSpec((None, nb, w), lambda b, g, i, pt: (b, 0, g)),
                      pl.BlockSpec((blk, w), lambda b, g, i, pt: (b * nb + i, col_z * ng + g)),
                      pl.BlockSpec(memory_space=pl.ANY)],
            out_specs=[pl.BlockSpec((blk, w), lambda b, g, i, pt: (b * nb + i, g)),
                       pl.BlockSpec((bps, nh, B_HEAD_DIM), lambda b, g, i, pt: ((b * ng + g) * nb + i, 0, 0))],
            scratch_shapes=([pltpu.VMEM((hps, nb, blk), F32), pltpu.VMEM((hps, B_HEAD_DIM, blk), F32)]
                            + _means_scratch(cache4, job, bps))),
        out_shape=[jax.ShapeDtypeStruct((bsz * t, d), BF16),
                   jax.ShapeDtypeStruct((n_steps * bps, nh, B_HEAD_DIM), F32)],
        compiler_params=_cparams("arbitrary", "arbitrary", "arbitrary"),
        name="moba_prompt",
    )(page_ids, qt, kb, vt, kmean, h, cache4)


def _merge_kernel(ya_ref, yb_ref, ga_ref, gb_ref, wa_ref, wb_ref, o_ref):
    pa = jnp.dot(ya_ref[...].astype(BF16), wa_ref[...], preferred_element_type=F32)
    pb = jnp.dot(yb_ref[...].astype(BF16), wb_ref[...], preferred_element_type=F32)
    o_ref[...] = (jax.nn.sigmoid(ga_ref[...]) * pa + jax.nn.sigmoid(gb_ref[...]) * pb).astype(o_ref.dtype)


def _merge(ya, yb, h, col_ga, col_gb, wa, wb, tm, tn):
    m, d = ya.shape
    nj = d // tn
    return pl.pallas_call(
        _merge_kernel,
        grid=(m // tm, nj),
        in_specs=[pl.BlockSpec((tm, d), lambda i, j: (i, 0)),
                  pl.BlockSpec((tm, d), lambda i, j: (i, 0)),
                  pl.BlockSpec((tm, tn), lambda i, j: (i, col_ga * nj + j)),
                  pl.BlockSpec((tm, tn), lambda i, j: (i, col_gb * nj + j)),
                  pl.BlockSpec((d, tn), lambda i, j: (0, j)),
                  pl.BlockSpec((d, tn), lambda i, j: (0, j))],
        out_specs=pl.BlockSpec((tm, tn), lambda i, j: (i, j)),
        out_shape=jax.ShapeDtypeStruct((m, d), BF16),
        compiler_params=_cparams("parallel", "arbitrary"),
        name="merge",
    )(ya, yb, h, h, wa, wb)


def _out_kernel(mg_ref, wo_ref, x_ref, g_ref, b_ref, y_ref, *, alpha):
    tm = mg_ref.shape[0]
    strip = tm // 4 if tm % 32 == 0 else tm
    rows = [slice(r, r + strip) for r in range(0, tm, strip)]
    outs = [jnp.dot(mg_ref[rs, :], wo_ref[...], preferred_element_type=F32) for rs in rows]
    for rs, out in zip(rows, outs):
        y_ref[rs, :] = _layernorm(alpha * x_ref[rs, :] + out) * g_ref[...] + b_ref[...]


def _out(mg, wo, x, g, b, tm, alpha):
    m, d = x.shape
    row = pl.BlockSpec((tm, d), lambda i: (i, 0))
    vec = pl.BlockSpec((1, d), lambda i: (0, 0))
    return pl.pallas_call(
        functools.partial(_out_kernel, alpha=alpha),
        grid=(m // tm,),
        in_specs=[row, pl.BlockSpec((d, d), lambda i: (0, 0), pipeline_mode=pl.Buffered(1)), row, vec, vec],
        out_specs=row,
        out_shape=jax.ShapeDtypeStruct((m, d), F32),
        compiler_params=_cparams("parallel"),
        name="outproj",
    )(mg, wo, x, g, b)


def _mlstm_step_kernel(q_ref, k_ref, v_ref, oa_ref, za_ref, gc_ref, ng_ref, c_ref, n_ref, m_ref,
                       ya_ref, c_out, n_out, m_out, *, scale):
    H = A_HEADS
    da = q_ref.shape[1] // H
    gc = gc_ref[...]
    eye = lax.broadcasted_iota(jnp.int32, (da, da), 0) == lax.broadcasted_iota(jnp.int32, (da, da), 1)
    for h in range(H):
        sl = slice(h * da, (h + 1) * da)
        ig = gc[:, h:h + 1]
        lf = _log_sigmoid(gc[:, h + H:h + H + 1])
        m_prev = m_ref[h][:, :1]
        a = ig - lf
        g = jnp.maximum(m_prev, a)
        m_t = lf + g
        inter = jnp.exp(m_prev - g)
        dm = jnp.exp(a - g)

        q = q_ref[:, sl]
        ks = k_ref[:, sl] * scale
        v = v_ref[:, sl]
        cmat = c_ref[h]
        nvec = n_ref[h]
        q8 = jnp.broadcast_to(q, (8, da)).astype(BF16)
        cq = lax.dot_general(q8, cmat.astype(BF16), _NT, preferred_element_type=F32)[0:1]
        s = jnp.sum(q * ks, axis=1, keepdims=True) * dm
        num = inter * cq + s * v
        den = inter * jnp.sum(nvec * q, axis=1, keepdims=True) + s
        hh = num / jnp.maximum(jnp.abs(den), jnp.exp(-m_t))

        v_col = jnp.sum(jnp.where(eye, jnp.broadcast_to(v, (da, da)), 0.0), axis=1, keepdims=True)
        c_out[h] = inter * cmat + (dm * v_col) * ks
        n_out[h] = inter * nvec + dm * ks
        m_out[h] = jnp.broadcast_to(m_t, m_out.shape[1:])
        ya = _layernorm(hh) * ng_ref[:, sl] * jax.nn.sigmoid(oa_ref[:, sl]) * _silu(za_ref[:, sl])
        ya_ref[:, sl] = ya.astype(ya_ref.dtype)


def _mlstm_step(h3, gc3, ng, c_all, c_row0, n0, m0, d):
    bs = h3.shape[0]
    H = A_HEADS
    da = d // H
    row = lambda g: pl.BlockSpec((None, 1, d), lambda b: (b, 0, g))
    st = lambda r, c: pl.BlockSpec((None, H, r, c), lambda b: (b, 0, 0, 0))
    return pl.pallas_call(
        functools.partial(_mlstm_step_kernel, scale=da ** -0.5),
        grid=(bs,),
        in_specs=[row(0), row(1), row(2), row(3), row(4),
                  pl.BlockSpec((None, 1, LANES), lambda b: (b, 0, 0)),
                  pl.BlockSpec((1, d), lambda b: (0, 0)),
                  pl.BlockSpec((None, H, da, da), lambda b: (c_row0 + b, 0, 0, 0)),
                  st(1, da), st(1, LANES)],
        out_specs=[pl.BlockSpec((None, 1, d), lambda b: (b, 0, 0)),
                   st(da, da), st(1, da), st(1, LANES)],
        out_shape=[jax.ShapeDtypeStruct((bs, 1, d), F32),
                   jax.ShapeDtypeStruct((bs, H, da, da), F32),
                   jax.ShapeDtypeStruct((bs, H, 1, da), F32),
                   jax.ShapeDtypeStruct((bs, H, 1, LANES), F32)],
        compiler_params=_cparams("parallel"),
        name="mlstm_step",
    )(h3, h3, h3, h3, h3, gc3, ng, c_all, n0, m0)


def _topk_kernel(q_ref, km_ref, o_ref):
    q = q_ref[...]
    nb = km_ref.shape[0]
    lane = lax.broadcasted_iota(jnp.int32, o_ref.shape, 1)
    gate = jnp.full(o_ref.shape, -jnp.inf, F32)
    for n in range(nb):
        gate = jnp.where(lane == n, jnp.sum(km_ref[n] * q, axis=1, keepdims=True), gate)
    rank = jnp.zeros(o_ref.shape, F32)
    for n2 in range(nb):
        g2 = gate[:, n2:n2 + 1]
        beats = (g2 > gate) | ((g2 == gate) & (n2 < lane))
        rank = rank + jnp.where(beats, 1.0, 0.0)
    lane_f = lane.astype(F32)
    out = jnp.zeros(o_ref.shape, F32)
    for r in range(MOBA_TOPK):
        idx = jnp.sum(jnp.where((rank == r) & (lane < nb), lane_f, 0.0), axis=1, keepdims=True)
        out = jnp.where(lane == r, idx, out)
    o_ref[...] = out.astype(jnp.int32)


def _topk_blocks(q4, kmean):
    bs, nb, nh, hd = kmean.shape
    assert nb <= LANES
    return pl.pallas_call(
        _topk_kernel,
        grid=(bs,),
        in_specs=[pl.BlockSpec((None, nh, hd), lambda b: (b, 0, 0)),
                  pl.BlockSpec((None, nb, nh, hd), lambda b: (b, 0, 0, 0))],
        out_specs=pl.BlockSpec((None, nh, LANES), lambda b: (b, 0, 0)),
        out_shape=jax.ShapeDtypeStruct((bs, nh, LANES), jnp.int32),
        compiler_params=_cparams("parallel"),
        name="topk_blocks",
    )(q4, kmean)


def _decode_attn_kernel(pt_ref, ix_ref, q_ref, kn_ref, vn_ref, z_ref, ck_hbm, cv_hbm, o_ref,
                        kbuf, vbuf, sems, *, per_blk, scale):
    b = pl.program_id(0)
    nbat = pl.num_programs(0)
    nh = q_ref.shape[0]
    n_pages = MOBA_TOPK * per_blk

    def page_copies(bb, slot, h, p):
        r, half = divmod(p, per_blk)
        pid = pt_ref[bb, ix_ref[bb, h * MOBA_TOPK + r] * per_blk + half]
        return (pltpu.make_async_copy(ck_hbm.at[pid, :, h, :], kbuf.at[slot, h, p], sems.at[0, slot]),
                pltpu.make_async_copy(cv_hbm.at[pid, :, h, :], vbuf.at[slot, h, p], sems.at[1, slot]))

    def for_each_copy(bb, slot, fn):
        def per_head(h, carry):
            for p in range(n_pages):
                for cp in page_copies(bb, slot, h, p):
                    fn(cp)
            return carry
        lax.fori_loop(0, nh, per_head, 0)

    @pl.when(b == 0)
    def _():
        for_each_copy(0, 0, lambda cp: cp.start())

    @pl.when(b + 1 < nbat)
    def _():
        for_each_copy(b + 1, (b + 1) % 2, lambda cp: cp.start())

    slot = b % 2
    for_each_copy(b, slot, lambda cp: cp.wait())

    def per_head(h, carry):
        row = pl.ds(h, 1)
        q = q_ref[row, :]
        scores = [jnp.sum(kbuf[slot, h, p] * q, axis=1, keepdims=True) * scale for p in range(n_pages)]
        s_own = jnp.sum(kn_ref[row, :] * q, axis=1, keepdims=True) * scale
        m = s_own
        for sc in scores:
            m = jnp.maximum(m, jnp.max(sc, axis=0, keepdims=True))
        e_own = jnp.exp(s_own - m)
        l = e_own
        acc = e_own * vn_ref[row, :]
        for p, sc in enumerate(scores):
            e = jnp.exp(sc - m)
            l = l + jnp.sum(e, axis=0, keepdims=True)
            acc = acc + jnp.sum(e * vbuf[slot, h, p], axis=0, keepdims=True)
        o_ref[row, :] = acc / l * _silu(z_ref[row, :])
        return carry

    lax.fori_loop(0, nh, per_head, 0, unroll=4 if nh % 4 == 0 else 1)


def _decode_attn(cache_k4, cache_v4, page_ids, idx, q4, kn4, vn4, z4):
    _, page, nh, hd = cache_k4.shape
    bs = page_ids.shape[0]
    per_blk = MOBA_BLOCK // page
    n_pages = MOBA_TOPK * per_blk
    vec = pl.BlockSpec((None, nh, hd), lambda b, pt, ix: (b, 0, 0))
    hbm = pl.BlockSpec(memory_space=pl.ANY)
    return pl.pallas_call(
        functools.partial(_decode_attn_kernel, per_blk=per_blk, scale=hd ** -0.5),
        grid_spec=pltpu.PrefetchScalarGridSpec(
            num_scalar_prefetch=2,
            grid=(bs,),
            in_specs=[vec, vec, vec, vec, hbm, hbm],
            out_specs=vec,
            scratch_shapes=[pltpu.VMEM((2, nh, n_pages, page, hd), F32),
                            pltpu.VMEM((2, nh, n_pages, page, hd), F32),
                            pltpu.SemaphoreType.DMA((2, 2))]),
        out_shape=jax.ShapeDtypeStruct((bs, nh, hd), F32),
        compiler_params=_cparams("arbitrary"),
        name="decode_attn",
    )(page_ids, idx, q4, kn4, vn4, z4, cache_k4, cache_v4)


def _rope_tables(pos):
    half = B_HEAD_DIM // 2
    freqs = ROPE_THETA ** (-jnp.arange(half, dtype=F32) / half)
    ang = pos.astype(F32)[:, None] * freqs[None, :]
    cos = jnp.cos(ang)
    sin = jnp.sin(ang)
    return jnp.concatenate([cos, cos], -1), jnp.concatenate([-sin, sin], -1)


def _tile(n, pref):
    return pref if n % pref == 0 else n


def _prep_weights(w_in_all, layer, b_in, w_proj_a, w_proj_b, w_out, d):
    c0 = 5 * d
    c1 = c0 + 2 * A_HEADS
    w_in = w_in_all[layer]
    w_a = w_in.astype(BF16)
    w_b = w_a[:, c1:]
    w_g = w_in[:, c0:c1]
    wc = jnp.pad(w_g, ((0, 0), (0, LANES - 2 * A_HEADS))).astype(BF16)
    wr = w_g.T.astype(BF16)
    bc = jnp.pad(b_in[c0:c1], (0, LANES - 2 * A_HEADS))[None, :]
    br = b_in[c0:c1][:, None]
    return ((w_a, b_in[None, :c0]), (w_b, b_in[None, c1:]),
            wc, wr, bc, br, w_proj_a.astype(BF16), w_proj_b.astype(BF16), w_out.astype(BF16))


_QA, _KA, _VA, _OA, _ZA = range(5)
_QB, _KB, _VB, _ZB, _GA, _GB = range(6)


def _project(xb, part, tm):
    w, b = part
    return _inproj(xb, w, b, tm, _tile(b.shape[1], 1024))


def _project_with_means(xb, parts, tm, cache_k4, page_ids, first_block):
    bs, n_pages = page_ids.shape
    n_blocks = bs * n_pages * cache_k4.shape[1] // MOBA_BLOCK
    tiles = [(tm, _tile(b.shape[1], 2048)) for _, b in parts]
    steps = [(xb.shape[0] // tm_) * (b.shape[1] // tn_) for (tm_, tn_), (_, b) in zip(tiles, parts)]
    bps = max(1, -(-(n_blocks - first_block) // sum(steps)))
    hs, means, first = [], [], first_block
    for (w, b), (tm_, tn_), n_steps in zip(parts, tiles, steps):
        h, km = _inproj_means(xb, w, b, tm_, tn_, page_ids, cache_k4, first, bps)
        hs.append(h)
        means.append(km)
        first += n_steps * bps
    return hs[0], hs[1], means


def _layer_prompt(x, wts, ng, ln_g, ln_b, alpha, cache_k4, page_ids):
    bsz, t, d = x.shape
    m = bsz * t
    part_a, part_b, wc, wr, bc, br, wa, wb, wo = wts
    x2 = x.reshape(m, d)
    first_mlstm = _moba_steps(bsz, t, d)[1] * MEANS_PER_MOBA_STEP
    first_proj = first_mlstm + _mlstm_steps(bsz, t)[1] * MEANS_PER_MLSTM_STEP
    xb, gc, gr = _gates(x2, wc, wr, bc, br, _tile(m, 512))
    h_a, h_b, means_proj = _project_with_means(xb, (part_a, part_b), _tile(m, 512), cache_k4, page_ids, first_proj)
    ya, c1, n1, m1, means_mlstm = _mlstm_prompt(h_a, gc, gr, ng, bsz, t, d, page_ids, cache_k4,
                                                first_mlstm, MEANS_PER_MLSTM_STEP)
    cos, sin = _rope_tables(jnp.arange(t, dtype=jnp.int32))
    cos = jnp.tile(cos, (bsz, 1))
    sin = jnp.tile(sin, (bsz, 1))
    q_t, k_new, k_b, v_new, v_t, kmean = _rope(h_b, _QB, _KB, _VB, d, cos, sin, B_HEAD_DIM ** -0.5 * LOG2E)
    nb = t // MOBA_BLOCK
    yb, means_moba = _moba_prompt(q_t, k_b, v_t, kmean.reshape(bsz, nb, d), h_b, _ZB, bsz, t, d,
                                  page_ids, cache_k4, 0, MEANS_PER_MOBA_STEP)
    bs_c, n_pages = page_ids.shape
    n_blocks = bs_c * n_pages * cache_k4.shape[1] // MOBA_BLOCK
    kmean_cache = jnp.concatenate([means_moba, means_mlstm] + means_proj)[:n_blocks]
    kmean_cache = kmean_cache.reshape((bs_c, n_blocks // bs_c) + cache_k4.shape[2:])
    mg = _merge(ya, yb, h_b, _GA, _GB, wa, wb, _tile(m, 1024), _tile(d, 512))
    y = _out(mg, wo, x2, ln_g, ln_b, _tile(m, 1024), alpha)
    nh = d // B_HEAD_DIM
    return (kmean_cache, y.reshape(bsz, t, d), k_new.reshape(bsz, t, nh, B_HEAD_DIM),
            v_new.reshape(bsz, t, nh, B_HEAD_DIM), c1, n1.reshape(bsz, A_HEADS, d // A_HEADS), m1[:, :, 0, 0])


def _layer_sample(x, wts, ng, ln_g, ln_b, alpha, cache_k4, cache_v4, page_ids, kmean, c_all, c_row0, n0, m0):
    bs, t, d = x.shape
    part_a, part_b, wc, wr, bc, br, wa, wb, wo = wts
    _, page, nh, hd = cache_k4.shape
    past = page_ids.shape[1] * page
    da = d // A_HEADS
    x2 = x.reshape(bs, d)
    xb, gc, _ = _gates(x2, wc, wr, bc, br, bs)
    h_a = _project(xb, part_a, bs)
    h_b = _project(xb, part_b, bs)
    ya, c1, n1, m1 = _mlstm_step(h_a.reshape(bs, 1, h_a.shape[1]), gc.reshape(bs, 1, LANES), ng, c_all, c_row0,
                                 n0.reshape(bs, A_HEADS, 1, da),
                                 jnp.broadcast_to(m0[:, :, None, None], (bs, A_HEADS, 1, LANES)), d)
    cos, sin = _rope_tables(jnp.full((bs,), past, dtype=jnp.int32))
    q_r, k_new, v_new = _rope_rows(h_b, _QB, _KB, _VB, d, cos, sin)
    heads = lambda a: a.reshape(bs, nh, hd)
    q4, kn4, vn4 = heads(q_r), heads(k_new), heads(v_new)
    idx = _topk_blocks(q4, kmean)[:, :, :MOBA_TOPK].reshape(bs, nh * MOBA_TOPK)
    yb = _decode_attn(cache_k4, cache_v4, page_ids, idx, q4, kn4, vn4, heads(h_b[:, _ZB * d:(_ZB + 1) * d]))
    mg = _merge(ya.reshape(bs, d), yb.reshape(bs, d), h_b, _GA, _GB, wa, wb, bs, _tile(d, 512))
    y = _out(mg, wo, x2, ln_g, ln_b, bs, alpha)
    return (y.reshape(bs, 1, d), kn4[:, None], vn4[:, None],
            c1, n1.reshape(bs, A_HEADS, da), m1[:, :, 0, 0])


def kernel(x_prompt, x_sample, cache_k, cache_v, page_table, state_mlstm_C, state_mlstm_n, state_mlstm_m,
           w_in, b_in, mh_norm_g, w_proj_a, w_proj_b, w_out, ln_g, ln_b):
    depth = w_in.shape[0]
    d = x_prompt.shape[-1]
    n_pages = page_table.shape[1]
    page = cache_k.shape[2]
    assert x_sample.shape[1] == 1 and d % (A_HEADS * LANES) == 0
    assert x_prompt.shape[1] % MOBA_BLOCK == 0 and MOBA_BLOCK % page == 0
    assert (n_pages * page) % MOBA_BLOCK == 0 and n_pages * page // MOBA_BLOCK >= MOBA_TOPK
    alpha = (2 * depth) ** 0.25
    n_phys = cache_k.shape[1]
    bs = x_sample.shape[0]
    cache_k3 = cache_k.reshape((depth * n_phys,) + cache_k.shape[2:])
    cache_v3 = cache_v.reshape((depth * n_phys,) + cache_v.shape[2:])
    c_all = state_mlstm_C.reshape((depth * bs,) + state_mlstm_C.shape[2:])
    yp, ys = x_prompt, x_sample
    new_p = [[] for _ in range(5)]
    new_s = [[] for _ in range(5)]
    for l in range(depth):
        wts = _prep_weights(w_in, l, b_in[l], w_proj_a[l], w_proj_b[l], w_out[l], d)
        ng, lg, lb = mh_norm_g[l][None, :], ln_g[l][None, :], ln_b[l][None, :]
        page_ids = page_table + l * n_phys
        kmean, yp, *st_p = _layer_prompt(yp, wts, ng, lg, lb, alpha, cache_k3, page_ids)
        ys, *st_s = _layer_sample(ys, wts, ng, lg, lb, alpha, cache_k3, cache_v3, page_ids, kmean,
                                  c_all, l * bs, state_mlstm_n[l], state_mlstm_m[l])
        for i in range(5):
            new_p[i].append(st_p[i])
            new_s[i].append(st_s[i])
    stack = lambda ts: ts[0][None] if len(ts) == 1 else jnp.stack(ts)
    k_p, v_p, c_p, n_p, m_p = [stack(t) for t in new_p]
    k_s, v_s, c_s, n_s, m_s = [stack(t) for t in new_s]
    return (yp, ys, k_p, v_p, c_p, n_p, m_p, k_s, v_s, c_s, n_s, m_s)
```
